```python
import jax, jax.numpy as jnp
from jax import lax
import numpy as np

D_MODEL = 2048
BATCH = 4
SEQ = 2048
DEPTH = 4
DEC_BATCH = 16
DEC_SEQ = 2048
PAST_LEN = 128

PLE_DIM = 256
HEAD_DIM = 128
A_HEADS = 6
A_KV_HEADS = 2
A_GROUP = A_HEADS // A_KV_HEADS
A_WINDOW = 128
A_BLOCK = 128
B_HEADS = 6
B_Q_LORA = 512
B_KV_LORA = 512
B_NOPE = 128
B_ROPE = 64
B_V = 128
B_QK = B_NOPE + B_ROPE
B_BLOCK = 128
ROPE_THETA = 10000.0
C_HEADS = 4
GRID_W = 64
NA_ROWS_MAX = 8
NA_COLS = 16
A_WIDTH = A_HEADS * HEAD_DIM
A_KV_WIDTH = A_KV_HEADS * HEAD_DIM
B_WIDTH = B_HEADS * B_V
C_WIDTH = C_HEADS * HEAD_DIM
MIX_WIDTH = A_WIDTH + B_WIDTH + C_WIDTH
IN_SPLITS = (A_WIDTH, A_KV_WIDTH, A_KV_WIDTH, A_WIDTH,
             B_Q_LORA, B_KV_LORA, B_ROPE, B_WIDTH,
             C_WIDTH, C_WIDTH, C_WIDTH, C_WIDTH)
IN_WIDTH = (2 * A_WIDTH + 2 * A_KV_WIDTH + B_Q_LORA + B_KV_LORA + B_ROPE + B_WIDTH + 4 * C_WIDTH)
EPS = 1e-6
NEG_INF = -1e30

kernel_name = "hymba_style_hybrid_encoder"


def _rmsnorm(x, g):
    xf = x.astype(jnp.float32)
    xf = xf * lax.rsqrt(jnp.mean(xf * xf, axis=-1, keepdims=True) + EPS)
    return (xf * g.astype(jnp.float32)).astype(x.dtype)


def _split(x, sizes):
    outs, off = [], 0
    for s in sizes:
        outs.append(x[..., off:off + s])
        off += s
    return outs


def _alibi_slopes(n):
    return jnp.exp2(-8.0 * jnp.arange(1, n + 1, dtype=jnp.float32) / n)


def _rope(x, pos):
    half = x.shape[-1] // 2
    inv = ROPE_THETA ** (-jnp.arange(half, dtype=jnp.float32) / half)
    ang = pos.astype(jnp.float32)[:, None] * inv[None, :]
    cos = jnp.cos(ang)[:, None, :].astype(x.dtype)
    sin = jnp.sin(ang)[:, None, :].astype(x.dtype)
    x1, x2 = x[..., :half], x[..., half:]
    return jnp.concatenate([x1 * cos - x2 * sin, x1 * sin + x2 * cos], axis=-1)


def _window_gqa(q, k, v, g_q, g_k, sink):
    B, L, _ = q.shape
    nb = L // A_BLOCK
    q = _rmsnorm(q.reshape(B, L, A_HEADS, HEAD_DIM), g_q)
    k = _rmsnorm(k.reshape(B, L, A_KV_HEADS, HEAD_DIM), g_k)
    v = v.reshape(B, L, A_KV_HEADS, HEAD_DIM)
    q = q.transpose(0, 2, 1, 3).reshape(B, A_KV_HEADS, A_GROUP, nb, A_BLOCK, HEAD_DIM)

    def band(t):
        t = jnp.pad(t.transpose(0, 2, 1, 3), ((0, 0), (0, 0), (A_BLOCK, A_BLOCK), (0, 0)))
        t = t.reshape(B, A_KV_HEADS, nb + 2, A_BLOCK, HEAD_DIM)
        return jnp.concatenate([t[:, :, :-2], t[:, :, 1:-1], t[:, :, 2:]], axis=3)

    kb, vb = band(k), band(v)
    logits = jnp.einsum('bkgnqd,bknsd->bkgnqs', q, kb).astype(jnp.float32) * (HEAD_DIM ** -0.5)
    qi = jnp.arange(A_BLOCK)
    si = jnp.arange(3 * A_BLOCK)
    delta = si[None, :] - A_BLOCK - qi[:, None]
    tk = jnp.arange(nb)[:, None] * A_BLOCK - A_BLOCK + si[None, :]
    valid = (jnp.abs(delta) <= A_WINDOW)[None] & ((tk >= 0) & (tk < L))[:, None, :]
    slopes = _alibi_slopes(A_HEADS).reshape(A_KV_HEADS, A_GROUP)
    logits = logits - slopes[:, :, None, None, None] * jnp.abs(delta).astype(jnp.float32)
    logits = jnp.where(valid, logits, NEG_INF)
    sink_f = sink.astype(jnp.float32).reshape(A_KV_HEADS, A_GROUP, 1, 1, 1)
    m = jnp.maximum(jnp.max(logits, axis=-1, keepdims=True), sink_f)
    e = jnp.exp(logits - m)
    probs = e / (jnp.sum(e, axis=-1, keepdims=True) + jnp.exp(sink_f - m))
    out = jnp.einsum('bkgnqs,bknsd->bkgnqd', probs.astype(vb.dtype), vb)
    return out.reshape(B, A_HEADS, L, HEAD_DIM).transpose(0, 2, 1, 3).reshape(B, L, A_WIDTH)


def _mla(cq, ckv, kr, g_cq, g_ckv, w_uq, w_ukv, g_q, g_k):
    B, L, _ = cq.shape
    nb = L // B_BLOCK
    pos = jnp.arange(L)
    q = (_rmsnorm(cq, g_cq) @ w_uq).reshape(B, L, B_HEADS, B_QK)
    kv = (_rmsnorm(ckv, g_ckv) @ w_ukv).reshape(B, L, B_HEADS, B_NOPE + B_V)
    k_nope, v = kv[..., :B_NOPE], kv[..., B_NOPE:]
    k = jnp.concatenate([k_nope, jnp.broadcast_to(kr[:, :, None, :], (B, L, B_HEADS, B_ROPE))], axis=-1)
    q = _rmsnorm(q, g_q)
    k = _rmsnorm(k, g_k)
    q = jnp.concatenate([q[..., :B_NOPE], _rope(q[..., B_NOPE:], pos)], axis=-1)
    k = jnp.concatenate([k[..., :B_NOPE], _rope(k[..., B_NOPE:], pos)], axis=-1)
    qb = q.transpose(0, 2, 1, 3).reshape(B, B_HEADS, nb, B_BLOCK, B_QK).transpose(2, 0, 1, 3, 4)
    k = k.transpose(0, 2, 1, 3)
    v = v.transpose(0, 2, 1, 3)
    scale = B_QK ** -0.5

    def block(qblk):
        s = jnp.einsum('bhqd,bhkd->bhqk', qblk, k).astype(jnp.float32) * scale
        p = jax.nn.softmax(s, axis=-1)
        return jnp.einsum('bhqk,bhkd->bhqd', p.astype(v.dtype), v)

    out = lax.map(block, qb)
    return out.transpose(1, 0, 3, 2, 4).reshape(B, L, B_WIDTH)


def _neighbourhood(q, k, v, g_q, g_k, rpb):
    B, L, _ = q.shape
    rows = L // GRID_W
    wr = min(NA_ROWS_MAX, rows)

    def grid(t):
        return t.reshape(B, rows, GRID_W, C_HEADS, HEAD_DIM).transpose(0, 3, 1, 2, 4)

    qg = grid(_rmsnorm(q.reshape(B, L, C_HEADS, HEAD_DIM), g_q))
    kg = grid(_rmsnorm(k.reshape(B, L, C_HEADS, HEAD_DIM), g_k))
    vg = grid(v)
    r = jnp.arange(rows)
    c = jnp.arange(GRID_W)
    rs = jnp.clip(r - wr // 2, 0, rows - wr)
    row_idx = rs[:, None] + jnp.arange(wr)[None, :]
    cs = jnp.clip(c - NA_COLS // 2, 0, GRID_W - NA_COLS)
    col_valid = (c[None, :] >= cs[:, None]) & (c[None, :] < cs[:, None] + NA_COLS)
    k_rows = kg[:, :, row_idx]
    v_rows = vg[:, :, row_idx]
    logits = jnp.einsum('bhrqd,bhrwcd->bhrqwc', qg, k_rows).astype(jnp.float32) * (HEAD_DIM ** -0.5)
    dr = row_idx - r[:, None]
    dc = c[None, :] - c[:, None]
    bias = rpb.astype(jnp.float32)[:, (dr + NA_ROWS_MAX - 1)[:, None, :, None],
                                   jnp.clip(dc + NA_COLS - 1, 0, 2 * NA_COLS - 2)[None, :, None, :]]
    logits = jnp.where(col_valid[:, None, :], logits + bias, NEG_INF)
    sh = logits.shape
    probs = jax.nn.softmax(logits.reshape(sh[:4] + (wr * GRID_W,)), axis=-1).reshape(sh)
    out = jnp.einsum('bhrqwc,bhrwcd->bhrqd', probs.astype(v_rows.dtype), v_rows)
    return out.transpose(0, 2, 3, 1, 4).reshape(B, L, C_WIDTH)


def _trunk(x, p, W):
    h = x
    for l in range(DEPTH):
        u = _rmsnorm(h, W['norm_in'][l])
        (aq, ak, av, az, bcq, bckv, bkr, bz, cq, ck, cv, cz) = _split(u @ W['w_in'][l], IN_SPLITS)
        ya = _window_gqa(aq, ak, av, W['a_q_norm'][l], W['a_k_norm'][l], W['a_sink'][l]) * jax.nn.silu(az)
        yb = _mla(bcq, bckv, bkr, W['b_cq_norm'][l], W['b_ckv_norm'][l], W['b_w_uq'][l], W['b_w_ukv'][l],
                  W['b_q_norm'][l], W['b_k_norm'][l]) * jax.nn.silu(bz)
        yc = _neighbourhood(cq, ck, cv, W['c_q_norm'][l], W['c_k_norm'][l], W['c_rpb'][l]) * jax.nn.silu(cz)
        h = h + jnp.concatenate([ya, yb, yc], axis=-1) @ W['w_out'][l]
        gate = jax.nn.sigmoid(_rmsnorm(h, W['ple_norm'][l]) @ W['w_ple_gate'][l])
        h = h + gate * _rmsnorm(p[l] @ W['w_ple_proj'][l], W['ple_post_norm'][l])
    return h


def setup_inputs(seed: int = 0) -> dict:
    key = jax.random.key(seed)
    ks = jax.random.split(key, 24)
    f32 = jnp.float32

    def nrm(k, shape, scale):
        return jax.random.normal(k, shape, f32) * scale

    def gain(k, shape):
        return 1.0 + 0.05 * jax.random.normal(k, shape, f32)

    return {
        'x_prompt': nrm(ks[0], (BATCH, SEQ, D_MODEL), 1.0),
        'x_sample': nrm(ks[1], (DEC_BATCH, DEC_SEQ, D_MODEL), 1.0),
        'p_prompt': nrm(ks[2], (DEPTH, BATCH, SEQ, PLE_DIM), 1.0),
        'p_sample': nrm(ks[3], (DEPTH, DEC_BATCH, DEC_SEQ, PLE_DIM), 1.0),
        'norm_in': gain(ks[4], (DEPTH, D_MODEL)),
        'w_in': nrm(ks[5], (DEPTH, D_MODEL, IN_WIDTH), D_MODEL ** -0.5),
        'a_q_norm': gain(ks[6], (DEPTH, HEAD_DIM)),
        'a_k_norm': gain(ks[7], (DEPTH, HEAD_DIM)),
        'a_sink': nrm(ks[8], (DEPTH, A_HEADS), 0.5),
        'b_cq_norm': gain(ks[9], (DEPTH, B_Q_LORA)),
        'b_ckv_norm': gain(ks[10], (DEPTH, B_KV_LORA)),
        'b_w_uq': nrm(ks[11], (DEPTH, B_Q_LORA, B_HEADS * B_QK), B_Q_LORA ** -0.5),
        'b_w_ukv': nrm(ks[12], (DEPTH, B_KV_LORA, B_HEADS * (B_NOPE + B_V)), B_KV_LORA ** -0.5),
        'b_q_norm': gain(ks[13], (DEPTH, B_QK)),
        'b_k_norm': gain(ks[14], (DEPTH, B_QK)),
        'c_q_norm': gain(ks[15], (DEPTH, HEAD_DIM)),
        'c_k_norm': gain(ks[16], (DEPTH, HEAD_DIM)),
        'c_rpb': nrm(ks[17], (DEPTH, C_HEADS, 2 * NA_ROWS_MAX - 1, 2 * NA_COLS - 1), 0.1),
        'w_out': nrm(ks[18], (DEPTH, MIX_WIDTH, D_MODEL), MIX_WIDTH ** -0.5),
        'ple_norm': gain(ks[19], (DEPTH, D_MODEL)),
        'w_ple_gate': nrm(ks[20], (DEPTH, D_MODEL, D_MODEL), D_MODEL ** -0.5),
        'w_ple_proj': nrm(ks[21], (DEPTH, PLE_DIM, D_MODEL), PLE_DIM ** -0.5),
        'ple_post_norm': gain(ks[22], (DEPTH, D_MODEL)),
    }


def reference(x_prompt, x_sample, p_prompt, p_sample, norm_in, w_in, a_q_norm, a_k_norm, a_sink,
              b_cq_norm, b_ckv_norm, b_w_uq, b_w_ukv, b_q_norm, b_k_norm, c_q_norm, c_k_norm, c_rpb,
              w_out, ple_norm, w_ple_gate, w_ple_proj, ple_post_norm):
    W = {
        'norm_in': norm_in, 'w_in': w_in,
        'a_q_norm': a_q_norm, 'a_k_norm': a_k_norm, 'a_sink': a_sink,
        'b_cq_norm': b_cq_norm, 'b_ckv_norm': b_ckv_norm, 'b_w_uq': b_w_uq, 'b_w_ukv': b_w_ukv,
        'b_q_norm': b_q_norm, 'b_k_norm': b_k_norm,
        'c_q_norm': c_q_norm, 'c_k_norm': c_k_norm, 'c_rpb': c_rpb,
        'w_out': w_out, 'ple_norm': ple_norm, 'w_ple_gate': w_ple_gate,
        'w_ple_proj': w_ple_proj, 'ple_post_norm': ple_post_norm,
    }
    y_prompt = _trunk(x_prompt, p_prompt, W)
    y_sample = _trunk(x_sample, p_sample, W)
    return (y_prompt, y_sample)
```

```python
import functools

import jax
import jax.numpy as jnp
import numpy as np
from jax import lax
from jax.experimental import pallas as pl
from jax.experimental.pallas import tpu as pltpu

D_MODEL = 2048
PLE_DIM = 256
HEAD_DIM = 128
A_HEADS = 6
A_KV_HEADS = 2
A_GROUP = A_HEADS // A_KV_HEADS
A_WINDOW = 128
A_BLOCK = 128
B_HEADS = 6
B_Q_LORA = 512
B_KV_LORA = 512
B_NOPE = 128
B_ROPE = 64
B_V = 128
B_QK = B_NOPE + B_ROPE
ROPE_THETA = 10000.0
C_HEADS = 4
GRID_W = 64
NA_ROWS_MAX = 8
NA_COLS = 16
A_WIDTH = A_HEADS * HEAD_DIM
A_KV_WIDTH = A_KV_HEADS * HEAD_DIM
B_WIDTH = B_HEADS * B_V
C_WIDTH = C_HEADS * HEAD_DIM
MIX_WIDTH = A_WIDTH + B_WIDTH + C_WIDTH
EPS = 1e-6
NEG_INF = -1e30

B_QK_PAD = 2 * HEAD_DIM
IN_A_Q = 0
IN_A_KV = IN_A_Q + A_WIDTH
IN_GATES = IN_A_KV + 2 * A_KV_WIDTH
IN_B_LAT = IN_GATES + MIX_WIDTH
IN_C_QKV = IN_B_LAT + B_Q_LORA + B_KV_LORA
IN_B_KR = IN_C_QKV + 3 * C_WIDTH
IN_WIDTH_PAD = IN_B_KR + 2 * B_ROPE

VMEM_LIMIT_BYTES = 56 * 1024 * 1024

_BF16 = jnp.bfloat16
_F32 = jnp.float32


def _params(n_grid_dims):
    return pltpu.CompilerParams(
        dimension_semantics=("arbitrary",) * n_grid_dims,
        vmem_limit_bytes=VMEM_LIMIT_BYTES)


def _resident(shape, index_map):
    return pl.BlockSpec(shape, index_map, pipeline_mode=pl.Buffered(1))


def _rms(x, g):
    return x * lax.rsqrt(jnp.mean(x * x, axis=-1, keepdims=True) + EPS) * g


def _dot(a, b):
    return jnp.dot(a, b, preferred_element_type=_F32)


def _dot_nt(a, b):
    return lax.dot_general(a, b, (((1,), (1,)), ((), ())), preferred_element_type=_F32)


def _in_proj_kernel(h_ref, gin_ref, w_ref, gaq_ref, gak_ref, gbq_ref, gbkv_ref, gcq_ref, gck_ref,
                    aq_ref, akv_ref, gates_ref, blat_ref, cqkv_ref, bkr_ref):
    u = _rms(h_ref[...], gin_ref[...]).astype(_BF16)
    scale = HEAD_DIM ** -0.5

    def seg(start, width):
        return _dot(u, w_ref[:, start:start + width])

    def heads(x, g, n, s=None):
        outs = []
        for i in range(n):
            y = _rms(x[:, i * HEAD_DIM:(i + 1) * HEAD_DIM], g)
            outs.append(y if s is None else y * s)
        return outs

    x = seg(IN_A_Q, A_WIDTH)
    for i, y in enumerate(heads(x, gaq_ref[...], A_HEADS, scale)):
        aq_ref[:, i * HEAD_DIM:(i + 1) * HEAD_DIM] = y.astype(_BF16)

    x = seg(IN_A_KV, 2 * A_KV_WIDTH)
    for i, y in enumerate(heads(x[:, :A_KV_WIDTH], gak_ref[...], A_KV_HEADS)):
        akv_ref[:, i * HEAD_DIM:(i + 1) * HEAD_DIM] = y.astype(_BF16)
    akv_ref[:, A_KV_WIDTH:] = x[:, A_KV_WIDTH:].astype(_BF16)

    chunk = 512
    for c in range(MIX_WIDTH // chunk):
        z = seg(IN_GATES + c * chunk, chunk)
        gates_ref[:, c * chunk:(c + 1) * chunk] = (z * jax.nn.sigmoid(z)).astype(_BF16)

    x = seg(IN_B_LAT, B_Q_LORA + B_KV_LORA)
    blat_ref[:, :B_Q_LORA] = _rms(x[:, :B_Q_LORA], gbq_ref[...]).astype(_BF16)
    blat_ref[:, B_Q_LORA:] = _rms(x[:, B_Q_LORA:], gbkv_ref[...]).astype(_BF16)

    x = seg(IN_C_QKV, 3 * C_WIDTH)
    for i, y in enumerate(heads(x[:, :C_WIDTH], gcq_ref[...], C_HEADS, scale)):
        cqkv_ref[:, i * HEAD_DIM:(i + 1) * HEAD_DIM] = y.astype(_BF16)
    for i, y in enumerate(heads(x[:, C_WIDTH:2 * C_WIDTH], gck_ref[...], C_HEADS)):
        cqkv_ref[:, C_WIDTH + i * HEAD_DIM:C_WIDTH + (i + 1) * HEAD_DIM] = y.astype(_BF16)
    cqkv_ref[:, 2 * C_WIDTH:] = x[:, 2 * C_WIDTH:].astype(_BF16)

    bkr_ref[...] = seg(IN_B_KR, 2 * B_ROPE)


def _in_proj(h, lw, layer, tm):
    t = h.shape[0]
    row = lambda i: (i, 0)
    vec = lambda n: pl.BlockSpec((None, 1, n), lambda i: (layer, 0, 0))
    widths = (A_WIDTH, 2 * A_KV_WIDTH, MIX_WIDTH, B_Q_LORA + B_KV_LORA, 3 * C_WIDTH)
    out_shape = [jax.ShapeDtypeStruct((t, w), _BF16) for w in widths]
    out_shape.append(jax.ShapeDtypeStruct((t, 2 * B_ROPE), _F32))
    out_specs = [pl.BlockSpec((tm, w), row) for w in widths] + [pl.BlockSpec((tm, 2 * B_ROPE), row)]
    return pl.pallas_call(
        _in_proj_kernel,
        grid=(t // tm,),
        in_specs=[
            pl.BlockSpec((tm, D_MODEL), row),
            vec(D_MODEL),
            _resident((None, D_MODEL, IN_WIDTH_PAD), lambda i: (layer, 0, 0)),
            vec(HEAD_DIM), vec(HEAD_DIM), vec(B_Q_LORA), vec(B_KV_LORA), vec(HEAD_DIM), vec(HEAD_DIM),
        ],
        out_specs=out_specs,
        out_shape=out_shape,
        compiler_params=_params(1),
        name="in_proj",
    )(h, lw['norm_in'], lw['w_in'], lw['a_q_norm'], lw['a_k_norm'], lw['b_cq_norm'], lw['b_ckv_norm'],
      lw['c_q_norm'], lw['c_k_norm'])


def _mla_prep_kernel(blat_ref, bkr_ref, wq_ref, wkv_ref, gq_ref, gkn_ref, gkr_ref, cos_ref, sin_ref,
                     q_ref, k_ref, v_ref):
    cqn = blat_ref[:, :B_Q_LORA]
    ckvn = blat_ref[:, B_Q_LORA:]
    cos2 = cos_ref[...]
    sin2 = sin_ref[...]
    lane = lax.broadcasted_iota(jnp.int32, (1, HEAD_DIM), 1)
    rope_lanes = lane < B_ROPE
    scale = B_QK ** -0.5

    def rope(c):
        return c * cos2 + pltpu.roll(c, B_ROPE, 1) * sin2

    kr2 = bkr_ref[...]
    ss_kr = jnp.sum(jnp.where(rope_lanes, kr2 * kr2, 0.0), axis=-1, keepdims=True)
    k_rope = rope(kr2 * gkr_ref[...])

    for hd in range(B_HEADS):
        x = _dot(cqn, wq_ref[:, hd * B_QK_PAD:(hd + 1) * B_QK_PAD])
        a = x[:, :HEAD_DIM]
        c = x[:, HEAD_DIM:]
        ss = jnp.sum(a * a, axis=-1, keepdims=True) + jnp.sum(
            jnp.where(rope_lanes, c * c, 0.0), axis=-1, keepdims=True)
        r = lax.rsqrt(ss * (1.0 / B_QK) + EPS)
        q_ref[:, hd * B_QK_PAD:hd * B_QK_PAD + HEAD_DIM] = (
            a * r * gq_ref[:, :HEAD_DIM] * scale).astype(_BF16)
        q_ref[:, hd * B_QK_PAD + HEAD_DIM:(hd + 1) * B_QK_PAD] = (
            rope(c * r * gq_ref[:, HEAD_DIM:]) * scale).astype(_BF16)

        kv = _dot(ckvn, wkv_ref[:, hd * 2 * HEAD_DIM:(hd + 1) * 2 * HEAD_DIM])
        kn = kv[:, :B_NOPE]
        ss = jnp.sum(kn * kn, axis=-1, keepdims=True) + ss_kr
        r = lax.rsqrt(ss * (1.0 / B_QK) + EPS)
        k_ref[:, hd * B_QK_PAD:hd * B_QK_PAD + HEAD_DIM] = (kn * r * gkn_ref[...]).astype(_BF16)
        k_ref[:, hd * B_QK_PAD + HEAD_DIM:(hd + 1) * B_QK_PAD] = (k_rope * r).astype(_BF16)
        v_ref[:, hd * B_V:(hd + 1) * B_V] = kv[:, B_NOPE:].astype(_BF16)


def _mla_prep(blat, bkr, lw, tables, layer, seq, tm):
    t = blat.shape[0]
    row = lambda i: (i, 0)
    pos = lambda i: (i % (seq // tm), 0)
    vec = lambda n: pl.BlockSpec((None, 1, n), lambda i: (layer, 0, 0))
    return pl.pallas_call(
        _mla_prep_kernel,
        grid=(t // tm,),
        in_specs=[
            pl.BlockSpec((tm, B_Q_LORA + B_KV_LORA), row),
            pl.BlockSpec((tm, 2 * B_ROPE), row),
            _resident((None, B_Q_LORA, B_HEADS * B_QK_PAD), lambda i: (layer, 0, 0)),
            _resident((None, B_KV_LORA, B_HEADS * (B_NOPE + B_V)), lambda i: (layer, 0, 0)),
            vec(B_QK_PAD), vec(HEAD_DIM), vec(2 * B_ROPE),
            pl.BlockSpec((tm, 2 * B_ROPE), pos),
            pl.BlockSpec((tm, 2 * B_ROPE), pos),
        ],
        out_specs=[
            pl.BlockSpec((tm, B_HEADS * B_QK_PAD), row),
            pl.BlockSpec((tm, B_HEADS * B_QK_PAD), row),
            pl.BlockSpec((tm, B_WIDTH), row),
        ],
        out_shape=[
            jax.ShapeDtypeStruct((t, B_HEADS * B_QK_PAD), _BF16),
            jax.ShapeDtypeStruct((t, B_HEADS * B_QK_PAD), _BF16),
            jax.ShapeDtypeStruct((t, B_WIDTH), _BF16),
        ],
        compiler_params=_params(1),
        name="mla_prep",
    )(blat, bkr, lw['b_w_uq'], lw['b_w_ukv'], lw['b_q_gain'], lw['b_k_gain_nope'], lw['b_k_gain_rope'],
      tables['cos2'], tables['sin2'])


def _mla_attn_kernel(q_ref, k_ref, v_ref, g_ref, o_ref, *, tq):
    n_chunks = q_ref.shape[0] // tq

    def body(i, carry):
        rows = pl.ds(pl.multiple_of(i * tq, tq), tq)
        s = _dot_nt(q_ref[rows, :], k_ref[...])
        m = jnp.max(s, axis=-1, keepdims=True)
        e = jnp.exp(s - m)
        denom = jnp.sum(e, axis=-1, keepdims=True)
        o = _dot(e.astype(_BF16), v_ref[...]) / denom
        o_ref[rows, :] = (o * g_ref[rows, :].astype(_F32)).astype(_BF16)
        return carry

    lax.fori_loop(0, n_chunks, body, 0)


def _mla_attn(q, k, v, gates, batch, seq, tq):
    t = q.shape[0]
    gate_blk = A_WIDTH // B_V
    return pl.pallas_call(
        functools.partial(_mla_attn_kernel, tq=tq),
        grid=(batch, B_HEADS),
        in_specs=[
            pl.BlockSpec((seq, B_QK_PAD), lambda b, hd: (b, hd)),
            pl.BlockSpec((seq, B_QK_PAD), lambda b, hd: (b, hd)),
            pl.BlockSpec((seq, B_V), lambda b, hd: (b, hd)),
            pl.BlockSpec((seq, B_V), lambda b, hd: (b, gate_blk + hd)),
        ],
        out_specs=pl.BlockSpec((seq, B_V), lambda b, hd: (b, hd)),
        out_shape=jax.ShapeDtypeStruct((t, B_WIDTH), _BF16),
        compiler_params=_params(2),
        name="mla_attn",
    )(q, k, v, gates)


def _win_attn_kernel(sink_ref, slope_ref, q_ref, k_ref, v_ref, g_ref, o_ref, bias_ref, sinkcol_ref):
    seq = q_ref.shape[0]
    nb = seq // A_BLOCK
    span = 3 * A_BLOCK
    rows_all = A_GROUP * A_BLOCK
    kvh = pl.program_id(1)

    qi = lax.broadcasted_iota(jnp.int32, (A_BLOCK, span), 0)
    si = lax.broadcasted_iota(jnp.int32, (A_BLOCK, span), 1)
    for variant in range(3):
        dist = jnp.abs(si - qi - variant * A_BLOCK)
        valid = dist <= A_WINDOW
        for g in range(A_GROUP):
            slope = slope_ref[kvh * A_GROUP + g]
            bias_ref[variant, g * A_BLOCK:(g + 1) * A_BLOCK, :] = jnp.where(
                valid, -(slope * dist.astype(_F32)), NEG_INF)
    for g in range(A_GROUP):
        sinkcol_ref[g * A_BLOCK:(g + 1) * A_BLOCK, :] = jnp.full(
            (A_BLOCK, 1), sink_ref[kvh * A_GROUP + g], _F32)

    def body(j, carry):
        start = jnp.clip((j - 1) * A_BLOCK, 0, seq - span)
        variant = (j * A_BLOCK - start) // A_BLOCK
        qrows = pl.ds(pl.multiple_of(j * A_BLOCK, A_BLOCK), A_BLOCK)
        krows = pl.ds(pl.multiple_of(start, A_BLOCK), span)
        qs = jnp.concatenate(
            [q_ref[qrows, g * HEAD_DIM:(g + 1) * HEAD_DIM] for g in range(A_GROUP)], axis=0)
        s = _dot_nt(qs, k_ref[krows, :]) + bias_ref[variant]
        sink = sinkcol_ref[...]
        m = jnp.maximum(jnp.max(s, axis=-1, keepdims=True), sink)
        e = jnp.exp(s - m)
        denom = jnp.sum(e, axis=-1, keepdims=True) + jnp.exp(sink - m)
        o = _dot(e.astype(_BF16), v_ref[krows, :]) / denom
        for g in range(A_GROUP):
            cols = slice(g * HEAD_DIM, (g + 1) * HEAD_DIM)
            o_ref[qrows, cols] = (
                o[g * A_BLOCK:(g + 1) * A_BLOCK] * g_ref[qrows, cols].astype(_F32)).astype(_BF16)
        return carry

    lax.fori_loop(0, nb, body, 0)


def _win_attn(aq, akv, gates, sink, batch, seq):
    t = aq.shape[0]
    gw = A_GROUP * HEAD_DIM
    assert seq >= 3 * A_BLOCK
    slopes = jnp.exp2(-8.0 * jnp.arange(1, A_HEADS + 1, dtype=_F32) / A_HEADS)
    return pl.pallas_call(
        _win_attn_kernel,
        grid=(batch, A_KV_HEADS),
        in_specs=[
            pl.BlockSpec(memory_space=pltpu.SMEM),
            pl.BlockSpec(memory_space=pltpu.SMEM),
            pl.BlockSpec((seq, gw), lambda b, kh: (b, kh)),
            pl.BlockSpec((seq, HEAD_DIM), lambda b, kh: (b, kh)),
            pl.BlockSpec((seq, HEAD_DIM), lambda b, kh: (b, A_KV_HEADS + kh)),
            pl.BlockSpec((seq, gw), lambda b, kh: (b, kh)),
        ],
        out_specs=pl.BlockSpec((seq, gw), lambda b, kh: (b, kh)),
        out_shape=jax.ShapeDtypeStruct((t, A_WIDTH), _BF16),
        scratch_shapes=[
            pltpu.VMEM((3, A_GROUP * A_BLOCK, 3 * A_BLOCK), _F32),
            pltpu.VMEM((A_GROUP * A_BLOCK, 1), _F32),
        ],
        compiler_params=_params(2),
        name="win_attn",
    )(sink, slopes, aq, akv, akv, gates)


def _na_attn_kernel(q_ref, k_ref, v_ref, g_ref, bias_ref, o_ref):
    seq = q_ref.shape[0]
    rows = seq // GRID_W
    wr = min(NA_ROWS_MAX, rows)
    span = wr * GRID_W

    def body(r, carry):
        rs = jnp.clip(r - wr // 2, 0, rows - wr)
        qrows = pl.ds(pl.multiple_of(r * GRID_W, GRID_W), GRID_W)
        krows = pl.ds(pl.multiple_of(rs * GRID_W, GRID_W), span)
        s = _dot_nt(q_ref[qrows, :], k_ref[krows, :]) + bias_ref[r - rs]
        m = jnp.max(s, axis=-1, keepdims=True)
        e = jnp.exp(s - m)
        denom = jnp.sum(e, axis=-1, keepdims=True)
        o = _dot(e.astype(_BF16), v_ref[krows, :]) / denom
        o_ref[qrows, :] = (o * g_ref[qrows, :].astype(_F32)).astype(_BF16)
        return carry

    lax.fori_loop(0, rows, body, 0)


def _na_attn(cqkv, gates, bias, batch, seq, layer):
    t = cqkv.shape[0]
    wr = bias.shape[2]
    gate_blk = (A_WIDTH + B_WIDTH) // HEAD_DIM
    blk = lambda off: pl.BlockSpec((seq, HEAD_DIM), lambda b, hd: (b, off + hd))
    return pl.pallas_call(
        _na_attn_kernel,
        grid=(batch, C_HEADS),
        in_specs=[
            blk(0), blk(C_HEADS), blk(2 * C_HEADS),
            pl.BlockSpec((seq, HEAD_DIM), lambda b, hd: (b, gate_blk + hd)),
            pl.BlockSpec((None, None, wr, GRID_W, wr * GRID_W), lambda b, hd: (layer, hd, 0, 0, 0)),
        ],
        out_specs=pl.BlockSpec((seq, HEAD_DIM), lambda b, hd: (b, hd)),
        out_shape=jax.ShapeDtypeStruct((t, C_WIDTH), _BF16),
        compiler_params=_params(2),
        name="na_attn",
    )(cqkv, cqkv, cqkv, gates, bias)


def _out_ple_kernel(ya_ref, yb_ref, yc_ref, h_ref, p_ref, wo_ref, gple_ref, wg_ref, wp_ref, gpost_ref,
                    o_ref):
    acc = _dot(ya_ref[...], wo_ref[:A_WIDTH, :])
    acc += _dot(yb_ref[...], wo_ref[A_WIDTH:A_WIDTH + B_WIDTH, :])
    acc += _dot(yc_ref[...], wo_ref[A_WIDTH + B_WIDTH:, :])
    h1 = h_ref[...] + acc
    gate = jax.nn.sigmoid(_dot(_rms(h1, gple_ref[...]).astype(_BF16), wg_ref[...]))
    pp = _rms(_dot(p_ref[...].astype(_BF16), wp_ref[...]), gpost_ref[...])
    o_ref[...] = h1 + gate * pp


def _out_ple(ya, yb, yc, h, p, lw, layer, tm):
    t = h.shape[0]
    row = lambda i: (i, 0)
    vec = lambda n: pl.BlockSpec((None, 1, n), lambda i: (layer, 0, 0))
    mat = lambda r, c: _resident((None, r, c), lambda i: (layer, 0, 0))
    return pl.pallas_call(
        _out_ple_kernel,
        grid=(t // tm,),
        in_specs=[
            pl.BlockSpec((tm, A_WIDTH), row),
            pl.BlockSpec((tm, B_WIDTH), row),
            pl.BlockSpec((tm, C_WIDTH), row),
            pl.BlockSpec((tm, D_MODEL), row),
            pl.BlockSpec((None, tm, PLE_DIM), lambda i: (layer, i, 0)),
            mat(MIX_WIDTH, D_MODEL), vec(D_MODEL), mat(D_MODEL, D_MODEL), mat(PLE_DIM, D_MODEL),
            vec(D_MODEL),
        ],
        out_specs=pl.BlockSpec((tm, D_MODEL), row),
        out_shape=jax.ShapeDtypeStruct((t, D_MODEL), _F32),
        compiler_params=_params(1),
        name="out_ple",
    )(ya, yb, yc, h, p, lw['w_out'], lw['ple_norm'], lw['w_ple_gate'], lw['w_ple_proj'],
      lw['ple_post_norm'])


def _swap_halves(x):
    half = x.shape[-1] // 2
    return jnp.concatenate([x[..., half:], x[..., :half]], axis=-1)


def _prepare_weights(W):
    depth = W['w_in'].shape[0]
    sizes = (A_WIDTH, A_KV_WIDTH, A_KV_WIDTH, A_WIDTH, B_Q_LORA, B_KV_LORA, B_ROPE, B_WIDTH,
             C_WIDTH, C_WIDTH, C_WIDTH, C_WIDTH)
    offs = np.concatenate([[0], np.cumsum(sizes)])
    aq, ak, av, az, bcq, bckv, bkr, bz, cq, ck, cv, cz = [
        W['w_in'][:, :, offs[i]:offs[i + 1]] for i in range(len(sizes))]
    w_in = jnp.concatenate(
        [aq, ak, av, az, bz, cz, bcq, bckv, cq, ck, cv, bkr, _swap_halves(bkr)], axis=-1).astype(_BF16)

    wq = W['b_w_uq'].reshape(depth, B_Q_LORA, B_HEADS, B_QK)
    wq = jnp.concatenate([wq, _swap_halves(wq[..., B_NOPE:])], axis=-1)
    wq = wq.reshape(depth, B_Q_LORA, B_HEADS * B_QK_PAD).astype(_BF16)
    gq = W['b_q_norm']
    gk = W['b_k_norm']
    row = lambda g: g[:, None, :].astype(_F32)
    return {
        'norm_in': row(W['norm_in']), 'w_in': w_in,
        'a_q_norm': row(W['a_q_norm']), 'a_k_norm': row(W['a_k_norm']), 'a_sink': W['a_sink'].astype(_F32),
        'b_cq_norm': row(W['b_cq_norm']), 'b_ckv_norm': row(W['b_ckv_norm']),
        'b_w_uq': wq, 'b_w_ukv': W['b_w_ukv'].astype(_BF16),
        'b_q_gain': row(jnp.concatenate([gq, _swap_halves(gq[:, B_NOPE:])], axis=-1)),
        'b_k_gain_nope': row(gk[:, :B_NOPE]),
        'b_k_gain_rope': row(jnp.concatenate([gk[:, B_NOPE:], _swap_halves(gk[:, B_NOPE:])], axis=-1)),
        'c_q_norm': row(W['c_q_norm']), 'c_k_norm': row(W['c_k_norm']),
        'w_out': W['w_out'].astype(_BF16), 'ple_norm': row(W['ple_norm']),
        'w_ple_gate': W['w_ple_gate'].astype(_BF16), 'w_ple_proj': W['w_ple_proj'].astype(_BF16),
        'ple_post_norm': row(W['ple_post_norm']),
    }


def _rope_tables(seq):
    half = B_ROPE // 2
    inv = ROPE_THETA ** (-jnp.arange(half, dtype=_F32) / half)
    ang = jnp.arange(seq).astype(_F32)[:, None] * inv[None, :]
    cos, sin = jnp.cos(ang), jnp.sin(ang)
    zeros = jnp.zeros((seq, B_ROPE), _F32)
    return {'cos2': jnp.concatenate([cos, cos, zeros], axis=-1),
            'sin2': jnp.concatenate([-sin, sin, zeros], axis=-1)}


def _na_bias_table(rpb, seq):
    rows = seq // GRID_W
    wr = min(NA_ROWS_MAX, rows)
    c = np.arange(GRID_W)
    cs = np.clip(c - NA_COLS // 2, 0, GRID_W - NA_COLS)
    col_valid = (c[None, :] >= cs[:, None]) & (c[None, :] < cs[:, None] + NA_COLS)
    dc = np.clip(c[None, :] - c[:, None] + NA_COLS - 1, 0, 2 * NA_COLS - 2)
    dr = np.arange(wr)[None, :] - np.arange(wr)[:, None] + NA_ROWS_MAX - 1
    bias = rpb.astype(_F32)[:, :, dr[:, None, :, None], dc[None, :, None, :]]
    bias = jnp.where(col_valid[None, None, None, :, None, :], bias, NEG_INF)
    return bias.reshape(rpb.shape[0], C_HEADS, wr, GRID_W, wr * GRID_W)


def _trunk(x, p, lw, tables, na_bias):
    batch, seq, _ = x.shape
    depth = p.shape[0]
    t = batch * seq
    h = x.reshape(t, D_MODEL)
    p = p.reshape(depth, t, PLE_DIM)
    tm = min(256, seq)
    for layer in range(depth):
        aq, akv, gates, blat, cqkv, bkr = _in_proj(h, lw, layer, tm)
        bq, bk, bv = _mla_prep(blat, bkr, lw, tables, layer, seq, tm)
        ya = _win_attn(aq, akv, gates, lw['a_sink'][layer], batch, seq)
        yb = _mla_attn(bq, bk, bv, gates, batch, seq, min(256, seq))
        yc = _na_attn(cqkv, gates, na_bias, batch, seq, layer)
        h = _out_ple(ya, yb, yc, h, p, lw, layer, tm)
    return h.reshape(batch, seq, D_MODEL)


def kernel(x_prompt, x_sample, p_prompt, p_sample, norm_in, w_in, a_q_norm, a_k_norm, a_sink, b_cq_norm, b_ckv_norm, b_w_uq, b_w_ukv, b_q_norm, b_k_norm, c_q_norm, c_k_norm, c_rpb, w_out, ple_norm, w_ple_gate, w_ple_proj, ple_post_norm):
    lw = _prepare_weights({
        'norm_in': norm_in, 'w_in': w_in, 'a_q_norm': a_q_norm, 'a_k_norm': a_k_norm, 'a_sink': a_sink,
        'b_cq_norm': b_cq_norm, 'b_ckv_norm': b_ckv_norm, 'b_w_uq': b_w_uq, 'b_w_ukv': b_w_ukv,
        'b_q_norm': b_q_norm, 'b_k_norm': b_k_norm, 'c_q_norm': c_q_norm, 'c_k_norm': c_k_norm,
        'w_out': w_out, 'ple_norm': ple_norm, 'w_ple_gate': w_ple_gate, 'w_ple_proj': w_ple_proj,
        'ple_post_norm': ple_post_norm,
    })
    outs = []
    for x, p in ((x_prompt, p_prompt), (x_sample, p_sample)):
        seq = x.shape[1]
        outs.append(_trunk(x, p, lw, _rope_tables(seq), _na_bias_table(c_rpb, seq)))
    return tuple(outs)
```

```python
import functools

import jax
import jax.numpy as jnp
import numpy as np
from jax import lax
from jax.experimental import pallas as pl
from jax.experimental.pallas import tpu as pltpu

D_MODEL = 2048
PLE_DIM = 256
HEAD_DIM = 128
A_HEADS = 6
A_KV_HEADS = 2
A_GROUP = A_HEADS // A_KV_HEADS
A_WINDOW = 128
A_BLOCK = 128
B_HEADS = 6
B_Q_LORA = 512
B_KV_LORA = 512
B_NOPE = 128
B_ROPE = 64
B_V = 128
B_QK = B_NOPE + B_ROPE
ROPE_THETA = 10000.0
C_HEADS = 4
GRID_W = 64
NA_ROWS_MAX = 8
NA_COLS = 16
A_WIDTH = A_HEADS * HEAD_DIM
A_KV_WIDTH = A_KV_HEADS * HEAD_DIM
B_WIDTH = B_HEADS * B_V
C_WIDTH = C_HEADS * HEAD_DIM
MIX_WIDTH = A_WIDTH + B_WIDTH + C_WIDTH
EPS = 1e-6
NEG_INF = -1e30

B_QK_PAD = 2 * HEAD_DIM
IN_A_Q = 0
IN_A_KV = IN_A_Q + A_WIDTH
IN_GATES = IN_A_KV + 2 * A_KV_WIDTH
IN_B_LAT = IN_GATES + MIX_WIDTH
IN_C_QKV = IN_B_LAT + B_Q_LORA + B_KV_LORA
IN_B_KR = IN_C_QKV + 3 * C_WIDTH
IN_WIDTH_PAD = IN_B_KR + 2 * B_ROPE

VMEM_LIMIT_BYTES = 56 * 1024 * 1024

_BF16 = jnp.bfloat16
_F32 = jnp.float32


def _params(n_grid_dims):
    return pltpu.CompilerParams(
        dimension_semantics=("arbitrary",) * n_grid_dims,
        vmem_limit_bytes=VMEM_LIMIT_BYTES)


def _resident(shape, index_map):
    return pl.BlockSpec(shape, index_map, pipeline_mode=pl.Buffered(1))


def _rms(x, g):
    return x * lax.rsqrt(jnp.mean(x * x, axis=-1, keepdims=True) + EPS) * g


def _dot(a, b):
    return jnp.dot(a, b, preferred_element_type=_F32)


def _dot_nt(a, b):
    return lax.dot_general(a, b, (((1,), (1,)), ((), ())), preferred_element_type=_F32)


def _in_proj_kernel(h_ref, gin_ref, w_ref, gaq_ref, gak_ref, gbq_ref, gbkv_ref, gcq_ref, gck_ref,
                    aq_ref, akv_ref, gates_ref, blat_ref, cqkv_ref, bkr_ref):
    u = _rms(h_ref[...], gin_ref[...]).astype(_BF16)
    scale = HEAD_DIM ** -0.5

    def seg(start, width):
        return _dot(u, w_ref[:, start:start + width])

    def heads(x, g, n, s=None):
        outs = []
        for i in range(n):
            y = _rms(x[:, i * HEAD_DIM:(i + 1) * HEAD_DIM], g)
            outs.append(y if s is None else y * s)
        return outs

    x = seg(IN_A_Q, A_WIDTH)
    for i, y in enumerate(heads(x, gaq_ref[...], A_HEADS, scale)):
        aq_ref[:, i * HEAD_DIM:(i + 1) * HEAD_DIM] = y.astype(_BF16)

    x = seg(IN_A_KV, 2 * A_KV_WIDTH)
    for i, y in enumerate(heads(x[:, :A_KV_WIDTH], gak_ref[...], A_KV_HEADS)):
        akv_ref[:, i * HEAD_DIM:(i + 1) * HEAD_DIM] = y.astype(_BF16)
    akv_ref[:, A_KV_WIDTH:] = x[:, A_KV_WIDTH:].astype(_BF16)

    chunk = 512
    for c in range(MIX_WIDTH // chunk):
        z = seg(IN_GATES + c * chunk, chunk)
        gates_ref[:, c * chunk:(c + 1) * chunk] = (z * jax.nn.sigmoid(z)).astype(_BF16)

    x = seg(IN_B_LAT, B_Q_LORA + B_KV_LORA)
    blat_ref[:, :B_Q_LORA] = _rms(x[:, :B_Q_LORA], gbq_ref[...]).astype(_BF16)
    blat_ref[:, B_Q_LORA:] = _rms(x[:, B_Q_LORA:], gbkv_ref[...]).astype(_BF16)

    x = seg(IN_C_QKV, 3 * C_WIDTH)
    for i, y in enumerate(heads(x[:, :C_WIDTH], gcq_ref[...], C_HEADS, scale)):
        cqkv_ref[:, i * HEAD_DIM:(i + 1) * HEAD_DIM] = y.astype(_BF16)
    for i, y in enumerate(heads(x[:, C_WIDTH:2 * C_WIDTH], gck_ref[...], C_HEADS)):
        cqkv_ref[:, C_WIDTH + i * HEAD_DIM:C_WIDTH + (i + 1) * HEAD_DIM] = y.astype(_BF16)
    cqkv_ref[:, 2 * C_WIDTH:] = x[:, 2 * C_WIDTH:].astype(_BF16)

    bkr_ref[...] = seg(IN_B_KR, 2 * B_ROPE)


def _in_proj(h, lw, layer, tm):
    t = h.shape[0]
    row = lambda i: (i, 0)
    vec = lambda n: pl.BlockSpec((None, 1, n), lambda i: (layer, 0, 0))
    widths = (A_WIDTH, 2 * A_KV_WIDTH, MIX_WIDTH, B_Q_LORA + B_KV_LORA, 3 * C_WIDTH)
    out_shape = [jax.ShapeDtypeStruct((t, w), _BF16) for w in widths]
    out_shape.append(jax.ShapeDtypeStruct((t, 2 * B_ROPE), _F32))
    out_specs = [pl.BlockSpec((tm, w), row) for w in widths] + [pl.BlockSpec((tm, 2 * B_ROPE), row)]
    return pl.pallas_call(
        _in_proj_kernel,
        grid=(t // tm,),
        in_specs=[
            pl.BlockSpec((tm, D_MODEL), row),
            vec(D_MODEL),
            _resident((None, D_MODEL, IN_WIDTH_PAD), lambda i: (layer, 0, 0)),
            vec(HEAD_DIM), vec(HEAD_DIM), vec(B_Q_LORA), vec(B_KV_LORA), vec(HEAD_DIM), vec(HEAD_DIM),
        ],
        out_specs=out_specs,
        out_shape=out_shape,
        compiler_params=_params(1),
        name="in_proj",
    )(h, lw['norm_in'], lw['w_in'], lw['a_q_norm'], lw['a_k_norm'], lw['b_cq_norm'], lw['b_ckv_norm'],
      lw['c_q_norm'], lw['c_k_norm'])


def _mla_prep_kernel(blat_ref, bkr_ref, wq_ref, wkv_ref, gq_ref, gkn_ref, gkr_ref, cos_ref, sin_ref,
                     q_ref, k_ref, v_ref):
    cqn = blat_ref[:, :B_Q_LORA]
    ckvn = blat_ref[:, B_Q_LORA:]
    cos2 = cos_ref[...]
    sin2 = sin_ref[...]
    lane = lax.broadcasted_iota(jnp.int32, (1, HEAD_DIM), 1)
    rope_lanes = lane < B_ROPE
    scale = B_QK ** -0.5

    def rope(c):
        return c * cos2 + pltpu.roll(c, B_ROPE, 1) * sin2

    kr2 = bkr_ref[...]
    ss_kr = jnp.sum(jnp.where(rope_lanes, kr2 * kr2, 0.0), axis=-1, keepdims=True)
    k_rope = rope(kr2 * gkr_ref[...])

    for hd in range(B_HEADS):
        x = _dot(cqn, wq_ref[:, hd * B_QK_PAD:(hd + 1) * B_QK_PAD])
        a = x[:, :HEAD_DIM]
        c = x[:, HEAD_DIM:]
        ss = jnp.sum(a * a, axis=-1, keepdims=True) + jnp.sum(
            jnp.where(rope_lanes, c * c, 0.0), axis=-1, keepdims=True)
        r = lax.rsqrt(ss * (1.0 / B_QK) + EPS)
        q_ref[:, hd * B_QK_PAD:hd * B_QK_PAD + HEAD_DIM] = (
            a * r * gq_ref[:, :HEAD_DIM] * scale).astype(_BF16)
        q_ref[:, hd * B_QK_PAD + HEAD_DIM:(hd + 1) * B_QK_PAD] = (
            rope(c * r * gq_ref[:, HEAD_DIM:]) * scale).astype(_BF16)

        kv = _dot(ckvn, wkv_ref[:, hd * 2 * HEAD_DIM:(hd + 1) * 2 * HEAD_DIM])
        kn = kv[:, :B_NOPE]
        ss = jnp.sum(kn * kn, axis=-1, keepdims=True) + ss_kr
        r = lax.rsqrt(ss * (1.0 / B_QK) + EPS)
        k_ref[:, hd * B_QK_PAD:hd * B_QK_PAD + HEAD_DIM] = (kn * r * gkn_ref[...]).astype(_BF16)
        k_ref[:, hd * B_QK_PAD + HEAD_DIM:(hd + 1) * B_QK_PAD] = (k_rope * r).astype(_BF16)
        v_ref[:, hd * B_V:(hd + 1) * B_V] = kv[:, B_NOPE:].astype(_BF16)


def _mla_prep(blat, bkr, lw, tables, layer, seq, tm):
    t = blat.shape[0]
    row = lambda i: (i, 0)
    pos = lambda i: (i % (seq // tm), 0)
    vec = lambda n: pl.BlockSpec((None, 1, n), lambda i: (layer, 0, 0))
    return pl.pallas_call(
        _mla_prep_kernel,
        grid=(t // tm,),
        in_specs=[
            pl.BlockSpec((tm, B_Q_LORA + B_KV_LORA), row),
            pl.BlockSpec((tm, 2 * B_ROPE), row),
            _resident((None, B_Q_LORA, B_HEADS * B_QK_PAD), lambda i: (layer, 0, 0)),
            _resident((None, B_KV_LORA, B_HEADS * (B_NOPE + B_V)), lambda i: (layer, 0, 0)),
            vec(B_QK_PAD), vec(HEAD_DIM), vec(2 * B_ROPE),
            pl.BlockSpec((tm, 2 * B_ROPE), pos),
            pl.BlockSpec((tm, 2 * B_ROPE), pos),
        ],
        out_specs=[
            pl.BlockSpec((tm, B_HEADS * B_QK_PAD), row),
            pl.BlockSpec((tm, B_HEADS * B_QK_PAD), row),
            pl.BlockSpec((tm, B_WIDTH), row),
        ],
        out_shape=[
            jax.ShapeDtypeStruct((t, B_HEADS * B_QK_PAD), _BF16),
            jax.ShapeDtypeStruct((t, B_HEADS * B_QK_PAD), _BF16),
            jax.ShapeDtypeStruct((t, B_WIDTH), _BF16),
        ],
        compiler_params=_params(1),
        name="mla_prep",
    )(blat, bkr, lw['b_w_uq'], lw['b_w_ukv'], lw['b_q_gain'], lw['b_k_gain_nope'], lw['b_k_gain_rope'],
      tables['cos2'], tables['sin2'])


def _mla_attn_kernel(q_ref, k_ref, v_ref, g_ref, o_ref, *, tq):
    n_chunks = q_ref.shape[0] // tq

    def body(i, carry):
        rows = pl.ds(pl.multiple_of(i * tq, tq), tq)
        s = _dot_nt(q_ref[rows, :], k_ref[...])
        m = jnp.max(s, axis=-1, keepdims=True)
        e = jnp.exp(s - m)
        denom = jnp.sum(e, axis=-1, keepdims=True)
        o = _dot(e.astype(_BF16), v_ref[...]) / denom
        o_ref[rows, :] = (o * g_ref[rows, :].astype(_F32)).astype(_BF16)
        return carry

    lax.fori_loop(0, n_chunks, body, 0, unroll=2)


def _mla_attn(q, k, v, gates, batch, seq, tq):
    t = q.shape[0]
    gate_blk = A_WIDTH // B_V
    return pl.pallas_call(
        functools.partial(_mla_attn_kernel, tq=tq),
        grid=(batch, B_HEADS),
        in_specs=[
            pl.BlockSpec((seq, B_QK_PAD), lambda b, hd: (b, hd)),
            pl.BlockSpec((seq, B_QK_PAD), lambda b, hd: (b, hd)),
            pl.BlockSpec((seq, B_V), lambda b, hd: (b, hd)),
            pl.BlockSpec((seq, B_V), lambda b, hd: (b, gate_blk + hd)),
        ],
        out_specs=pl.BlockSpec((seq, B_V), lambda b, hd: (b, hd)),
        out_shape=jax.ShapeDtypeStruct((t, B_WIDTH), _BF16),
        compiler_params=_params(2),
        name="mla_attn",
    )(q, k, v, gates)


def _win_attn_kernel(sink_ref, slope_ref, q_ref, k_ref, v_ref, g_ref, o_ref, bias_ref, sinkcol_ref):
    seq = q_ref.shape[0]
    nb = seq // A_BLOCK
    span = 3 * A_BLOCK
    rows_all = A_GROUP * A_BLOCK
    kvh = pl.program_id(1)

    qi = lax.broadcasted_iota(jnp.int32, (A_BLOCK, span), 0)
    si = lax.broadcasted_iota(jnp.int32, (A_BLOCK, span), 1)
    for variant in range(3):
        dist = jnp.abs(si - qi - variant * A_BLOCK)
        valid = dist <= A_WINDOW
        for g in range(A_GROUP):
            slope = slope_ref[kvh * A_GROUP + g]
            bias_ref[variant, g * A_BLOCK:(g + 1) * A_BLOCK, :] = jnp.where(
                valid, -(slope * dist.astype(_F32)), NEG_INF)
    for g in range(A_GROUP):
        sinkcol_ref[g * A_BLOCK:(g + 1) * A_BLOCK, :] = jnp.full(
            (A_BLOCK, 1), sink_ref[kvh * A_GROUP + g], _F32)

    def body(j, carry):
        start = jnp.clip((j - 1) * A_BLOCK, 0, seq - span)
        variant = (j * A_BLOCK - start) // A_BLOCK
        qrows = pl.ds(pl.multiple_of(j * A_BLOCK, A_BLOCK), A_BLOCK)
        krows = pl.ds(pl.multiple_of(start, A_BLOCK), span)
        qs = jnp.concatenate(
            [q_ref[qrows, g * HEAD_DIM:(g + 1) * HEAD_DIM] for g in range(A_GROUP)], axis=0)
        s = _dot_nt(qs, k_ref[krows, :]) + bias_ref[variant]
        sink = sinkcol_ref[...]
        m = jnp.maximum(jnp.max(s, axis=-1, keepdims=True), sink)
        e = jnp.exp(s - m)
        denom = jnp.sum(e, axis=-1, keepdims=True) + jnp.exp(sink - m)
        o = _dot(e.astype(_BF16), v_ref[krows, :]) / denom
        for g in range(A_GROUP):
            cols = slice(g * HEAD_DIM, (g + 1) * HEAD_DIM)
            o_ref[qrows, cols] = (
                o[g * A_BLOCK:(g + 1) * A_BLOCK] * g_ref[qrows, cols].astype(_F32)).astype(_BF16)
        return carry

    lax.fori_loop(0, nb, body, 0, unroll=2)


def _win_attn(aq, akv, gates, sink, batch, seq):
    t = aq.shape[0]
    gw = A_GROUP * HEAD_DIM
    assert seq >= 3 * A_BLOCK
    slopes = jnp.exp2(-8.0 * jnp.arange(1, A_HEADS + 1, dtype=_F32) / A_HEADS)
    return pl.pallas_call(
        _win_attn_kernel,
        grid=(batch, A_KV_HEADS),
        in_specs=[
            pl.BlockSpec(memory_space=pltpu.SMEM),
            pl.BlockSpec(memory_space=pltpu.SMEM),
            pl.BlockSpec((seq, gw), lambda b, kh: (b, kh)),
            pl.BlockSpec((seq, HEAD_DIM), lambda b, kh: (b, kh)),
            pl.BlockSpec((seq, HEAD_DIM), lambda b, kh: (b, A_KV_HEADS + kh)),
            pl.BlockSpec((seq, gw), lambda b, kh: (b, kh)),
        ],
        out_specs=pl.BlockSpec((seq, gw), lambda b, kh: (b, kh)),
        out_shape=jax.ShapeDtypeStruct((t, A_WIDTH), _BF16),
        scratch_shapes=[
            pltpu.VMEM((3, A_GROUP * A_BLOCK, 3 * A_BLOCK), _F32),
            pltpu.VMEM((A_GROUP * A_BLOCK, 1), _F32),
        ],
        compiler_params=_params(2),
        name="win_attn",
    )(sink, slopes, aq, akv, akv, gates)


def _na_attn_kernel(q_ref, k_ref, v_ref, g_ref, bias_ref, o_ref):
    seq = q_ref.shape[0]
    rows = seq // GRID_W
    wr = min(NA_ROWS_MAX, rows)
    span = wr * GRID_W

    def body(r, carry):
        rs = jnp.clip(r - wr // 2, 0, rows - wr)
        qrows = pl.ds(pl.multiple_of(r * GRID_W, GRID_W), GRID_W)
        krows = pl.ds(pl.multiple_of(rs * GRID_W, GRID_W), span)
        s = _dot_nt(q_ref[qrows, :], k_ref[krows, :]) + bias_ref[r - rs]
        m = jnp.max(s, axis=-1, keepdims=True)
        e = jnp.exp(s - m)
        denom = jnp.sum(e, axis=-1, keepdims=True)
        o = _dot(e.astype(_BF16), v_ref[krows, :]) / denom
        o_ref[qrows, :] = (o * g_ref[qrows, :].astype(_F32)).astype(_BF16)
        return carry

    lax.fori_loop(0, rows, body, 0, unroll=4)


def _na_attn(cqkv, gates, bias, batch, seq, layer):
    t = cqkv.shape[0]
    wr = bias.shape[2]
    gate_blk = (A_WIDTH + B_WIDTH) // HEAD_DIM
    blk = lambda off: pl.BlockSpec((seq, HEAD_DIM), lambda b, hd: (b, off + hd))
    return pl.pallas_call(
        _na_attn_kernel,
        grid=(batch, C_HEADS),
        in_specs=[
            blk(0), blk(C_HEADS), blk(2 * C_HEADS),
            pl.BlockSpec((seq, HEAD_DIM), lambda b, hd: (b, gate_blk + hd)),
            pl.BlockSpec((None, None, wr, GRID_W, wr * GRID_W), lambda b, hd: (layer, hd, 0, 0, 0)),
        ],
        out_specs=pl.BlockSpec((seq, HEAD_DIM), lambda b, hd: (b, hd)),
        out_shape=jax.ShapeDtypeStruct((t, C_WIDTH), _BF16),
        compiler_params=_params(2),
        name="na_attn",
    )(cqkv, cqkv, cqkv, gates, bias)


def _out_ple_kernel(ya_ref, yb_ref, yc_ref, h_ref, p_ref, wo_ref, gple_ref, wg_ref, wp_ref, gpost_ref,
                    o_ref):
    acc = _dot(ya_ref[...], wo_ref[:A_WIDTH, :])
    acc += _dot(yb_ref[...], wo_ref[A_WIDTH:A_WIDTH + B_WIDTH, :])
    acc += _dot(yc_ref[...], wo_ref[A_WIDTH + B_WIDTH:, :])
    h1 = h_ref[...] + acc
    gate = jax.nn.sigmoid(_dot(_rms(h1, gple_ref[...]).astype(_BF16), wg_ref[...]))
    pp = _rms(_dot(p_ref[...].astype(_BF16), wp_ref[...]), gpost_ref[...])
    o_ref[...] = h1 + gate * pp


def _out_ple(ya, yb, yc, h, p, lw, layer, tm):
    t = h.shape[0]
    row = lambda i: (i, 0)
    vec = lambda n: pl.BlockSpec((None, 1, n), lambda i: (layer, 0, 0))
    mat = lambda r, c: _resident((None, r, c), lambda i: (layer, 0, 0))
    return pl.pallas_call(
        _out_ple_kernel,
        grid=(t // tm,),
        in_specs=[
            pl.BlockSpec((tm, A_WIDTH), row),
            pl.BlockSpec((tm, B_WIDTH), row),
            pl.BlockSpec((tm, C_WIDTH), row),
            pl.BlockSpec((tm, D_MODEL), row),
            pl.BlockSpec((None, tm, PLE_DIM), lambda i: (layer, i, 0)),
            mat(MIX_WIDTH, D_MODEL), vec(D_MODEL), mat(D_MODEL, D_MODEL), mat(PLE_DIM, D_MODEL),
            vec(D_MODEL),
        ],
        out_specs=pl.BlockSpec((tm, D_MODEL), row),
        out_shape=jax.ShapeDtypeStruct((t, D_MODEL), _F32),
        compiler_params=_params(1),
        name="out_ple",
    )(ya, yb, yc, h, p, lw['w_out'], lw['ple_norm'], lw['w_ple_gate'], lw['w_ple_proj'],
      lw['ple_post_norm'])


def _swap_halves(x):
    half = x.shape[-1] // 2
    return jnp.concatenate([x[..., half:], x[..., :half]], axis=-1)


def _prepare_weights(W):
    depth = W['w_in'].shape[0]
    sizes = (A_WIDTH, A_KV_WIDTH, A_KV_WIDTH, A_WIDTH, B_Q_LORA, B_KV_LORA, B_ROPE, B_WIDTH,
             C_WIDTH, C_WIDTH, C_WIDTH, C_WIDTH)
    offs = np.concatenate([[0], np.cumsum(sizes)])
    aq, ak, av, az, bcq, bckv, bkr, bz, cq, ck, cv, cz = [
        W['w_in'][:, :, offs[i]:offs[i + 1]] for i in range(len(sizes))]
    w_in = jnp.concatenate(
        [aq, ak, av, az, bz, cz, bcq, bckv, cq, ck, cv, bkr, _swap_halves(bkr)], axis=-1).astype(_BF16)

    wq = W['b_w_uq'].reshape(depth, B_Q_LORA, B_HEADS, B_QK)
    wq = jnp.concatenate([wq, _swap_halves(wq[..., B_NOPE:])], axis=-1)
    wq = wq.reshape(depth, B_Q_LORA, B_HEADS * B_QK_PAD).astype(_BF16)
    gq = W['b_q_norm']
    gk = W['b_k_norm']
    row = lambda g: g[:, None, :].astype(_F32)
    return {
        'norm_in': row(W['norm_in']), 'w_in': w_in,
        'a_q_norm': row(W['a_q_norm']), 'a_k_norm': row(W['a_k_norm']), 'a_sink': W['a_sink'].astype(_F32),
        'b_cq_norm': row(W['b_cq_norm']), 'b_ckv_norm': row(W['b_ckv_norm']),
        'b_w_uq': wq, 'b_w_ukv': W['b_w_ukv'].astype(_BF16),
        'b_q_gain': row(jnp.concatenate([gq, _swap_halves(gq[:, B_NOPE:])], axis=-1)),
        'b_k_gain_nope': row(gk[:, :B_NOPE]),
        'b_k_gain_rope': row(jnp.concatenate([gk[:, B_NOPE:], _swap_halves(gk[:, B_NOPE:])], axis=-1)),
        'c_q_norm': row(W['c_q_norm']), 'c_k_norm': row(W['c_k_norm']),
        'w_out': W['w_out'].astype(_BF16), 'ple_norm': row(W['ple_norm']),
        'w_ple_gate': W['w_ple_gate'].astype(_BF16), 'w_ple_proj': W['w_ple_proj'].astype(_BF16),
        'ple_post_norm': row(W['ple_post_norm']),
    }


def _rope_tables(seq):
    half = B_ROPE // 2
    inv = ROPE_THETA ** (-jnp.arange(half, dtype=_F32) / half)
    ang = jnp.arange(seq).astype(_F32)[:, None] * inv[None, :]
    cos, sin = jnp.cos(ang), jnp.sin(ang)
    zeros = jnp.zeros((seq, B_ROPE), _F32)
    return {'cos2': jnp.concatenate([cos, cos, zeros], axis=-1),
            'sin2': jnp.concatenate([-sin, sin, zeros], axis=-1)}


def _na_bias_table(rpb, seq):
    rows = seq // GRID_W
    wr = min(NA_ROWS_MAX, rows)
    c = np.arange(GRID_W)
    cs = np.clip(c - NA_COLS // 2, 0, GRID_W - NA_COLS)
    col_valid = (c[None, :] >= cs[:, None]) & (c[None, :] < cs[:, None] + NA_COLS)
    dc = np.clip(c[None, :] - c[:, None] + NA_COLS - 1, 0, 2 * NA_COLS - 2)
    by_dc = jnp.where(col_valid, rpb.astype(_F32)[:, :, :, dc], NEG_INF)
    top = NA_ROWS_MAX - 1
    bias = jnp.stack([by_dc[:, :, top - pat:top - pat + wr] for pat in range(wr)], axis=2)
    bias = bias.transpose(0, 1, 2, 4, 3, 5)
    return bias.reshape(rpb.shape[0], C_HEADS, wr, GRID_W, wr * GRID_W)


def _trunk(x, p, lw, tables, na_bias):
    batch, seq, _ = x.shape
    depth = p.shape[0]
    t = batch * seq
    h = x.reshape(t, D_MODEL)
    p = p.reshape(depth, t, PLE_DIM)
    tm = min(256, seq)
    for layer in range(depth):
        aq, akv, gates, blat, cqkv, bkr = _in_proj(h, lw, layer, tm)
        bq, bk, bv = _mla_prep(blat, bkr, lw, tables, layer, seq, tm)
        ya = _win_attn(aq, akv, gates, lw['a_sink'][layer], batch, seq)
        yb = _mla_attn(bq, bk, bv, gates, batch, seq, min(256, seq))
        yc = _na_attn(cqkv, gates, na_bias, batch, seq, layer)
        h = _out_ple(ya, yb, yc, h, p, lw, layer, tm)
    return h.reshape(batch, seq, D_MODEL)


def kernel(x_prompt, x_sample, p_prompt, p_sample, norm_in, w_in, a_q_norm, a_k_norm, a_sink, b_cq_norm, b_ckv_norm, b_w_uq, b_w_ukv, b_q_norm, b_k_norm, c_q_norm, c_k_norm, c_rpb, w_out, ple_norm, w_ple_gate, w_ple_proj, ple_post_norm):
    lw = _prepare_weights({
        'norm_in': norm_in, 'w_in': w_in, 'a_q_norm': a_q_norm, 'a_k_norm': a_k_norm, 'a_sink': a_sink,
        'b_cq_norm': b_cq_norm, 'b_ckv_norm': b_ckv_norm, 'b_w_uq': b_w_uq, 'b_w_ukv': b_w_ukv,
        'b_q_norm': b_q_norm, 'b_k_norm': b_k_norm, 'c_q_norm': c_q_norm, 'c_k_norm': c_k_norm,
        'w_out': w_out, 'ple_norm': ple_norm, 'w_ple_gate': w_ple_gate, 'w_ple_proj': w_ple_proj,
        'ple_post_norm': ple_post_norm,
    })
    tables = {}
    outs = []
    for x, p in ((x_prompt, p_prompt), (x_sample, p_sample)):
        seq = x.shape[1]
        if seq not in tables:
            tables[seq] = (_rope_tables(seq), _na_bias_table(c_rpb, seq))
        outs.append(_trunk(x, p, lw, *tables[seq]))
    return tuple(outs)
```

```python
import functools

import jax
import jax.numpy as jnp
import numpy as np
from jax import lax
from jax.experimental import pallas as pl
from jax.experimental.pallas import tpu as pltpu

D_MODEL = 2048
PLE_DIM = 256
HEAD_DIM = 128
A_HEADS = 6
A_KV_HEADS = 2
A_GROUP = A_HEADS // A_KV_HEADS
A_WINDOW = 128
A_BLOCK = 128
B_HEADS = 6
B_Q_LORA = 512
B_KV_LORA = 512
B_NOPE = 128
B_ROPE = 64
B_V = 128
B_QK = B_NOPE + B_ROPE
ROPE_THETA = 10000.0
C_HEADS = 4
GRID_W = 64
NA_ROWS_MAX = 8
NA_COLS = 16
A_WIDTH = A_HEADS * HEAD_DIM
A_KV_WIDTH = A_KV_HEADS * HEAD_DIM
B_WIDTH = B_HEADS * B_V
C_WIDTH = C_HEADS * HEAD_DIM
MIX_WIDTH = A_WIDTH + B_WIDTH + C_WIDTH
EPS = 1e-6
NEG_INF = -1e30
LOG2E = 1.4426950408889634

B_QK_PAD = 2 * HEAD_DIM
IN_A_Q = 0
IN_A_KV = IN_A_Q + A_WIDTH
IN_GATES = IN_A_KV + 2 * A_KV_WIDTH
IN_B_LAT = IN_GATES + MIX_WIDTH
IN_C_QKV = IN_B_LAT + B_Q_LORA + B_KV_LORA
IN_B_KR = IN_C_QKV + 3 * C_WIDTH
IN_WIDTH_PAD = IN_B_KR + 2 * B_ROPE

VMEM_LIMIT_BYTES = 56 * 1024 * 1024

_BF16 = jnp.bfloat16
_F32 = jnp.float32


def _params(n_grid_dims):
    return pltpu.CompilerParams(
        dimension_semantics=("arbitrary",) * n_grid_dims,
        vmem_limit_bytes=VMEM_LIMIT_BYTES)


def _resident(shape, index_map):
    return pl.BlockSpec(shape, index_map, pipeline_mode=pl.Buffered(1))


def _rms(x, g):
    return x * lax.rsqrt(jnp.mean(x * x, axis=-1, keepdims=True) + EPS) * g


def _dot(a, b):
    return jnp.dot(a, b, preferred_element_type=_F32)


def _dot_nt(a, b):
    return lax.dot_general(a, b, (((1,), (1,)), ((), ())), preferred_element_type=_F32)


def _in_proj_kernel(h_ref, gin_ref, w_ref, gaq_ref, gak_ref, gbq_ref, gbkv_ref, gcq_ref, gck_ref,
                    aq_ref, akv_ref, gates_ref, blat_ref, cqkv_ref, bkr_ref):
    u = _rms(h_ref[...], gin_ref[...]).astype(_BF16)
    scale = HEAD_DIM ** -0.5 * LOG2E

    def seg(start, width):
        return _dot(u, w_ref[:, start:start + width])

    def heads(x, g, n, s=None):
        outs = []
        for i in range(n):
            y = _rms(x[:, i * HEAD_DIM:(i + 1) * HEAD_DIM], g)
            outs.append(y if s is None else y * s)
        return outs

    x = seg(IN_A_Q, A_WIDTH)
    for i, y in enumerate(heads(x, gaq_ref[...], A_HEADS, scale)):
        aq_ref[:, i * HEAD_DIM:(i + 1) * HEAD_DIM] = y.astype(_BF16)

    x = seg(IN_A_KV, 2 * A_KV_WIDTH)
    for i, y in enumerate(heads(x[:, :A_KV_WIDTH], gak_ref[...], A_KV_HEADS)):
        akv_ref[:, i * HEAD_DIM:(i + 1) * HEAD_DIM] = y.astype(_BF16)
    akv_ref[:, A_KV_WIDTH:] = x[:, A_KV_WIDTH:].astype(_BF16)

    chunk = 512
    for c in range(MIX_WIDTH // chunk):
        z = seg(IN_GATES + c * chunk, chunk)
        gates_ref[:, c * chunk:(c + 1) * chunk] = (z * jax.nn.sigmoid(z)).astype(_BF16)

    x = seg(IN_B_LAT, B_Q_LORA + B_KV_LORA)
    blat_ref[:, :B_Q_LORA] = _rms(x[:, :B_Q_LORA], gbq_ref[...]).astype(_BF16)
    blat_ref[:, B_Q_LORA:] = _rms(x[:, B_Q_LORA:], gbkv_ref[...]).astype(_BF16)

    x = seg(IN_C_QKV, 3 * C_WIDTH)
    for i, y in enumerate(heads(x[:, :C_WIDTH], gcq_ref[...], C_HEADS, scale)):
        cqkv_ref[:, i * HEAD_DIM:(i + 1) * HEAD_DIM] = y.astype(_BF16)
    for i, y in enumerate(heads(x[:, C_WIDTH:2 * C_WIDTH], gck_ref[...], C_HEADS)):
        cqkv_ref[:, C_WIDTH + i * HEAD_DIM:C_WIDTH + (i + 1) * HEAD_DIM] = y.astype(_BF16)
    cqkv_ref[:, 2 * C_WIDTH:] = x[:, 2 * C_WIDTH:].astype(_BF16)

    bkr_ref[...] = seg(IN_B_KR, 2 * B_ROPE)


def _in_proj(h, lw, layer, tm):
    t = h.shape[0]
    row = lambda i: (i, 0)
    vec = lambda n: pl.BlockSpec((None, 1, n), lambda i: (layer, 0, 0))
    widths = (A_WIDTH, 2 * A_KV_WIDTH, MIX_WIDTH, B_Q_LORA + B_KV_LORA, 3 * C_WIDTH)
    out_shape = [jax.ShapeDtypeStruct((t, w), _BF16) for w in widths]
    out_shape.append(jax.ShapeDtypeStruct((t, 2 * B_ROPE), _F32))
    out_specs = [pl.BlockSpec((tm, w), row) for w in widths] + [pl.BlockSpec((tm, 2 * B_ROPE), row)]
    return pl.pallas_call(
        _in_proj_kernel,
        grid=(t // tm,),
        in_specs=[
            pl.BlockSpec((tm, D_MODEL), row),
            vec(D_MODEL),
            _resident((None, D_MODEL, IN_WIDTH_PAD), lambda i: (layer, 0, 0)),
            vec(HEAD_DIM), vec(HEAD_DIM), vec(B_Q_LORA), vec(B_KV_LORA), vec(HEAD_DIM), vec(HEAD_DIM),
        ],
        out_specs=out_specs,
        out_shape=out_shape,
        compiler_params=_params(1),
        name="in_proj",
    )(h, lw['norm_in'], lw['w_in'], lw['a_q_norm'], lw['a_k_norm'], lw['b_cq_norm'], lw['b_ckv_norm'],
      lw['c_q_norm'], lw['c_k_norm'])


def _mla_prep_kernel(blat_ref, bkr_ref, wq_ref, wkv_ref, gq_ref, gkn_ref, gkr_ref, cos_ref, sin_ref,
                     q_ref, k_ref, v_ref):
    cqn = blat_ref[:, :B_Q_LORA]
    ckvn = blat_ref[:, B_Q_LORA:]
    cos2 = cos_ref[...]
    sin2 = sin_ref[...]
    lane = lax.broadcasted_iota(jnp.int32, (1, HEAD_DIM), 1)
    rope_lanes = lane < B_ROPE
    scale = B_QK ** -0.5 * LOG2E

    def rope(c):
        return c * cos2 + pltpu.roll(c, B_ROPE, 1) * sin2

    kr2 = bkr_ref[...]
    ss_kr = jnp.sum(jnp.where(rope_lanes, kr2 * kr2, 0.0), axis=-1, keepdims=True)
    k_rope = rope(kr2 * gkr_ref[...])

    for hd in range(B_HEADS):
        x = _dot(cqn, wq_ref[:, hd * B_QK_PAD:(hd + 1) * B_QK_PAD])
        a = x[:, :HEAD_DIM]
        c = x[:, HEAD_DIM:]
        ss = jnp.sum(a * a, axis=-1, keepdims=True) + jnp.sum(
            jnp.where(rope_lanes, c * c, 0.0), axis=-1, keepdims=True)
        r = lax.rsqrt(ss * (1.0 / B_QK) + EPS)
        q_ref[:, hd * B_QK_PAD:hd * B_QK_PAD + HEAD_DIM] = (
            a * r * gq_ref[:, :HEAD_DIM] * scale).astype(_BF16)
        q_ref[:, hd * B_QK_PAD + HEAD_DIM:(hd + 1) * B_QK_PAD] = (
            rope(c * r * gq_ref[:, HEAD_DIM:]) * scale).astype(_BF16)

        kv = _dot(ckvn, wkv_ref[:, hd * 2 * HEAD_DIM:(hd + 1) * 2 * HEAD_DIM])
        kn = kv[:, :B_NOPE]
        ss = jnp.sum(kn * kn, axis=-1, keepdims=True) + ss_kr
        r = lax.rsqrt(ss * (1.0 / B_QK) + EPS)
        k_ref[:, hd * B_QK_PAD:hd * B_QK_PAD + HEAD_DIM] = (kn * r * gkn_ref[...]).astype(_BF16)
        k_ref[:, hd * B_QK_PAD + HEAD_DIM:(hd + 1) * B_QK_PAD] = (k_rope * r).astype(_BF16)
        v_ref[:, hd * B_V:(hd + 1) * B_V] = kv[:, B_NOPE:].astype(_BF16)


def _mla_prep(blat, bkr, lw, tables, layer, seq, tm):
    t = blat.shape[0]
    row = lambda i: (i, 0)
    pos = lambda i: (i % (seq // tm), 0)
    vec = lambda n: pl.BlockSpec((None, 1, n), lambda i: (layer, 0, 0))
    return pl.pallas_call(
        _mla_prep_kernel,
        grid=(t // tm,),
        in_specs=[
            pl.BlockSpec((tm, B_Q_LORA + B_KV_LORA), row),
            pl.BlockSpec((tm, 2 * B_ROPE), row),
            _resident((None, B_Q_LORA, B_HEADS * B_QK_PAD), lambda i: (layer, 0, 0)),
            _resident((None, B_KV_LORA, B_HEADS * (B_NOPE + B_V)), lambda i: (layer, 0, 0)),
            vec(B_QK_PAD), vec(HEAD_DIM), vec(2 * B_ROPE),
            pl.BlockSpec((tm, 2 * B_ROPE), pos),
            pl.BlockSpec((tm, 2 * B_ROPE), pos),
        ],
        out_specs=[
            pl.BlockSpec((tm, B_HEADS * B_QK_PAD), row),
            pl.BlockSpec((tm, B_HEADS * B_QK_PAD), row),
            pl.BlockSpec((tm, B_WIDTH), row),
        ],
        out_shape=[
            jax.ShapeDtypeStruct((t, B_HEADS * B_QK_PAD), _BF16),
            jax.ShapeDtypeStruct((t, B_HEADS * B_QK_PAD), _BF16),
            jax.ShapeDtypeStruct((t, B_WIDTH), _BF16),
        ],
        compiler_params=_params(1),
        name="mla_prep",
    )(blat, bkr, lw['b_w_uq'], lw['b_w_ukv'], lw['b_q_gain'], lw['b_k_gain_nope'], lw['b_k_gain_rope'],
      tables['cos2'], tables['sin2'])


def _fill_v_ext(vext_ref, v_ref):
    width = v_ref.shape[1]
    vext_ref[:, :width] = v_ref[...]
    vext_ref[:, width:] = jnp.ones((v_ref.shape[0], vext_ref.shape[1] - width), vext_ref.dtype)


def _mla_attn_kernel(q_ref, k_ref, v_ref, g_ref, o_ref, vext_ref, *, tq):
    n_chunks = q_ref.shape[0] // tq
    _fill_v_ext(vext_ref, v_ref)

    def logits(i):
        return _dot_nt(q_ref[i * tq:(i + 1) * tq, :], k_ref[...])

    s_next = logits(0)
    for i in range(n_chunks):
        s = s_next
        if i + 1 < n_chunks:
            s_next = logits(i + 1)
        e = jnp.exp2(s - jnp.max(s, axis=-1, keepdims=True)).astype(_BF16)
        acc = _dot(e, vext_ref[...])
        rows = slice(i * tq, (i + 1) * tq)
        o = acc[:, :B_V] / acc[:, B_V:]
        o_ref[rows, :] = (o * g_ref[rows, :].astype(_F32)).astype(_BF16)


def _mla_attn(q, k, v, gates, batch, seq, tq):
    t = q.shape[0]
    gate_blk = A_WIDTH // B_V
    return pl.pallas_call(
        functools.partial(_mla_attn_kernel, tq=tq),
        grid=(batch, B_HEADS),
        in_specs=[
            pl.BlockSpec((seq, B_QK_PAD), lambda b, hd: (b, hd)),
            pl.BlockSpec((seq, B_QK_PAD), lambda b, hd: (b, hd)),
            pl.BlockSpec((seq, B_V), lambda b, hd: (b, hd)),
            pl.BlockSpec((seq, B_V), lambda b, hd: (b, gate_blk + hd)),
        ],
        out_specs=pl.BlockSpec((seq, B_V), lambda b, hd: (b, hd)),
        out_shape=jax.ShapeDtypeStruct((t, B_WIDTH), _BF16),
        scratch_shapes=[pltpu.VMEM((seq, 2 * B_V), _BF16)],
        compiler_params=_params(2),
        name="mla_attn",
    )(q, k, v, gates)


def _win_attn_kernel(sink_ref, slope_ref, q_ref, k_ref, v_ref, g_ref, o_ref, bias_ref, sinkcol_ref,
                     vext_ref):
    seq = q_ref.shape[0]
    nb = seq // A_BLOCK
    span = 3 * A_BLOCK
    kvh = pl.program_id(1)
    _fill_v_ext(vext_ref, v_ref)

    qi = lax.broadcasted_iota(jnp.int32, (A_BLOCK, span), 0)
    si = lax.broadcasted_iota(jnp.int32, (A_BLOCK, span), 1)
    for variant in range(3):
        dist = jnp.abs(si - qi - variant * A_BLOCK)
        valid = dist <= A_WINDOW
        for g in range(A_GROUP):
            slope = slope_ref[kvh * A_GROUP + g] * LOG2E
            bias_ref[variant, g * A_BLOCK:(g + 1) * A_BLOCK, :] = jnp.where(
                valid, -(slope * dist.astype(_F32)), NEG_INF)
    for g in range(A_GROUP):
        sinkcol_ref[g * A_BLOCK:(g + 1) * A_BLOCK, :] = jnp.full(
            (A_BLOCK, 1), sink_ref[kvh * A_GROUP + g] * LOG2E, _F32)

    def key_start(j):
        return min(max((j - 1) * A_BLOCK, 0), seq - span)

    def logits(j):
        qrows = slice(j * A_BLOCK, (j + 1) * A_BLOCK)
        qs = jnp.concatenate(
            [q_ref[qrows, g * HEAD_DIM:(g + 1) * HEAD_DIM] for g in range(A_GROUP)], axis=0)
        return _dot_nt(qs, k_ref[key_start(j):key_start(j) + span, :])

    s_next = logits(0)
    for j in range(nb):
        start = key_start(j)
        s = s_next + bias_ref[(j * A_BLOCK - start) // A_BLOCK]
        if j + 1 < nb:
            s_next = logits(j + 1)
        sink = sinkcol_ref[...]
        m = jnp.maximum(jnp.max(s, axis=-1, keepdims=True), sink)
        e = jnp.exp2(s - m).astype(_BF16)
        acc = _dot(e, vext_ref[start:start + span, :])
        o = acc[:, :HEAD_DIM] / (acc[:, HEAD_DIM:] + jnp.exp2(sink - m))
        qrows = slice(j * A_BLOCK, (j + 1) * A_BLOCK)
        for g in range(A_GROUP):
            cols = slice(g * HEAD_DIM, (g + 1) * HEAD_DIM)
            o_ref[qrows, cols] = (
                o[g * A_BLOCK:(g + 1) * A_BLOCK] * g_ref[qrows, cols].astype(_F32)).astype(_BF16)


def _win_attn(aq, akv, gates, sink, batch, seq):
    t = aq.shape[0]
    gw = A_GROUP * HEAD_DIM
    assert seq >= 3 * A_BLOCK
    slopes = jnp.exp2(-8.0 * jnp.arange(1, A_HEADS + 1, dtype=_F32) / A_HEADS)
    return pl.pallas_call(
        _win_attn_kernel,
        grid=(batch, A_KV_HEADS),
        in_specs=[
            pl.BlockSpec(memory_space=pltpu.SMEM),
            pl.BlockSpec(memory_space=pltpu.SMEM),
            pl.BlockSpec((seq, gw), lambda b, kh: (b, kh)),
            pl.BlockSpec((seq, HEAD_DIM), lambda b, kh: (b, kh)),
            pl.BlockSpec((seq, HEAD_DIM), lambda b, kh: (b, A_KV_HEADS + kh)),
            pl.BlockSpec((seq, gw), lambda b, kh: (b, kh)),
        ],
        out_specs=pl.BlockSpec((seq, gw), lambda b, kh: (b, kh)),
        out_shape=jax.ShapeDtypeStruct((t, A_WIDTH), _BF16),
        scratch_shapes=[
            pltpu.VMEM((3, A_GROUP * A_BLOCK, 3 * A_BLOCK), _F32),
            pltpu.VMEM((A_GROUP * A_BLOCK, 1), _F32),
            pltpu.VMEM((seq, 2 * HEAD_DIM), _BF16),
        ],
        compiler_params=_params(2),
        name="win_attn",
    )(sink, slopes, aq, akv, akv, gates)


def _na_attn_kernel(q_ref, k_ref, v_ref, g_ref, bias_ref, o_ref, vext_ref):
    seq = q_ref.shape[0]
    rows = seq // GRID_W
    wr = min(NA_ROWS_MAX, rows)
    span = wr * GRID_W
    group = min(8, rows)
    _fill_v_ext(vext_ref, v_ref)

    def body(i, carry):
        rws = [i * group + u for u in range(group)]
        starts = [jnp.clip(r - wr // 2, 0, rows - wr) for r in rws]
        qrows = [pl.ds(pl.multiple_of(r * GRID_W, GRID_W), GRID_W) for r in rws]
        krows = [pl.ds(pl.multiple_of(rs * GRID_W, GRID_W), span) for rs in starts]
        logits = [_dot_nt(q_ref[qrows[u], :], k_ref[krows[u], :]) for u in range(group)]
        probs = []
        for u in range(group):
            s = logits[u] + bias_ref[rws[u] - starts[u]]
            probs.append(jnp.exp2(s - jnp.max(s, axis=-1, keepdims=True)).astype(_BF16))
        accs = [_dot(probs[u], vext_ref[krows[u], :]) for u in range(group)]
        for u in range(group):
            o = accs[u][:, :HEAD_DIM] / accs[u][:, HEAD_DIM:]
            o_ref[qrows[u], :] = (o * g_ref[qrows[u], :].astype(_F32)).astype(_BF16)
        return carry

    lax.fori_loop(0, rows // group, body, 0)


def _na_attn(cqkv, gates, bias, batch, seq, layer):
    t = cqkv.shape[0]
    wr = bias.shape[2]
    gate_blk = (A_WIDTH + B_WIDTH) // HEAD_DIM
    blk = lambda off: pl.BlockSpec((seq, HEAD_DIM), lambda b, hd: (b, off + hd))
    return pl.pallas_call(
        _na_attn_kernel,
        grid=(batch, C_HEADS),
        in_specs=[
            blk(0), blk(C_HEADS), blk(2 * C_HEADS),
            pl.BlockSpec((seq, HEAD_DIM), lambda b, hd: (b, gate_blk + hd)),
            pl.BlockSpec((None, None, wr, GRID_W, wr * GRID_W), lambda b, hd: (layer, hd, 0, 0, 0)),
        ],
        out_specs=pl.BlockSpec((seq, HEAD_DIM), lambda b, hd: (b, hd)),
        out_shape=jax.ShapeDtypeStruct((t, C_WIDTH), _BF16),
        scratch_shapes=[pltpu.VMEM((seq, 2 * HEAD_DIM), _BF16)],
        compiler_params=_params(2),
        name="na_attn",
    )(cqkv, cqkv, cqkv, gates, bias)


def _out_ple_kernel(ya_ref, yb_ref, yc_ref, h_ref, p_ref, wo_ref, gple_ref, wg_ref, wp_ref, gpost_ref,
                    o_ref):
    acc = _dot(ya_ref[...], wo_ref[:A_WIDTH, :])
    acc += _dot(yb_ref[...], wo_ref[A_WIDTH:A_WIDTH + B_WIDTH, :])
    acc += _dot(yc_ref[...], wo_ref[A_WIDTH + B_WIDTH:, :])
    h1 = h_ref[...] + acc
    gate = jax.nn.sigmoid(_dot(_rms(h1, gple_ref[...]).astype(_BF16), wg_ref[...]))
    pp = _rms(_dot(p_ref[...].astype(_BF16), wp_ref[...]), gpost_ref[...])
    o_ref[...] = h1 + gate * pp


def _out_ple(ya, yb, yc, h, p, lw, layer, tm):
    t = h.shape[0]
    row = lambda i: (i, 0)
    vec = lambda n: pl.BlockSpec((None, 1, n), lambda i: (layer, 0, 0))
    mat = lambda r, c: _resident((None, r, c), lambda i: (layer, 0, 0))
    return pl.pallas_call(
        _out_ple_kernel,
        grid=(t // tm,),
        in_specs=[
            pl.BlockSpec((tm, A_WIDTH), row),
            pl.BlockSpec((tm, B_WIDTH), row),
            pl.BlockSpec((tm, C_WIDTH), row),
            pl.BlockSpec((tm, D_MODEL), row),
            pl.BlockSpec((None, tm, PLE_DIM), lambda i: (layer, i, 0)),
            mat(MIX_WIDTH, D_MODEL), vec(D_MODEL), mat(D_MODEL, D_MODEL), mat(PLE_DIM, D_MODEL),
            vec(D_MODEL),
        ],
        out_specs=pl.BlockSpec((tm, D_MODEL), row),
        out_shape=jax.ShapeDtypeStruct((t, D_MODEL), _F32),
        compiler_params=_params(1),
        name="out_ple",
    )(ya, yb, yc, h, p, lw['w_out'], lw['ple_norm'], lw['w_ple_gate'], lw['w_ple_proj'],
      lw['ple_post_norm'])


def _swap_halves(x):
    half = x.shape[-1] // 2
    return jnp.concatenate([x[..., half:], x[..., :half]], axis=-1)


def _prepare_weights(W):
    depth = W['w_in'].shape[0]
    sizes = (A_WIDTH, A_KV_WIDTH, A_KV_WIDTH, A_WIDTH, B_Q_LORA, B_KV_LORA, B_ROPE, B_WIDTH,
             C_WIDTH, C_WIDTH, C_WIDTH, C_WIDTH)
    offs = np.concatenate([[0], np.cumsum(sizes)])
    aq, ak, av, az, bcq, bckv, bkr, bz, cq, ck, cv, cz = [
        W['w_in'][:, :, offs[i]:offs[i + 1]] for i in range(len(sizes))]
    w_in = jnp.concatenate(
        [aq, ak, av, az, bz, cz, bcq, bckv, cq, ck, cv, bkr, _swap_halves(bkr)], axis=-1).astype(_BF16)

    wq = W['b_w_uq'].reshape(depth, B_Q_LORA, B_HEADS, B_QK)
    wq = jnp.concatenate([wq, _swap_halves(wq[..., B_NOPE:])], axis=-1)
    wq = wq.reshape(depth, B_Q_LORA, B_HEADS * B_QK_PAD).astype(_BF16)
    gq = W['b_q_norm']
    gk = W['b_k_norm']
    row = lambda g: g[:, None, :].astype(_F32)
    return {
        'norm_in': row(W['norm_in']), 'w_in': w_in,
        'a_q_norm': row(W['a_q_norm']), 'a_k_norm': row(W['a_k_norm']), 'a_sink': W['a_sink'].astype(_F32),
        'b_cq_norm': row(W['b_cq_norm']), 'b_ckv_norm': row(W['b_ckv_norm']),
        'b_w_uq': wq, 'b_w_ukv': W['b_w_ukv'].astype(_BF16),
        'b_q_gain': row(jnp.concatenate([gq, _swap_halves(gq[:, B_NOPE:])], axis=-1)),
        'b_k_gain_nope': row(gk[:, :B_NOPE]),
        'b_k_gain_rope': row(jnp.concatenate([gk[:, B_NOPE:], _swap_halves(gk[:, B_NOPE:])], axis=-1)),
        'c_q_norm': row(W['c_q_norm']), 'c_k_norm': row(W['c_k_norm']),
        'w_out': W['w_out'].astype(_BF16), 'ple_norm': row(W['ple_norm']),
        'w_ple_gate': W['w_ple_gate'].astype(_BF16), 'w_ple_proj': W['w_ple_proj'].astype(_BF16),
        'ple_post_norm': row(W['ple_post_norm']),
    }


def _rope_tables(seq):
    half = B_ROPE // 2
    inv = ROPE_THETA ** (-jnp.arange(half, dtype=_F32) / half)
    ang = jnp.arange(seq).astype(_F32)[:, None] * inv[None, :]
    cos, sin = jnp.cos(ang), jnp.sin(ang)
    zeros = jnp.zeros((seq, B_ROPE), _F32)
    return {'cos2': jnp.concatenate([cos, cos, zeros], axis=-1),
            'sin2': jnp.concatenate([-sin, sin, zeros], axis=-1)}


def _na_bias_table(rpb, seq):
    rows = seq // GRID_W
    wr = min(NA_ROWS_MAX, rows)
    c = np.arange(GRID_W)
    cs = np.clip(c - NA_COLS // 2, 0, GRID_W - NA_COLS)
    col_valid = (c[None, :] >= cs[:, None]) & (c[None, :] < cs[:, None] + NA_COLS)
    dc = np.clip(c[None, :] - c[:, None] + NA_COLS - 1, 0, 2 * NA_COLS - 2)
    by_dc = jnp.where(col_valid, rpb.astype(_F32)[:, :, :, dc] * LOG2E, NEG_INF)
    top = NA_ROWS_MAX - 1
    bias = jnp.stack([by_dc[:, :, top - pat:top - pat + wr] for pat in range(wr)], axis=2)
    bias = bias.transpose(0, 1, 2, 4, 3, 5)
    return bias.reshape(rpb.shape[0], C_HEADS, wr, GRID_W, wr * GRID_W)


def _trunk(x, p, lw, tables, na_bias):
    batch, seq, _ = x.shape
    depth = p.shape[0]
    t = batch * seq
    h = x.reshape(t, D_MODEL)
    p = p.reshape(depth, t, PLE_DIM)
    tm = min(256, seq)
    for layer in range(depth):
        aq, akv, gates, blat, cqkv, bkr = _in_proj(h, lw, layer, tm)
        bq, bk, bv = _mla_prep(blat, bkr, lw, tables, layer, seq, tm)
        ya = _win_attn(aq, akv, gates, lw['a_sink'][layer], batch, seq)
        yb = _mla_attn(bq, bk, bv, gates, batch, seq, min(256, seq))
        yc = _na_attn(cqkv, gates, na_bias, batch, seq, layer)
        h = _out_ple(ya, yb, yc, h, p, lw, layer, tm)
    return h.reshape(batch, seq, D_MODEL)


def kernel(x_prompt, x_sample, p_prompt, p_sample, norm_in, w_in, a_q_norm, a_k_norm, a_sink, b_cq_norm, b_ckv_norm, b_w_uq, b_w_ukv, b_q_norm, b_k_norm, c_q_norm, c_k_norm, c_rpb, w_out, ple_norm, w_ple_gate, w_ple_proj, ple_post_norm):
    lw = _prepare_weights({
        'norm_in': norm_in, 'w_in': w_in, 'a_q_norm': a_q_norm, 'a_k_norm': a_k_norm, 'a_sink': a_sink,
        'b_cq_norm': b_cq_norm, 'b_ckv_norm': b_ckv_norm, 'b_w_uq': b_w_uq, 'b_w_ukv': b_w_ukv,
        'b_q_norm': b_q_norm, 'b_k_norm': b_k_norm, 'c_q_norm': c_q_norm, 'c_k_norm': c_k_norm,
        'w_out': w_out, 'ple_norm': ple_norm, 'w_ple_gate': w_ple_gate, 'w_ple_proj': w_ple_proj,
        'ple_post_norm': ple_post_norm,
    })
    tables = {}
    outs = []
    for x, p in ((x_prompt, p_prompt), (x_sample, p_sample)):
        seq = x.shape[1]
        if seq not in tables:
            tables[seq] = (_rope_tables(seq), _na_bias_table(c_rpb, seq))
        outs.append(_trunk(x, p, lw, *tables[seq]))
    return tuple(outs)
```

```python
import functools

import jax
import jax.numpy as jnp
import numpy as np
from jax import lax
from jax.experimental import pallas as pl
from jax.experimental.pallas import tpu as pltpu

D_MODEL = 2048
PLE_DIM = 256
HEAD_DIM = 128
A_HEADS = 6
A_KV_HEADS = 2
A_GROUP = A_HEADS // A_KV_HEADS
A_WINDOW = 128
A_BLOCK = 128
B_HEADS = 6
B_Q_LORA = 512
B_KV_LORA = 512
B_NOPE = 128
B_ROPE = 64
B_V = 128
B_QK = B_NOPE + B_ROPE
ROPE_THETA = 10000.0
C_HEADS = 4
GRID_W = 64
NA_ROWS_MAX = 8
NA_COLS = 16
A_WIDTH = A_HEADS * HEAD_DIM
A_KV_WIDTH = A_KV_HEADS * HEAD_DIM
B_WIDTH = B_HEADS * B_V
C_WIDTH = C_HEADS * HEAD_DIM
MIX_WIDTH = A_WIDTH + B_WIDTH + C_WIDTH
EPS = 1e-6
NEG_INF = -1e30
LOG2E = 1.4426950408889634

B_QK_PAD = 2 * HEAD_DIM
IN_A_Q = 0
IN_A_KV = IN_A_Q + A_WIDTH
IN_GATES = IN_A_KV + 2 * A_KV_WIDTH
IN_B_LAT = IN_GATES + MIX_WIDTH
IN_C_QKV = IN_B_LAT + B_Q_LORA + B_KV_LORA
IN_B_KR = IN_C_QKV + 3 * C_WIDTH
IN_WIDTH_PAD = IN_B_KR + 2 * B_ROPE

VMEM_LIMIT_BYTES = 56 * 1024 * 1024
IN_PROJ_ROWS = 256
OUT_PLE_ROWS = 512
MLA_QUERY_ROWS = 256

_BF16 = jnp.bfloat16
_F32 = jnp.float32


def _params(n_grid_dims):
    return pltpu.CompilerParams(
        dimension_semantics=("arbitrary",) * n_grid_dims,
        vmem_limit_bytes=VMEM_LIMIT_BYTES)


def _resident(shape, index_map):
    return pl.BlockSpec(shape, index_map, pipeline_mode=pl.Buffered(1))


def _rms(x, g):
    return x * lax.rsqrt(jnp.mean(x * x, axis=-1, keepdims=True) + EPS) * g


def _dot(a, b):
    return jnp.dot(a, b, preferred_element_type=_F32)


def _dot_nt(a, b):
    return lax.dot_general(a, b, (((1,), (1,)), ((), ())), preferred_element_type=_F32)


def _in_proj_kernel(h_ref, gin_ref, w_ref, gaq_ref, gak_ref, gbq_ref, gbkv_ref, gcq_ref, gck_ref,
                    wq_ref, wkv_ref, gq_ref, gkn_ref, gkr_ref, cos_ref, sin_ref,
                    aq_ref, akv_ref, gates_ref, cqkv_ref, bq_ref, bk_ref, bv_ref):
    u = _rms(h_ref[...], gin_ref[...]).astype(_BF16)
    scale = HEAD_DIM ** -0.5 * LOG2E
    b_scale = B_QK ** -0.5 * LOG2E

    def seg(start, width):
        return _dot(u, w_ref[:, start:start + width])

    def heads(x, g, n, s=None):
        outs = []
        for i in range(n):
            y = _rms(x[:, i * HEAD_DIM:(i + 1) * HEAD_DIM], g)
            outs.append(y if s is None else y * s)
        return outs

    x = seg(IN_B_LAT, B_Q_LORA + B_KV_LORA)
    kr2 = seg(IN_B_KR, 2 * B_ROPE)
    cqn = _rms(x[:, :B_Q_LORA], gbq_ref[...]).astype(_BF16)
    ckvn = _rms(x[:, B_Q_LORA:], gbkv_ref[...]).astype(_BF16)

    chunk = 512
    for c in range(MIX_WIDTH // chunk):
        z = seg(IN_GATES + c * chunk, chunk)
        gates_ref[:, c * chunk:(c + 1) * chunk] = (z * jax.nn.sigmoid(z)).astype(_BF16)

    xq = _dot(cqn, wq_ref[...])
    xkv = _dot(ckvn, wkv_ref[...])

    xa = seg(IN_A_Q, A_WIDTH)
    xakv = seg(IN_A_KV, 2 * A_KV_WIDTH)
    xc = seg(IN_C_QKV, 3 * C_WIDTH)

    cos2 = cos_ref[...]
    sin2 = sin_ref[...]
    rope_lanes = lax.broadcasted_iota(jnp.int32, (1, HEAD_DIM), 1) < B_ROPE

    def rope(c):
        return c * cos2 + pltpu.roll(c, B_ROPE, 1) * sin2

    def sumsq(y, lanes=None):
        y2 = y * y
        return jnp.sum(y2 if lanes is None else jnp.where(lanes, y2, 0.0), axis=-1, keepdims=True)

    ss_kr = sumsq(kr2, rope_lanes)
    k_rope = rope(kr2 * gkr_ref[...])
    for hd in range(B_HEADS):
        lo, mid, hi = hd * B_QK_PAD, hd * B_QK_PAD + HEAD_DIM, (hd + 1) * B_QK_PAD
        a = xq[:, lo:mid]
        c = xq[:, mid:hi]
        r = lax.rsqrt((sumsq(a) + sumsq(c, rope_lanes)) * (1.0 / B_QK) + EPS)
        bq_ref[:, lo:mid] = (a * r * gq_ref[:, :HEAD_DIM] * b_scale).astype(_BF16)
        bq_ref[:, mid:hi] = (rope(c * r * gq_ref[:, HEAD_DIM:]) * b_scale).astype(_BF16)
        kn = xkv[:, lo:mid]
        r = lax.rsqrt((sumsq(kn) + ss_kr) * (1.0 / B_QK) + EPS)
        bk_ref[:, lo:mid] = (kn * r * gkn_ref[...]).astype(_BF16)
        bk_ref[:, mid:hi] = (k_rope * r).astype(_BF16)
        bv_ref[:, hd * B_V:(hd + 1) * B_V] = xkv[:, mid:hi].astype(_BF16)

    for i, y in enumerate(heads(xa, gaq_ref[...], A_HEADS, scale)):
        aq_ref[:, i * HEAD_DIM:(i + 1) * HEAD_DIM] = y.astype(_BF16)
    for i, y in enumerate(heads(xakv[:, :A_KV_WIDTH], gak_ref[...], A_KV_HEADS)):
        akv_ref[:, i * HEAD_DIM:(i + 1) * HEAD_DIM] = y.astype(_BF16)
    akv_ref[:, A_KV_WIDTH:] = xakv[:, A_KV_WIDTH:].astype(_BF16)
    for i, y in enumerate(heads(xc[:, :C_WIDTH], gcq_ref[...], C_HEADS, scale)):
        cqkv_ref[:, i * HEAD_DIM:(i + 1) * HEAD_DIM] = y.astype(_BF16)
    for i, y in enumerate(heads(xc[:, C_WIDTH:2 * C_WIDTH], gck_ref[...], C_HEADS)):
        cqkv_ref[:, C_WIDTH + i * HEAD_DIM:C_WIDTH + (i + 1) * HEAD_DIM] = y.astype(_BF16)
    cqkv_ref[:, 2 * C_WIDTH:] = xc[:, 2 * C_WIDTH:].astype(_BF16)


def _in_proj(h, lw, tables, layer, seq, tm):
    t = h.shape[0]
    row = lambda i: (i, 0)
    pos = lambda i: (i % (seq // tm), 0)
    vec = lambda n: pl.BlockSpec((None, 1, n), lambda i: (layer, 0, 0))
    mat = lambda r, c: _resident((None, r, c), lambda i: (layer, 0, 0))
    widths = (A_WIDTH, 2 * A_KV_WIDTH, MIX_WIDTH, 3 * C_WIDTH, B_HEADS * B_QK_PAD, B_HEADS * B_QK_PAD,
              B_WIDTH)
    return pl.pallas_call(
        _in_proj_kernel,
        grid=(t // tm,),
        in_specs=[
            pl.BlockSpec((tm, D_MODEL), row),
            vec(D_MODEL),
            mat(D_MODEL, IN_WIDTH_PAD),
            vec(HEAD_DIM), vec(HEAD_DIM), vec(B_Q_LORA), vec(B_KV_LORA), vec(HEAD_DIM), vec(HEAD_DIM),
            mat(B_Q_LORA, B_HEADS * B_QK_PAD), mat(B_KV_LORA, B_HEADS * (B_NOPE + B_V)),
            vec(B_QK_PAD), vec(HEAD_DIM), vec(2 * B_ROPE),
            pl.BlockSpec((tm, 2 * B_ROPE), pos),
            pl.BlockSpec((tm, 2 * B_ROPE), pos),
        ],
        out_specs=[pl.BlockSpec((tm, w), row) for w in widths],
        out_shape=[jax.ShapeDtypeStruct((t, w), _BF16) for w in widths],
        compiler_params=_params(1),
        name="in_proj",
    )(h, lw['norm_in'], lw['w_in'], lw['a_q_norm'], lw['a_k_norm'], lw['b_cq_norm'], lw['b_ckv_norm'],
      lw['c_q_norm'], lw['c_k_norm'], lw['b_w_uq'], lw['b_w_ukv'], lw['b_q_gain'], lw['b_k_gain_nope'],
      lw['b_k_gain_rope'], tables['cos2'], tables['sin2'])


def _fill_v_ext(vext_ref, v_ref):
    width = v_ref.shape[1]
    vext_ref[:, :width] = v_ref[...]
    vext_ref[:, width:] = jnp.ones((v_ref.shape[0], vext_ref.shape[1] - width), vext_ref.dtype)


def _mla_attn_kernel(q_ref, k_ref, v_ref, g_ref, o_ref, vext_ref, *, tq):
    n_chunks = q_ref.shape[0] // tq
    _fill_v_ext(vext_ref, v_ref)

    def logits(i):
        return _dot_nt(q_ref[i * tq:(i + 1) * tq, :], k_ref[...])

    s_next = logits(0)
    for i in range(n_chunks):
        s = s_next
        if i + 1 < n_chunks:
            s_next = logits(i + 1)
        e = jnp.exp2(s - jnp.max(s, axis=-1, keepdims=True)).astype(_BF16)
        acc = _dot(e, vext_ref[...])
        rows = slice(i * tq, (i + 1) * tq)
        o = acc[:, :B_V] / acc[:, B_V:]
        o_ref[rows, :] = (o * g_ref[rows, :].astype(_F32)).astype(_BF16)


def _mla_attn(q, k, v, gates, batch, seq, tq):
    t = q.shape[0]
    gate_blk = A_WIDTH // B_V
    return pl.pallas_call(
        functools.partial(_mla_attn_kernel, tq=tq),
        grid=(batch, B_HEADS),
        in_specs=[
            pl.BlockSpec((seq, B_QK_PAD), lambda b, hd: (b, hd)),
            pl.BlockSpec((seq, B_QK_PAD), lambda b, hd: (b, hd)),
            pl.BlockSpec((seq, B_V), lambda b, hd: (b, hd)),
            pl.BlockSpec((seq, B_V), lambda b, hd: (b, gate_blk + hd)),
        ],
        out_specs=pl.BlockSpec((seq, B_V), lambda b, hd: (b, hd)),
        out_shape=jax.ShapeDtypeStruct((t, B_WIDTH), _BF16),
        scratch_shapes=[pltpu.VMEM((seq, 2 * B_V), _BF16)],
        compiler_params=_params(2),
        name="mla_attn",
    )(q, k, v, gates)


def _win_attn_kernel(sink_ref, slope_ref, q_ref, k_ref, v_ref, g_ref, o_ref, bias_ref, sinkcol_ref,
                     vext_ref):
    seq = q_ref.shape[0]
    nb = seq // A_BLOCK
    span = 3 * A_BLOCK
    kvh = pl.program_id(1)
    _fill_v_ext(vext_ref, v_ref)

    qi = lax.broadcasted_iota(jnp.int32, (A_BLOCK, span), 0)
    si = lax.broadcasted_iota(jnp.int32, (A_BLOCK, span), 1)
    for variant in range(3):
        dist = jnp.abs(si - qi - variant * A_BLOCK)
        valid = dist <= A_WINDOW
        for g in range(A_GROUP):
            slope = slope_ref[kvh * A_GROUP + g] * LOG2E
            bias_ref[variant, g * A_BLOCK:(g + 1) * A_BLOCK, :] = jnp.where(
                valid, -(slope * dist.astype(_F32)), NEG_INF)
    for g in range(A_GROUP):
        sinkcol_ref[g * A_BLOCK:(g + 1) * A_BLOCK, :] = jnp.full(
            (A_BLOCK, 1), sink_ref[kvh * A_GROUP + g] * LOG2E, _F32)

    def key_start(j):
        return min(max((j - 1) * A_BLOCK, 0), seq - span)

    def logits(j):
        qrows = slice(j * A_BLOCK, (j + 1) * A_BLOCK)
        qs = jnp.concatenate(
            [q_ref[qrows, g * HEAD_DIM:(g + 1) * HEAD_DIM] for g in range(A_GROUP)], axis=0)
        return _dot_nt(qs, k_ref[key_start(j):key_start(j) + span, :])

    s_next = logits(0)
    for j in range(nb):
        start = key_start(j)
        s = s_next + bias_ref[(j * A_BLOCK - start) // A_BLOCK]
        if j + 1 < nb:
            s_next = logits(j + 1)
        sink = sinkcol_ref[...]
        m = jnp.maximum(jnp.max(s, axis=-1, keepdims=True), sink)
        e = jnp.exp2(s - m).astype(_BF16)
        acc = _dot(e, vext_ref[start:start + span, :])
        o = acc[:, :HEAD_DIM] / (acc[:, HEAD_DIM:] + jnp.exp2(sink - m))
        qrows = slice(j * A_BLOCK, (j + 1) * A_BLOCK)
        for g in range(A_GROUP):
            cols = slice(g * HEAD_DIM, (g + 1) * HEAD_DIM)
            o_ref[qrows, cols] = (
                o[g * A_BLOCK:(g + 1) * A_BLOCK] * g_ref[qrows, cols].astype(_F32)).astype(_BF16)


def _win_attn(aq, akv, gates, sink, batch, seq):
    t = aq.shape[0]
    gw = A_GROUP * HEAD_DIM
    assert seq >= 3 * A_BLOCK
    slopes = jnp.exp2(-8.0 * jnp.arange(1, A_HEADS + 1, dtype=_F32) / A_HEADS)
    return pl.pallas_call(
        _win_attn_kernel,
        grid=(batch, A_KV_HEADS),
        in_specs=[
            pl.BlockSpec(memory_space=pltpu.SMEM),
            pl.BlockSpec(memory_space=pltpu.SMEM),
            pl.BlockSpec((seq, gw), lambda b, kh: (b, kh)),
            pl.BlockSpec((seq, HEAD_DIM), lambda b, kh: (b, kh)),
            pl.BlockSpec((seq, HEAD_DIM), lambda b, kh: (b, A_KV_HEADS + kh)),
            pl.BlockSpec((seq, gw), lambda b, kh: (b, kh)),
        ],
        out_specs=pl.BlockSpec((seq, gw), lambda b, kh: (b, kh)),
        out_shape=jax.ShapeDtypeStruct((t, A_WIDTH), _BF16),
        scratch_shapes=[
            pltpu.VMEM((3, A_GROUP * A_BLOCK, 3 * A_BLOCK), _F32),
            pltpu.VMEM((A_GROUP * A_BLOCK, 1), _F32),
            pltpu.VMEM((seq, 2 * HEAD_DIM), _BF16),
        ],
        compiler_params=_params(2),
        name="win_attn",
    )(sink, slopes, aq, akv, akv, gates)


def _na_attn_kernel(q_ref, k_ref, v_ref, g_ref, bias_ref, o_ref, vext_ref):
    seq = q_ref.shape[0]
    rows = seq // GRID_W
    wr = min(NA_ROWS_MAX, rows)
    span = wr * GRID_W
    group = min(8, rows)
    _fill_v_ext(vext_ref, v_ref)

    def body(i, carry):
        rws = [i * group + u for u in range(group)]
        starts = [jnp.clip(r - wr // 2, 0, rows - wr) for r in rws]
        qrows = [pl.ds(pl.multiple_of(r * GRID_W, GRID_W), GRID_W) for r in rws]
        krows = [pl.ds(pl.multiple_of(rs * GRID_W, GRID_W), span) for rs in starts]
        logits = [_dot_nt(q_ref[qrows[u], :], k_ref[krows[u], :]) for u in range(group)]
        probs = []
        for u in range(group):
            s = logits[u] + bias_ref[rws[u] - starts[u]]
            probs.append(jnp.exp2(s - jnp.max(s, axis=-1, keepdims=True)).astype(_BF16))
        accs = [_dot(probs[u], vext_ref[krows[u], :]) for u in range(group)]
        for u in range(group):
            o = accs[u][:, :HEAD_DIM] / accs[u][:, HEAD_DIM:]
            o_ref[qrows[u], :] = (o * g_ref[qrows[u], :].astype(_F32)).astype(_BF16)
        return carry

    lax.fori_loop(0, rows // group, body, 0)


def _na_attn(cqkv, gates, bias, batch, seq, layer):
    t = cqkv.shape[0]
    wr = bias.shape[2]
    gate_blk = (A_WIDTH + B_WIDTH) // HEAD_DIM
    blk = lambda off: pl.BlockSpec((seq, HEAD_DIM), lambda b, hd: (b, off + hd))
    return pl.pallas_call(
        _na_attn_kernel,
        grid=(batch, C_HEADS),
        in_specs=[
            blk(0), blk(C_HEADS), blk(2 * C_HEADS),
            pl.BlockSpec((seq, HEAD_DIM), lambda b, hd: (b, gate_blk + hd)),
            pl.BlockSpec((None, None, wr, GRID_W, wr * GRID_W), lambda b, hd: (layer, hd, 0, 0, 0)),
        ],
        out_specs=pl.BlockSpec((seq, HEAD_DIM), lambda b, hd: (b, hd)),
        out_shape=jax.ShapeDtypeStruct((t, C_WIDTH), _BF16),
        scratch_shapes=[pltpu.VMEM((seq, 2 * HEAD_DIM), _BF16)],
        compiler_params=_params(2),
        name="na_attn",
    )(cqkv, cqkv, cqkv, gates, bias)


def _out_ple_kernel(ya_ref, yb_ref, yc_ref, h_ref, p_ref, wo_ref, gple_ref, wg_ref, wp_ref, gpost_ref,
                    o_ref):
    tm = h_ref.shape[0]
    halves = [slice(i * (tm // 2), (i + 1) * (tm // 2)) for i in range(2)]
    pp = [_rms(_dot(p_ref[rows, :].astype(_BF16), wp_ref[...]), gpost_ref[...]) for rows in halves]
    h1 = []
    for rows in halves:
        acc = _dot(ya_ref[rows, :], wo_ref[:A_WIDTH, :])
        acc += _dot(yb_ref[rows, :], wo_ref[A_WIDTH:A_WIDTH + B_WIDTH, :])
        acc += _dot(yc_ref[rows, :], wo_ref[A_WIDTH + B_WIDTH:, :])
        h1.append(h_ref[rows, :] + acc)
    logit = [_dot(_rms(x, gple_ref[...]).astype(_BF16), wg_ref[...]) for x in h1]
    for i, rows in enumerate(halves):
        o_ref[rows, :] = h1[i] + jax.nn.sigmoid(logit[i]) * pp[i]


def _out_ple(ya, yb, yc, h, p, lw, layer, tm):
    t = h.shape[0]
    row = lambda i: (i, 0)
    vec = lambda n: pl.BlockSpec((None, 1, n), lambda i: (layer, 0, 0))
    mat = lambda r, c: _resident((None, r, c), lambda i: (layer, 0, 0))
    return pl.pallas_call(
        _out_ple_kernel,
        grid=(t // tm,),
        in_specs=[
            pl.BlockSpec((tm, A_WIDTH), row),
            pl.BlockSpec((tm, B_WIDTH), row),
            pl.BlockSpec((tm, C_WIDTH), row),
            pl.BlockSpec((tm, D_MODEL), row),
            pl.BlockSpec((None, tm, PLE_DIM), lambda i: (layer, i, 0)),
            mat(MIX_WIDTH, D_MODEL), vec(D_MODEL), mat(D_MODEL, D_MODEL), mat(PLE_DIM, D_MODEL),
            vec(D_MODEL),
        ],
        out_specs=pl.BlockSpec((tm, D_MODEL), row),
        out_shape=jax.ShapeDtypeStruct((t, D_MODEL), _F32),
        compiler_params=_params(1),
        name="out_ple",
    )(ya, yb, yc, h, p, lw['w_out'], lw['ple_norm'], lw['w_ple_gate'], lw['w_ple_proj'],
      lw['ple_post_norm'])


def _swap_halves(x):
    half = x.shape[-1] // 2
    return jnp.concatenate([x[..., half:], x[..., :half]], axis=-1)


def _prepare_weights(W):
    depth = W['w_in'].shape[0]
    sizes = (A_WIDTH, A_KV_WIDTH, A_KV_WIDTH, A_WIDTH, B_Q_LORA, B_KV_LORA, B_ROPE, B_WIDTH,
             C_WIDTH, C_WIDTH, C_WIDTH, C_WIDTH)
    offs = np.concatenate([[0], np.cumsum(sizes)])
    w_in_bf16 = W['w_in'].astype(_BF16)
    aq, ak, av, az, bcq, bckv, bkr, bz, cq, ck, cv, cz = [
        w_in_bf16[:, :, offs[i]:offs[i + 1]] for i in range(len(sizes))]
    w_in = jnp.concatenate(
        [aq, ak, av, az, bz, cz, bcq, bckv, cq, ck, cv, bkr, _swap_halves(bkr)], axis=-1)

    wq = W['b_w_uq'].reshape(depth, B_Q_LORA, B_HEADS, B_QK)
    wq = jnp.concatenate([wq, _swap_halves(wq[..., B_NOPE:])], axis=-1)
    wq = wq.reshape(depth, B_Q_LORA, B_HEADS * B_QK_PAD).astype(_BF16)
    gq = W['b_q_norm']
    gk = W['b_k_norm']
    row = lambda g: g[:, None, :].astype(_F32)
    return {
        'norm_in': row(W['norm_in']), 'w_in': w_in,
        'a_q_norm': row(W['a_q_norm']), 'a_k_norm': row(W['a_k_norm']), 'a_sink': W['a_sink'].astype(_F32),
        'b_cq_norm': row(W['b_cq_norm']), 'b_ckv_norm': row(W['b_ckv_norm']),
        'b_w_uq': wq, 'b_w_ukv': W['b_w_ukv'].astype(_BF16),
        'b_q_gain': row(jnp.concatenate([gq, _swap_halves(gq[:, B_NOPE:])], axis=-1)),
        'b_k_gain_nope': row(gk[:, :B_NOPE]),
        'b_k_gain_rope': row(jnp.concatenate([gk[:, B_NOPE:], _swap_halves(gk[:, B_NOPE:])], axis=-1)),
        'c_q_norm': row(W['c_q_norm']), 'c_k_norm': row(W['c_k_norm']),
        'w_out': W['w_out'].astype(_BF16), 'ple_norm': row(W['ple_norm']),
        'w_ple_gate': W['w_ple_gate'].astype(_BF16), 'w_ple_proj': W['w_ple_proj'].astype(_BF16),
        'ple_post_norm': row(W['ple_post_norm']),
    }


def _rope_tables(seq):
    half = B_ROPE // 2
    inv = ROPE_THETA ** (-jnp.arange(half, dtype=_F32) / half)
    ang = jnp.arange(seq).astype(_F32)[:, None] * inv[None, :]
    cos, sin = jnp.cos(ang), jnp.sin(ang)
    zeros = jnp.zeros((seq, B_ROPE), _F32)
    return {'cos2': jnp.concatenate([cos, cos, zeros], axis=-1),
            'sin2': jnp.concatenate([-sin, sin, zeros], axis=-1)}


def _na_bias_table(rpb, seq):
    rows = seq // GRID_W
    wr = min(NA_ROWS_MAX, rows)
    c = np.arange(GRID_W)
    cs = np.clip(c - NA_COLS // 2, 0, GRID_W - NA_COLS)
    col_valid = (c[None, :] >= cs[:, None]) & (c[None, :] < cs[:, None] + NA_COLS)
    dc = np.clip(c[None, :] - c[:, None] + NA_COLS - 1, 0, 2 * NA_COLS - 2)
    by_dc = jnp.where(col_valid, rpb.astype(_F32)[:, :, :, dc] * LOG2E, NEG_INF)
    top = NA_ROWS_MAX - 1
    bias = jnp.stack([by_dc[:, :, top - pat:top - pat + wr] for pat in range(wr)], axis=2)
    bias = bias.transpose(0, 1, 2, 4, 3, 5)
    return bias.reshape(rpb.shape[0], C_HEADS, wr, GRID_W, wr * GRID_W)


def _trunk(x, p, lw, tables, na_bias):
    batch, seq, _ = x.shape
    depth = p.shape[0]
    t = batch * seq
    h = x.reshape(t, D_MODEL)
    p = p.reshape(depth, t, PLE_DIM)
    for layer in range(depth):
        aq, akv, gates, cqkv, bq, bk, bv = _in_proj(h, lw, tables, layer, seq, IN_PROJ_ROWS)
        ya = _win_attn(aq, akv, gates, lw['a_sink'][layer], batch, seq)
        yb = _mla_attn(bq, bk, bv, gates, batch, seq, MLA_QUERY_ROWS)
        yc = _na_attn(cqkv, gates, na_bias, batch, seq, layer)
        h = _out_ple(ya, yb, yc, h, p, lw, layer, OUT_PLE_ROWS)
    return h.reshape(batch, seq, D_MODEL)


def kernel(x_prompt, x_sample, p_prompt, p_sample, norm_in, w_in, a_q_norm, a_k_norm, a_sink, b_cq_norm, b_ckv_norm, b_w_uq, b_w_ukv, b_q_norm, b_k_norm, c_q_norm, c_k_norm, c_rpb, w_out, ple_norm, w_ple_gate, w_ple_proj, ple_post_norm):
    lw = _prepare_weights({
        'norm_in': norm_in, 'w_in': w_in, 'a_q_norm': a_q_norm, 'a_k_norm': a_k_norm, 'a_sink': a_sink,
        'b_cq_norm': b_cq_norm, 'b_ckv_norm': b_ckv_norm, 'b_w_uq': b_w_uq, 'b_w_ukv': b_w_ukv,
        'b_q_norm': b_q_norm, 'b_k_norm': b_k_norm, 'c_q_norm': c_q_norm, 'c_k_norm': c_k_norm,
        'w_out': w_out, 'ple_norm': ple_norm, 'w_ple_gate': w_ple_gate, 'w_ple_proj': w_ple_proj,
        'ple_post_norm': ple_post_norm,
    })
    tables = {}
    outs = []
    for x, p in ((x_prompt, p_prompt), (x_sample, p_sample)):
        seq = x.shape[1]
        if seq not in tables:
            tables[seq] = (_rope_tables(seq), _na_bias_table(c_rpb, seq))
        outs.append(_trunk(x, p, lw, *tables[seq]))
    return tuple(outs)
```

```python
import functools

import jax
import jax.numpy as jnp
import numpy as np
from jax import lax
from jax.experimental import pallas as pl
from jax.experimental.pallas import tpu as pltpu

D_MODEL = 2048
PLE_DIM = 256
HEAD_DIM = 128
A_HEADS = 6
A_KV_HEADS = 2
A_GROUP = A_HEADS // A_KV_HEADS
A_WINDOW = 128
A_BLOCK = 128
B_HEADS = 6
B_Q_LORA = 512
B_KV_LORA = 512
B_NOPE = 128
B_ROPE = 64
B_V = 128
B_QK = B_NOPE + B_ROPE
ROPE_THETA = 10000.0
C_HEADS = 4
GRID_W = 64
NA_ROWS_MAX = 8
NA_COLS = 16
A_WIDTH = A_HEADS * HEAD_DIM
A_KV_WIDTH = A_KV_HEADS * HEAD_DIM
B_WIDTH = B_HEADS * B_V
C_WIDTH = C_HEADS * HEAD_DIM
MIX_WIDTH = A_WIDTH + B_WIDTH + C_WIDTH
EPS = 1e-6
NEG_INF = -1e30
LOG2E = 1.4426950408889634

B_QK_PAD = 2 * HEAD_DIM
IN_A_Q = 0
IN_A_KV = IN_A_Q + A_WIDTH
IN_GATES = IN_A_KV + 2 * A_KV_WIDTH
IN_B_LAT = IN_GATES + MIX_WIDTH
IN_C_QKV = IN_B_LAT + B_Q_LORA + B_KV_LORA
IN_B_KR = IN_C_QKV + 3 * C_WIDTH
IN_WIDTH_PAD = IN_B_KR + 2 * B_ROPE

VMEM_LIMIT_BYTES = 56 * 1024 * 1024
IN_PROJ_ROWS = 256
OUT_PLE_ROWS = 512
MLA_QUERY_ROWS = 256
NA_TILE_ROWS = 4
NA_SPAN_ROWS = 12

_BF16 = jnp.bfloat16
_F32 = jnp.float32


def _params(n_grid_dims):
    return pltpu.CompilerParams(
        dimension_semantics=("arbitrary",) * n_grid_dims,
        vmem_limit_bytes=VMEM_LIMIT_BYTES)


def _resident(shape, index_map):
    return pl.BlockSpec(shape, index_map, pipeline_mode=pl.Buffered(1))


def _rms(x, g):
    return x * lax.rsqrt(jnp.mean(x * x, axis=-1, keepdims=True) + EPS) * g


def _dot(a, b):
    return jnp.dot(a, b, preferred_element_type=_F32)


def _dot_nt(a, b):
    return lax.dot_general(a, b, (((1,), (1,)), ((), ())), preferred_element_type=_F32)


def _in_proj_kernel(h_ref, gin_ref, w_ref, gaq_ref, gak_ref, gbq_ref, gbkv_ref, gcq_ref, gck_ref,
                    wq_ref, wkv_ref, gq_ref, gkn_ref, gkr_ref, cos_ref, sin_ref,
                    aq_ref, akv_ref, gates_ref, cqkv_ref, bq_ref, bk_ref, bv_ref):
    u = _rms(h_ref[...], gin_ref[...]).astype(_BF16)
    scale = HEAD_DIM ** -0.5 * LOG2E
    b_scale = B_QK ** -0.5 * LOG2E

    def seg(start, width):
        return _dot(u, w_ref[:, start:start + width])

    def heads(x, g, n, s=None):
        outs = []
        for i in range(n):
            y = _rms(x[:, i * HEAD_DIM:(i + 1) * HEAD_DIM], g)
            outs.append(y if s is None else y * s)
        return outs

    x = seg(IN_B_LAT, B_Q_LORA + B_KV_LORA)
    kr2 = seg(IN_B_KR, 2 * B_ROPE)
    cqn = _rms(x[:, :B_Q_LORA], gbq_ref[...]).astype(_BF16)
    ckvn = _rms(x[:, B_Q_LORA:], gbkv_ref[...]).astype(_BF16)

    chunk = 512
    for c in range(MIX_WIDTH // chunk):
        z = seg(IN_GATES + c * chunk, chunk)
        gates_ref[:, c * chunk:(c + 1) * chunk] = (z * jax.nn.sigmoid(z)).astype(_BF16)

    xq = _dot(cqn, wq_ref[...])
    xkv = _dot(ckvn, wkv_ref[...])

    xa = seg(IN_A_Q, A_WIDTH)
    xakv = seg(IN_A_KV, 2 * A_KV_WIDTH)
    xc = seg(IN_C_QKV, 3 * C_WIDTH)

    cos2 = cos_ref[...]
    sin2 = sin_ref[...]
    rope_lanes = lax.broadcasted_iota(jnp.int32, (1, HEAD_DIM), 1) < B_ROPE

    def rope(c):
        return c * cos2 + pltpu.roll(c, B_ROPE, 1) * sin2

    def sumsq(y, lanes=None):
        y2 = y * y
        return jnp.sum(y2 if lanes is None else jnp.where(lanes, y2, 0.0), axis=-1, keepdims=True)

    ss_kr = sumsq(kr2, rope_lanes)
    k_rope = rope(kr2 * gkr_ref[...])
    for hd in range(B_HEADS):
        lo, mid, hi = hd * B_QK_PAD, hd * B_QK_PAD + HEAD_DIM, (hd + 1) * B_QK_PAD
        a = xq[:, lo:mid]
        c = xq[:, mid:hi]
        r = lax.rsqrt((sumsq(a) + sumsq(c, rope_lanes)) * (1.0 / B_QK) + EPS)
        bq_ref[:, lo:mid] = (a * r * gq_ref[:, :HEAD_DIM] * b_scale).astype(_BF16)
        bq_ref[:, mid:hi] = (rope(c * r * gq_ref[:, HEAD_DIM:]) * b_scale).astype(_BF16)
        kn = xkv[:, lo:mid]
        r = lax.rsqrt((sumsq(kn) + ss_kr) * (1.0 / B_QK) + EPS)
        bk_ref[:, lo:mid] = (kn * r * gkn_ref[...]).astype(_BF16)
        bk_ref[:, mid:hi] = (k_rope * r).astype(_BF16)
        bv_ref[:, hd * B_V:(hd + 1) * B_V] = xkv[:, mid:hi].astype(_BF16)

    for i, y in enumerate(heads(xa, gaq_ref[...], A_HEADS, scale)):
        aq_ref[:, i * HEAD_DIM:(i + 1) * HEAD_DIM] = y.astype(_BF16)
    for i, y in enumerate(heads(xakv[:, :A_KV_WIDTH], gak_ref[...], A_KV_HEADS)):
        akv_ref[:, i * HEAD_DIM:(i + 1) * HEAD_DIM] = y.astype(_BF16)
    akv_ref[:, A_KV_WIDTH:] = xakv[:, A_KV_WIDTH:].astype(_BF16)
    for i, y in enumerate(heads(xc[:, :C_WIDTH], gcq_ref[...], C_HEADS, scale)):
        cqkv_ref[:, i * HEAD_DIM:(i + 1) * HEAD_DIM] = y.astype(_BF16)
    for i, y in enumerate(heads(xc[:, C_WIDTH:2 * C_WIDTH], gck_ref[...], C_HEADS)):
        cqkv_ref[:, C_WIDTH + i * HEAD_DIM:C_WIDTH + (i + 1) * HEAD_DIM] = y.astype(_BF16)
    cqkv_ref[:, 2 * C_WIDTH:] = xc[:, 2 * C_WIDTH:].astype(_BF16)


def _in_proj(h, lw, tables, layer, seq, tm):
    t = h.shape[0]
    row = lambda i: (i, 0)
    pos = lambda i: (i % (seq // tm), 0)
    vec = lambda n: pl.BlockSpec((None, 1, n), lambda i: (layer, 0, 0))
    mat = lambda r, c: _resident((None, r, c), lambda i: (layer, 0, 0))
    widths = (A_WIDTH, 2 * A_KV_WIDTH, MIX_WIDTH, 3 * C_WIDTH, B_HEADS * B_QK_PAD, B_HEADS * B_QK_PAD,
              B_WIDTH)
    return pl.pallas_call(
        _in_proj_kernel,
        grid=(t // tm,),
        in_specs=[
            pl.BlockSpec((tm, D_MODEL), row),
            vec(D_MODEL),
            mat(D_MODEL, IN_WIDTH_PAD),
            vec(HEAD_DIM), vec(HEAD_DIM), vec(B_Q_LORA), vec(B_KV_LORA), vec(HEAD_DIM), vec(HEAD_DIM),
            mat(B_Q_LORA, B_HEADS * B_QK_PAD), mat(B_KV_LORA, B_HEADS * (B_NOPE + B_V)),
            vec(B_QK_PAD), vec(HEAD_DIM), vec(2 * B_ROPE),
            pl.BlockSpec((tm, 2 * B_ROPE), pos),
            pl.BlockSpec((tm, 2 * B_ROPE), pos),
        ],
        out_specs=[pl.BlockSpec((tm, w), row) for w in widths],
        out_shape=[jax.ShapeDtypeStruct((t, w), _BF16) for w in widths],
        compiler_params=_params(1),
        name="in_proj",
    )(h, lw['norm_in'], lw['w_in'], lw['a_q_norm'], lw['a_k_norm'], lw['b_cq_norm'], lw['b_ckv_norm'],
      lw['c_q_norm'], lw['c_k_norm'], lw['b_w_uq'], lw['b_w_ukv'], lw['b_q_gain'], lw['b_k_gain_nope'],
      lw['b_k_gain_rope'], tables['cos2'], tables['sin2'])


def _fill_v_ext(vext_ref, v_ref):
    width = v_ref.shape[1]
    vext_ref[:, :width] = v_ref[...]
    vext_ref[:, width:] = jnp.ones((v_ref.shape[0], vext_ref.shape[1] - width), vext_ref.dtype)


def _mla_attn_kernel(q_ref, k_ref, v_ref, g_ref, o_ref, vext_ref, *, tq):
    n_chunks = q_ref.shape[0] // tq
    _fill_v_ext(vext_ref, v_ref)

    def logits(i):
        return _dot_nt(q_ref[i * tq:(i + 1) * tq, :], k_ref[...])

    s_next = logits(0)
    for i in range(n_chunks):
        s = s_next
        if i + 1 < n_chunks:
            s_next = logits(i + 1)
        e = jnp.exp2(s - jnp.max(s, axis=-1, keepdims=True)).astype(_BF16)
        acc = _dot(e, vext_ref[...])
        rows = slice(i * tq, (i + 1) * tq)
        o = acc[:, :B_V] / acc[:, B_V:]
        o_ref[rows, :] = (o * g_ref[rows, :].astype(_F32)).astype(_BF16)


def _mla_attn(q, k, v, gates, batch, seq, tq):
    t = q.shape[0]
    gate_blk = A_WIDTH // B_V
    return pl.pallas_call(
        functools.partial(_mla_attn_kernel, tq=tq),
        grid=(batch, B_HEADS),
        in_specs=[
            pl.BlockSpec((seq, B_QK_PAD), lambda b, hd: (b, hd)),
            pl.BlockSpec((seq, B_QK_PAD), lambda b, hd: (b, hd)),
            pl.BlockSpec((seq, B_V), lambda b, hd: (b, hd)),
            pl.BlockSpec((seq, B_V), lambda b, hd: (b, gate_blk + hd)),
        ],
        out_specs=pl.BlockSpec((seq, B_V), lambda b, hd: (b, hd)),
        out_shape=jax.ShapeDtypeStruct((t, B_WIDTH), _BF16),
        scratch_shapes=[pltpu.VMEM((seq, 2 * B_V), _BF16)],
        compiler_params=_params(2),
        name="mla_attn",
    )(q, k, v, gates)


def _win_attn_kernel(sink_ref, slope_ref, q_ref, k_ref, v_ref, g_ref, o_ref, bias_ref, sinkcol_ref,
                     vext_ref):
    seq = q_ref.shape[0]
    nb = seq // A_BLOCK
    span = 3 * A_BLOCK
    kvh = pl.program_id(1)
    _fill_v_ext(vext_ref, v_ref)

    qi = lax.broadcasted_iota(jnp.int32, (A_BLOCK, span), 0)
    si = lax.broadcasted_iota(jnp.int32, (A_BLOCK, span), 1)
    for variant in range(3):
        dist = jnp.abs(si - qi - variant * A_BLOCK)
        valid = dist <= A_WINDOW
        for g in range(A_GROUP):
            slope = slope_ref[kvh * A_GROUP + g] * LOG2E
            bias_ref[variant, g * A_BLOCK:(g + 1) * A_BLOCK, :] = jnp.where(
                valid, -(slope * dist.astype(_F32)), NEG_INF)
    for g in range(A_GROUP):
        sinkcol_ref[g * A_BLOCK:(g + 1) * A_BLOCK, :] = jnp.full(
            (A_BLOCK, 1), sink_ref[kvh * A_GROUP + g] * LOG2E, _F32)

    def key_start(j):
        return min(max((j - 1) * A_BLOCK, 0), seq - span)

    def logits(j):
        qrows = slice(j * A_BLOCK, (j + 1) * A_BLOCK)
        qs = jnp.concatenate(
            [q_ref[qrows, g * HEAD_DIM:(g + 1) * HEAD_DIM] for g in range(A_GROUP)], axis=0)
        return _dot_nt(qs, k_ref[key_start(j):key_start(j) + span, :])

    s_next = logits(0)
    for j in range(nb):
        start = key_start(j)
        s = s_next + bias_ref[(j * A_BLOCK - start) // A_BLOCK]
        if j + 1 < nb:
            s_next = logits(j + 1)
        sink = sinkcol_ref[...]
        m = jnp.maximum(jnp.max(s, axis=-1, keepdims=True), sink)
        e = jnp.exp2(s - m).astype(_BF16)
        acc = _dot(e, vext_ref[start:start + span, :])
        o = acc[:, :HEAD_DIM] / (acc[:, HEAD_DIM:] + jnp.exp2(sink - m))
        qrows = slice(j * A_BLOCK, (j + 1) * A_BLOCK)
        for g in range(A_GROUP):
            cols = slice(g * HEAD_DIM, (g + 1) * HEAD_DIM)
            o_ref[qrows, cols] = (
                o[g * A_BLOCK:(g + 1) * A_BLOCK] * g_ref[qrows, cols].astype(_F32)).astype(_BF16)


def _win_attn(aq, akv, gates, sink, batch, seq):
    t = aq.shape[0]
    gw = A_GROUP * HEAD_DIM
    assert seq >= 3 * A_BLOCK
    slopes = jnp.exp2(-8.0 * jnp.arange(1, A_HEADS + 1, dtype=_F32) / A_HEADS)
    return pl.pallas_call(
        _win_attn_kernel,
        grid=(batch, A_KV_HEADS),
        in_specs=[
            pl.BlockSpec(memory_space=pltpu.SMEM),
            pl.BlockSpec(memory_space=pltpu.SMEM),
            pl.BlockSpec((seq, gw), lambda b, kh: (b, kh)),
            pl.BlockSpec((seq, HEAD_DIM), lambda b, kh: (b, kh)),
            pl.BlockSpec((seq, HEAD_DIM), lambda b, kh: (b, A_KV_HEADS + kh)),
            pl.BlockSpec((seq, gw), lambda b, kh: (b, kh)),
        ],
        out_specs=pl.BlockSpec((seq, gw), lambda b, kh: (b, kh)),
        out_shape=jax.ShapeDtypeStruct((t, A_WIDTH), _BF16),
        scratch_shapes=[
            pltpu.VMEM((3, A_GROUP * A_BLOCK, 3 * A_BLOCK), _F32),
            pltpu.VMEM((A_GROUP * A_BLOCK, 1), _F32),
            pltpu.VMEM((seq, 2 * HEAD_DIM), _BF16),
        ],
        compiler_params=_params(2),
        name="win_attn",
    )(sink, slopes, aq, akv, akv, gates)


def _na_tiles(rows):
    wr = min(NA_ROWS_MAX, rows)
    assert rows % NA_TILE_ROWS == 0 and rows >= NA_SPAN_ROWS
    starts, tile_pattern, patterns = [], [], []
    for t in range(rows // NA_TILE_ROWS):
        ws = min(max(NA_TILE_ROWS * t - wr // 2, 0), rows - NA_SPAN_ROWS)
        pattern = []
        for i in range(NA_TILE_ROWS):
            r = NA_TILE_ROWS * t + i
            rs = min(max(r - wr // 2, 0), rows - wr)
            assert ws <= rs and rs + wr <= ws + NA_SPAN_ROWS
            pattern.append((rs - ws, r - ws))
        pattern = tuple(pattern)
        if pattern not in patterns:
            patterns.append(pattern)
        starts.append(ws)
        tile_pattern.append(patterns.index(pattern))
    return starts, tile_pattern, patterns


def _na_attn_kernel(q_ref, k_ref, v_ref, g_ref, bias_ref, o_ref, vext_ref):
    seq = q_ref.shape[0]
    starts, tile_pattern, _ = _na_tiles(seq // GRID_W)
    n_tiles = len(starts)
    tq = NA_TILE_ROWS * GRID_W
    span = NA_SPAN_ROWS * GRID_W
    _fill_v_ext(vext_ref, v_ref)

    def logits(t):
        return _dot_nt(q_ref[t * tq:(t + 1) * tq, :], k_ref[starts[t] * GRID_W:starts[t] * GRID_W + span, :])

    s_next = logits(0)
    for t in range(n_tiles):
        s = s_next + bias_ref[tile_pattern[t]]
        if t + 1 < n_tiles:
            s_next = logits(t + 1)
        e = jnp.exp2(s - jnp.max(s, axis=-1, keepdims=True)).astype(_BF16)
        acc = _dot(e, vext_ref[starts[t] * GRID_W:starts[t] * GRID_W + span, :])
        rows = slice(t * tq, (t + 1) * tq)
        o = acc[:, :HEAD_DIM] / acc[:, HEAD_DIM:]
        o_ref[rows, :] = (o * g_ref[rows, :].astype(_F32)).astype(_BF16)


def _na_attn(cqkv, gates, bias, batch, seq, layer):
    t = cqkv.shape[0]
    gate_blk = (A_WIDTH + B_WIDTH) // HEAD_DIM
    blk = lambda off: pl.BlockSpec((seq, HEAD_DIM), lambda hd, b: (b, off + hd))
    return pl.pallas_call(
        _na_attn_kernel,
        grid=(C_HEADS, batch),
        in_specs=[
            blk(0), blk(C_HEADS), blk(2 * C_HEADS),
            pl.BlockSpec((seq, HEAD_DIM), lambda hd, b: (b, gate_blk + hd)),
            pl.BlockSpec((None, None) + bias.shape[2:], lambda hd, b: (layer, hd, 0, 0, 0)),
        ],
        out_specs=pl.BlockSpec((seq, HEAD_DIM), lambda hd, b: (b, hd)),
        out_shape=jax.ShapeDtypeStruct((t, C_WIDTH), _BF16),
        scratch_shapes=[pltpu.VMEM((seq, 2 * HEAD_DIM), _BF16)],
        compiler_params=_params(2),
        name="na_attn",
    )(cqkv, cqkv, cqkv, gates, bias)


def _out_ple_kernel(ya_ref, yb_ref, yc_ref, h_ref, p_ref, wo_ref, gple_ref, wg_ref, wp_ref, gpost_ref,
                    o_ref):
    tm = h_ref.shape[0]
    halves = [slice(i * (tm // 2), (i + 1) * (tm // 2)) for i in range(2)]
    pp = [_rms(_dot(p_ref[rows, :].astype(_BF16), wp_ref[...]), gpost_ref[...]) for rows in halves]
    h1 = []
    for rows in halves:
        acc = _dot(ya_ref[rows, :], wo_ref[:A_WIDTH, :])
        acc += _dot(yb_ref[rows, :], wo_ref[A_WIDTH:A_WIDTH + B_WIDTH, :])
        acc += _dot(yc_ref[rows, :], wo_ref[A_WIDTH + B_WIDTH:, :])
        h1.append(h_ref[rows, :] + acc)
    logit = [_dot(_rms(x, gple_ref[...]).astype(_BF16), wg_ref[...]) for x in h1]
    for i, rows in enumerate(halves):
        o_ref[rows, :] = h1[i] + jax.nn.sigmoid(logit[i]) * pp[i]


def _out_ple(ya, yb, yc, h, p, lw, layer, tm):
    t = h.shape[0]
    row = lambda i: (i, 0)
    vec = lambda n: pl.BlockSpec((None, 1, n), lambda i: (layer, 0, 0))
    mat = lambda r, c: _resident((None, r, c), lambda i: (layer, 0, 0))
    return pl.pallas_call(
        _out_ple_kernel,
        grid=(t // tm,),
        in_specs=[
            pl.BlockSpec((tm, A_WIDTH), row),
            pl.BlockSpec((tm, B_WIDTH), row),
            pl.BlockSpec((tm, C_WIDTH), row),
            pl.BlockSpec((tm, D_MODEL), row),
            pl.BlockSpec((None, tm, PLE_DIM), lambda i: (layer, i, 0)),
            mat(MIX_WIDTH, D_MODEL), vec(D_MODEL), mat(D_MODEL, D_MODEL), mat(PLE_DIM, D_MODEL),
            vec(D_MODEL),
        ],
        out_specs=pl.BlockSpec((tm, D_MODEL), row),
        out_shape=jax.ShapeDtypeStruct((t, D_MODEL), _F32),
        compiler_params=_params(1),
        name="out_ple",
    )(ya, yb, yc, h, p, lw['w_out'], lw['ple_norm'], lw['w_ple_gate'], lw['w_ple_proj'],
      lw['ple_post_norm'])


def _swap_halves(x):
    half = x.shape[-1] // 2
    return jnp.concatenate([x[..., half:], x[..., :half]], axis=-1)


def _prepare_weights(W):
    depth = W['w_in'].shape[0]
    sizes = (A_WIDTH, A_KV_WIDTH, A_KV_WIDTH, A_WIDTH, B_Q_LORA, B_KV_LORA, B_ROPE, B_WIDTH,
             C_WIDTH, C_WIDTH, C_WIDTH, C_WIDTH)
    offs = np.concatenate([[0], np.cumsum(sizes)])
    w_in_bf16 = W['w_in'].astype(_BF16)
    aq, ak, av, az, bcq, bckv, bkr, bz, cq, ck, cv, cz = [
        w_in_bf16[:, :, offs[i]:offs[i + 1]] for i in range(len(sizes))]
    w_in = jnp.concatenate(
        [aq, ak, av, az, bz, cz, bcq, bckv, cq, ck, cv, bkr, _swap_halves(bkr)], axis=-1)

    wq = W['b_w_uq'].reshape(depth, B_Q_LORA, B_HEADS, B_QK)
    wq = jnp.concatenate([wq, _swap_halves(wq[..., B_NOPE:])], axis=-1)
    wq = wq.reshape(depth, B_Q_LORA, B_HEADS * B_QK_PAD).astype(_BF16)
    gq = W['b_q_norm']
    gk = W['b_k_norm']
    row = lambda g: g[:, None, :].astype(_F32)
    return {
        'norm_in': row(W['norm_in']), 'w_in': w_in,
        'a_q_norm': row(W['a_q_norm']), 'a_k_norm': row(W['a_k_norm']), 'a_sink': W['a_sink'].astype(_F32),
        'b_cq_norm': row(W['b_cq_norm']), 'b_ckv_norm': row(W['b_ckv_norm']),
        'b_w_uq': wq, 'b_w_ukv': W['b_w_ukv'].astype(_BF16),
        'b_q_gain': row(jnp.concatenate([gq, _swap_halves(gq[:, B_NOPE:])], axis=-1)),
        'b_k_gain_nope': row(gk[:, :B_NOPE]),
        'b_k_gain_rope': row(jnp.concatenate([gk[:, B_NOPE:], _swap_halves(gk[:, B_NOPE:])], axis=-1)),
        'c_q_norm': row(W['c_q_norm']), 'c_k_norm': row(W['c_k_norm']),
        'w_out': W['w_out'].astype(_BF16), 'ple_norm': row(W['ple_norm']),
        'w_ple_gate': W['w_ple_gate'].astype(_BF16), 'w_ple_proj': W['w_ple_proj'].astype(_BF16),
        'ple_post_norm': row(W['ple_post_norm']),
    }


def _rope_tables(seq):
    half = B_ROPE // 2
    inv = ROPE_THETA ** (-jnp.arange(half, dtype=_F32) / half)
    ang = jnp.arange(seq).astype(_F32)[:, None] * inv[None, :]
    cos, sin = jnp.cos(ang), jnp.sin(ang)
    zeros = jnp.zeros((seq, B_ROPE), _F32)
    return {'cos2': jnp.concatenate([cos, cos, zeros], axis=-1),
            'sin2': jnp.concatenate([-sin, sin, zeros], axis=-1)}


def _na_bias_table(rpb, seq):
    rows = seq // GRID_W
    wr = min(NA_ROWS_MAX, rows)
    depth = rpb.shape[0]
    c = np.arange(GRID_W)
    cs = np.clip(c - NA_COLS // 2, 0, GRID_W - NA_COLS)
    col_valid = (c[None, :] >= cs[:, None]) & (c[None, :] < cs[:, None] + NA_COLS)
    dc = np.clip(c[None, :] - c[:, None] + NA_COLS - 1, 0, 2 * NA_COLS - 2)
    by_dc = jnp.where(col_valid, rpb.astype(_F32)[:, :, :, dc] * LOG2E, NEG_INF)
    _, _, patterns = _na_tiles(rows)
    top = NA_ROWS_MAX - 1

    def masked(n):
        return jnp.full((depth, C_HEADS, n, GRID_W, GRID_W), NEG_INF, _F32)

    slabs = []
    for pattern in patterns:
        per_row = []
        for first, q_off in pattern:
            lo = first - q_off + top
            per_row.append(jnp.concatenate(
                [masked(first), by_dc[:, :, lo:lo + wr], masked(NA_SPAN_ROWS - first - wr)], axis=2))
        slab = jnp.stack(per_row, axis=2)
        slabs.append(slab.transpose(0, 1, 2, 4, 3, 5).reshape(
            depth, C_HEADS, NA_TILE_ROWS * GRID_W, NA_SPAN_ROWS * GRID_W))
    return jnp.stack(slabs, axis=2)


def _trunk(x, p, lw, tables, na_bias):
    batch, seq, _ = x.shape
    depth = p.shape[0]
    t = batch * seq
    h = x.reshape(t, D_MODEL)
    p = p.reshape(depth, t, PLE_DIM)
    for layer in range(depth):
        aq, akv, gates, cqkv, bq, bk, bv = _in_proj(h, lw, tables, layer, seq, IN_PROJ_ROWS)
        ya = _win_attn(aq, akv, gates, lw['a_sink'][layer], batch, seq)
        yb = _mla_attn(bq, bk, bv, gates, batch, seq, MLA_QUERY_ROWS)
        yc = _na_attn(cqkv, gates, na_bias, batch, seq, layer)
        h = _out_ple(ya, yb, yc, h, p, lw, layer, OUT_PLE_ROWS)
    return h.reshape(batch, seq, D_MODEL)


def kernel(x_prompt, x_sample, p_prompt, p_sample, norm_in, w_in, a_q_norm, a_k_norm, a_sink, b_cq_norm, b_ckv_norm, b_w_uq, b_w_ukv, b_q_norm, b_k_norm, c_q_norm, c_k_norm, c_rpb, w_out, ple_norm, w_ple_gate, w_ple_proj, ple_post_norm):
    lw = _prepare_weights({
        'norm_in': norm_in, 'w_in': w_in, 'a_q_norm': a_q_norm, 'a_k_norm': a_k_norm, 'a_sink': a_sink,
        'b_cq_norm': b_cq_norm, 'b_ckv_norm': b_ckv_norm, 'b_w_uq': b_w_uq, 'b_w_ukv': b_w_ukv,
        'b_q_norm': b_q_norm, 'b_k_norm': b_k_norm, 'c_q_norm': c_q_norm, 'c_k_norm': c_k_norm,
        'w_out': w_out, 'ple_norm': ple_norm, 'w_ple_gate': w_ple_gate, 'w_ple_proj': w_ple_proj,
        'ple_post_norm': ple_post_norm,
    })
    tables = {}
    outs = []
    for x, p in ((x_prompt, p_prompt), (x_sample, p_sample)):
        seq = x.shape[1]
        if seq not in tables:
            tables[seq] = (_rope_tables(seq), _na_bias_table(c_rpb, seq))
        outs.append(_trunk(x, p, lw, *tables[seq]))
    return tuple(outs)
```

```python
import functools

import jax
import jax.numpy as jnp
import numpy as np
from jax import lax
from jax.experimental import pallas as pl
from jax.experimental.pallas import tpu as pltpu

D_MODEL = 2048
PLE_DIM = 256
HEAD_DIM = 128
A_HEADS = 6
A_KV_HEADS = 2
A_GROUP = A_HEADS // A_KV_HEADS
A_WINDOW = 128
A_BLOCK = 128
B_HEADS = 6
B_Q_LORA = 512
B_KV_LORA = 512
B_NOPE = 128
B_ROPE = 64
B_V = 128
B_QK = B_NOPE + B_ROPE
ROPE_THETA = 10000.0
C_HEADS = 4
GRID_W = 64
NA_ROWS_MAX = 8
NA_COLS = 16
A_WIDTH = A_HEADS * HEAD_DIM
A_KV_WIDTH = A_KV_HEADS * HEAD_DIM
B_WIDTH = B_HEADS * B_V
C_WIDTH = C_HEADS * HEAD_DIM
MIX_WIDTH = A_WIDTH + B_WIDTH + C_WIDTH
EPS = 1e-6
NEG_INF = -1e30
LOG2E = 1.4426950408889634

B_QK_PAD = 2 * HEAD_DIM
IN_LO_WIDTH = 2 * A_WIDTH + 2 * A_KV_WIDTH + B_Q_LORA + B_KV_LORA
IN_HI_START = IN_LO_WIDTH + B_ROPE
IN_HI_WIDTH = B_WIDTH + 4 * C_WIDTH
LO_A_Q = 0
LO_A_KV = LO_A_Q + A_WIDTH
LO_A_Z = LO_A_KV + 2 * A_KV_WIDTH
LO_B_LAT = LO_A_Z + A_WIDTH
HI_B_Z = 0
HI_C_QKV = HI_B_Z + B_WIDTH
HI_C_Z = HI_C_QKV + 3 * C_WIDTH

VMEM_LIMIT_BYTES = 56 * 1024 * 1024
IN_PROJ_ROWS = 256
OUT_PLE_ROWS = 512
MLA_QUERY_ROWS = 256
NA_TILE_ROWS = 4
NA_SPAN_ROWS = 12

_BF16 = jnp.bfloat16
_F32 = jnp.float32


def _params(n_grid_dims):
    return pltpu.CompilerParams(
        dimension_semantics=("arbitrary",) * n_grid_dims,
        vmem_limit_bytes=VMEM_LIMIT_BYTES)


def _resident(shape, index_map):
    return pl.BlockSpec(shape, index_map, pipeline_mode=pl.Buffered(1))


def _rms(x, g):
    return x * lax.rsqrt(jnp.mean(x * x, axis=-1, keepdims=True) + EPS) * g


def _dot(a, b):
    return jnp.dot(a, b, preferred_element_type=_F32)


def _dot_nt(a, b):
    return lax.dot_general(a, b, (((1,), (1,)), ((), ())), preferred_element_type=_F32)


def _in_proj_kernel(h_ref, gin_ref, wlo_ref, whi_ref, wkr_ref, gaq_ref, gak_ref, gbq_ref, gbkv_ref,
                    gcq_ref, gck_ref,
                    wq_ref, wkv_ref, gq_ref, gkn_ref, gkr_ref, cos_ref, sin_ref,
                    aq_ref, akv_ref, gates_ref, cqkv_ref, bq_ref, bk_ref, bv_ref):
    u = _rms(h_ref[...], gin_ref[...]).astype(_BF16)
    scale = HEAD_DIM ** -0.5 * LOG2E
    b_scale = B_QK ** -0.5 * LOG2E

    def seg(w_ref, start, width):
        return _dot(u, w_ref[:, start:start + width])

    def heads(x, g, n, s=None):
        outs = []
        for i in range(n):
            y = _rms(x[:, i * HEAD_DIM:(i + 1) * HEAD_DIM], g)
            outs.append(y if s is None else y * s)
        return outs

    x = seg(wlo_ref, LO_B_LAT, B_Q_LORA + B_KV_LORA)
    kr2 = _dot(u, wkr_ref[...])
    cqn = _rms(x[:, :B_Q_LORA], gbq_ref[...]).astype(_BF16)
    ckvn = _rms(x[:, B_Q_LORA:], gbkv_ref[...]).astype(_BF16)

    out_col = 0
    for w_ref, start, width in ((wlo_ref, LO_A_Z, A_WIDTH), (whi_ref, HI_B_Z, B_WIDTH),
                                (whi_ref, HI_C_Z, C_WIDTH)):
        z = seg(w_ref, start, width)
        gates_ref[:, out_col:out_col + width] = (z * jax.nn.sigmoid(z)).astype(_BF16)
        out_col += width

    xq = _dot(cqn, wq_ref[...])
    xkv = _dot(ckvn, wkv_ref[...])

    xa = seg(wlo_ref, LO_A_Q, A_WIDTH)
    xakv = seg(wlo_ref, LO_A_KV, 2 * A_KV_WIDTH)
    xc = seg(whi_ref, HI_C_QKV, 3 * C_WIDTH)

    cos2 = cos_ref[...]
    sin2 = sin_ref[...]
    rope_lanes = lax.broadcasted_iota(jnp.int32, (1, HEAD_DIM), 1) < B_ROPE

    def rope(c):
        return c * cos2 + pltpu.roll(c, B_ROPE, 1) * sin2

    def sumsq(y, lanes=None):
        y2 = y * y
        return jnp.sum(y2 if lanes is None else jnp.where(lanes, y2, 0.0), axis=-1, keepdims=True)

    ss_kr = sumsq(kr2, rope_lanes)
    k_rope = rope(kr2 * gkr_ref[...])
    for hd in range(B_HEADS):
        lo, mid, hi = hd * B_QK_PAD, hd * B_QK_PAD + HEAD_DIM, (hd + 1) * B_QK_PAD
        a = xq[:, lo:mid]
        c = xq[:, mid:hi]
        r = lax.rsqrt((sumsq(a) + sumsq(c, rope_lanes)) * (1.0 / B_QK) + EPS)
        bq_ref[:, lo:mid] = (a * r * gq_ref[:, :HEAD_DIM] * b_scale).astype(_BF16)
        bq_ref[:, mid:hi] = (rope(c * r * gq_ref[:, HEAD_DIM:]) * b_scale).astype(_BF16)
        kn = xkv[:, lo:mid]
        r = lax.rsqrt((sumsq(kn) + ss_kr) * (1.0 / B_QK) + EPS)
        bk_ref[:, lo:mid] = (kn * r * gkn_ref[...]).astype(_BF16)
        bk_ref[:, mid:hi] = (k_rope * r).astype(_BF16)
        bv_ref[:, hd * B_V:(hd + 1) * B_V] = xkv[:, mid:hi].astype(_BF16)

    for i, y in enumerate(heads(xa, gaq_ref[...], A_HEADS, scale)):
        aq_ref[:, i * HEAD_DIM:(i + 1) * HEAD_DIM] = y.astype(_BF16)
    for i, y in enumerate(heads(xakv[:, :A_KV_WIDTH], gak_ref[...], A_KV_HEADS)):
        akv_ref[:, i * HEAD_DIM:(i + 1) * HEAD_DIM] = y.astype(_BF16)
    akv_ref[:, A_KV_WIDTH:] = xakv[:, A_KV_WIDTH:].astype(_BF16)
    for i, y in enumerate(heads(xc[:, :C_WIDTH], gcq_ref[...], C_HEADS, scale)):
        cqkv_ref[:, i * HEAD_DIM:(i + 1) * HEAD_DIM] = y.astype(_BF16)
    for i, y in enumerate(heads(xc[:, C_WIDTH:2 * C_WIDTH], gck_ref[...], C_HEADS)):
        cqkv_ref[:, C_WIDTH + i * HEAD_DIM:C_WIDTH + (i + 1) * HEAD_DIM] = y.astype(_BF16)
    cqkv_ref[:, 2 * C_WIDTH:] = xc[:, 2 * C_WIDTH:].astype(_BF16)


def _in_proj(h, lw, tables, layer, seq, tm):
    t = h.shape[0]
    row = lambda i: (i, 0)
    pos = lambda i: (i % (seq // tm), 0)
    vec = lambda n: pl.BlockSpec((None, 1, n), lambda i: (layer, 0, 0))
    mat = lambda r, c: _resident((None, r, c), lambda i: (layer, 0, 0))
    widths = (A_WIDTH, 2 * A_KV_WIDTH, MIX_WIDTH, 3 * C_WIDTH, B_HEADS * B_QK_PAD, B_HEADS * B_QK_PAD,
              B_WIDTH)
    return pl.pallas_call(
        _in_proj_kernel,
        grid=(t // tm,),
        in_specs=[
            pl.BlockSpec((tm, D_MODEL), row),
            vec(D_MODEL),
            mat(D_MODEL, IN_LO_WIDTH), mat(D_MODEL, IN_HI_WIDTH), mat(D_MODEL, 2 * B_ROPE),
            vec(HEAD_DIM), vec(HEAD_DIM), vec(B_Q_LORA), vec(B_KV_LORA), vec(HEAD_DIM), vec(HEAD_DIM),
            mat(B_Q_LORA, B_HEADS * B_QK_PAD), mat(B_KV_LORA, B_HEADS * (B_NOPE + B_V)),
            vec(B_QK_PAD), vec(HEAD_DIM), vec(2 * B_ROPE),
            pl.BlockSpec((tm, 2 * B_ROPE), pos),
            pl.BlockSpec((tm, 2 * B_ROPE), pos),
        ],
        out_specs=[pl.BlockSpec((tm, w), row) for w in widths],
        out_shape=[jax.ShapeDtypeStruct((t, w), _BF16) for w in widths],
        compiler_params=_params(1),
        name="in_proj",
    )(h, lw['norm_in'], lw['w_in_lo'], lw['w_in_hi'], lw['w_in_kr'], lw['a_q_norm'], lw['a_k_norm'],
      lw['b_cq_norm'], lw['b_ckv_norm'], lw['c_q_norm'], lw['c_k_norm'], lw['b_w_uq'], lw['b_w_ukv'], lw['b_q_gain'], lw['b_k_gain_nope'],
      lw['b_k_gain_rope'], tables['cos2'], tables['sin2'])


def _fill_v_ext(vext_ref, v_ref):
    width = v_ref.shape[1]
    vext_ref[:, :width] = v_ref[...]
    vext_ref[:, width:] = jnp.ones((v_ref.shape[0], vext_ref.shape[1] - width), vext_ref.dtype)


def _pipelined(n, logits, finish, depth):
    queue = [logits(i) for i in range(min(depth, n))]
    for i in range(n):
        s = queue.pop(0)
        if i + depth < n:
            queue.append(logits(i + depth))
        finish(i, s)


def _mla_attn_kernel(q_ref, k_ref, v_ref, g_ref, o_ref, vext_ref, *, tq):
    n_chunks = q_ref.shape[0] // tq
    _fill_v_ext(vext_ref, v_ref)

    def logits(i):
        return _dot_nt(q_ref[i * tq:(i + 1) * tq, :], k_ref[...])

    def finish(i, s):
        e = jnp.exp2(s - jnp.max(s, axis=-1, keepdims=True)).astype(_BF16)
        acc = _dot(e, vext_ref[...])
        rows = slice(i * tq, (i + 1) * tq)
        o = acc[:, :B_V] / acc[:, B_V:]
        o_ref[rows, :] = (o * g_ref[rows, :].astype(_F32)).astype(_BF16)

    _pipelined(n_chunks, logits, finish, depth=1)


def _mla_attn(q, k, v, gates, batch, seq, tq):
    t = q.shape[0]
    gate_blk = A_WIDTH // B_V
    return pl.pallas_call(
        functools.partial(_mla_attn_kernel, tq=tq),
        grid=(batch, B_HEADS),
        in_specs=[
            pl.BlockSpec((seq, B_QK_PAD), lambda b, hd: (b, hd)),
            pl.BlockSpec((seq, B_QK_PAD), lambda b, hd: (b, hd)),
            pl.BlockSpec((seq, B_V), lambda b, hd: (b, hd)),
            pl.BlockSpec((seq, B_V), lambda b, hd: (b, gate_blk + hd)),
        ],
        out_specs=pl.BlockSpec((seq, B_V), lambda b, hd: (b, hd)),
        out_shape=jax.ShapeDtypeStruct((t, B_WIDTH), _BF16),
        scratch_shapes=[pltpu.VMEM((seq, 2 * B_V), _BF16)],
        compiler_params=_params(2),
        name="mla_attn",
    )(q, k, v, gates)


def _win_attn_kernel(sink_ref, slope_ref, q_ref, k_ref, v_ref, g_ref, o_ref, bias_ref, vext_ref):
    seq = q_ref.shape[0]
    nb = seq // A_BLOCK
    span = 3 * A_BLOCK
    kvh = pl.program_id(1)
    _fill_v_ext(vext_ref, v_ref)

    qi = lax.broadcasted_iota(jnp.int32, (A_BLOCK, span), 0)
    si = lax.broadcasted_iota(jnp.int32, (A_BLOCK, span), 1)
    for variant in range(3):
        dist = jnp.abs(si - qi - variant * A_BLOCK)
        valid = dist <= A_WINDOW
        for g in range(A_GROUP):
            slope = slope_ref[kvh * A_GROUP + g] * LOG2E
            bias_ref[variant, g * A_BLOCK:(g + 1) * A_BLOCK, :] = jnp.where(
                valid, -(slope * dist.astype(_F32)), NEG_INF)
    sinks = [sink_ref[kvh * A_GROUP + g] * LOG2E for g in range(A_GROUP)]

    def key_start(j):
        return min(max((j - 1) * A_BLOCK, 0), seq - span)

    def logits(j):
        qrows = slice(j * A_BLOCK, (j + 1) * A_BLOCK)
        qs = jnp.concatenate(
            [q_ref[qrows, g * HEAD_DIM:(g + 1) * HEAD_DIM] for g in range(A_GROUP)], axis=0)
        return _dot_nt(qs, k_ref[key_start(j):key_start(j) + span, :])

    def finish(j, s):
        start = key_start(j)
        s = s + bias_ref[(j * A_BLOCK - start) // A_BLOCK]
        probs, sink_terms = [], []
        for g in range(A_GROUP):
            sg = s[g * A_BLOCK:(g + 1) * A_BLOCK]
            m = jnp.maximum(jnp.max(sg, axis=-1, keepdims=True), sinks[g])
            probs.append(jnp.exp2(sg - m).astype(_BF16))
            sink_terms.append(jnp.exp2(sinks[g] - m))
        acc = _dot(jnp.concatenate(probs, axis=0), vext_ref[start:start + span, :])
        qrows = slice(j * A_BLOCK, (j + 1) * A_BLOCK)
        for g in range(A_GROUP):
            ag = acc[g * A_BLOCK:(g + 1) * A_BLOCK]
            o = ag[:, :HEAD_DIM] / (ag[:, HEAD_DIM:] + sink_terms[g])
            cols = slice(g * HEAD_DIM, (g + 1) * HEAD_DIM)
            o_ref[qrows, cols] = (o * g_ref[qrows, cols].astype(_F32)).astype(_BF16)

    _pipelined(nb, logits, finish, depth=2)


def _win_attn(aq, akv, gates, sink, batch, seq):
    t = aq.shape[0]
    gw = A_GROUP * HEAD_DIM
    assert seq >= 3 * A_BLOCK
    slopes = jnp.exp2(-8.0 * jnp.arange(1, A_HEADS + 1, dtype=_F32) / A_HEADS)
    return pl.pallas_call(
        _win_attn_kernel,
        grid=(batch, A_KV_HEADS),
        in_specs=[
            pl.BlockSpec(memory_space=pltpu.SMEM),
            pl.BlockSpec(memory_space=pltpu.SMEM),
            pl.BlockSpec((seq, gw), lambda b, kh: (b, kh)),
            pl.BlockSpec((seq, HEAD_DIM), lambda b, kh: (b, kh)),
            pl.BlockSpec((seq, HEAD_DIM), lambda b, kh: (b, A_KV_HEADS + kh)),
            pl.BlockSpec((seq, gw), lambda b, kh: (b, kh)),
        ],
        out_specs=pl.BlockSpec((seq, gw), lambda b, kh: (b, kh)),
        out_shape=jax.ShapeDtypeStruct((t, A_WIDTH), _BF16),
        scratch_shapes=[
            pltpu.VMEM((3, A_GROUP * A_BLOCK, 3 * A_BLOCK), _F32),
            pltpu.VMEM((seq, 2 * HEAD_DIM), _BF16),
        ],
        compiler_params=_params(2),
        name="win_attn",
    )(sink, slopes, aq, akv, akv, gates)


def _na_tiles(rows):
    wr = min(NA_ROWS_MAX, rows)
    assert rows % NA_TILE_ROWS == 0 and rows >= NA_SPAN_ROWS
    starts, tile_pattern, patterns = [], [], []
    for t in range(rows // NA_TILE_ROWS):
        ws = min(max(NA_TILE_ROWS * t - wr // 2, 0), rows - NA_SPAN_ROWS)
        pattern = []
        for i in range(NA_TILE_ROWS):
            r = NA_TILE_ROWS * t + i
            rs = min(max(r - wr // 2, 0), rows - wr)
            assert ws <= rs and rs + wr <= ws + NA_SPAN_ROWS
            pattern.append((rs - ws, r - ws))
        pattern = tuple(pattern)
        if pattern not in patterns:
            patterns.append(pattern)
        starts.append(ws)
        tile_pattern.append(patterns.index(pattern))
    return starts, tile_pattern, patterns


def _na_attn_kernel(q_ref, k_ref, v_ref, g_ref, bias_ref, o_ref, vext_ref):
    seq = q_ref.shape[0]
    starts, tile_pattern, _ = _na_tiles(seq // GRID_W)
    n_tiles = len(starts)
    tq = NA_TILE_ROWS * GRID_W
    span = NA_SPAN_ROWS * GRID_W
    _fill_v_ext(vext_ref, v_ref)

    def keys(t):
        return slice(starts[t] * GRID_W, starts[t] * GRID_W + span)

    def logits(t):
        return _dot_nt(q_ref[t * tq:(t + 1) * tq, :], k_ref[keys(t), :])

    def finish(t, s):
        s = s + bias_ref[tile_pattern[t]]
        e = jnp.exp2(s - jnp.max(s, axis=-1, keepdims=True)).astype(_BF16)
        acc = _dot(e, vext_ref[keys(t), :])
        rows = slice(t * tq, (t + 1) * tq)
        o = acc[:, :HEAD_DIM] / acc[:, HEAD_DIM:]
        o_ref[rows, :] = (o * g_ref[rows, :].astype(_F32)).astype(_BF16)

    _pipelined(n_tiles, logits, finish, depth=2)


def _na_attn(cqkv, gates, bias, batch, seq, layer):
    t = cqkv.shape[0]
    gate_blk = (A_WIDTH + B_WIDTH) // HEAD_DIM
    blk = lambda off: pl.BlockSpec((seq, HEAD_DIM), lambda hd, b: (b, off + hd))
    return pl.pallas_call(
        _na_attn_kernel,
        grid=(C_HEADS, batch),
        in_specs=[
            blk(0), blk(C_HEADS), blk(2 * C_HEADS),
            pl.BlockSpec((seq, HEAD_DIM), lambda hd, b: (b, gate_blk + hd)),
            pl.BlockSpec((None, None) + bias.shape[2:], lambda hd, b: (layer, hd, 0, 0, 0)),
        ],
        out_specs=pl.BlockSpec((seq, HEAD_DIM), lambda hd, b: (b, hd)),
        out_shape=jax.ShapeDtypeStruct((t, C_WIDTH), _BF16),
        scratch_shapes=[pltpu.VMEM((seq, 2 * HEAD_DIM), _BF16)],
        compiler_params=_params(2),
        name="na_attn",
    )(cqkv, cqkv, cqkv, gates, bias)


def _out_ple_kernel(ya_ref, yb_ref, yc_ref, h_ref, p_ref, wo_ref, gple_ref, wg_ref, wp_ref, gpost_ref,
                    o_ref):
    tm = h_ref.shape[0]
    halves = [slice(i * (tm // 2), (i + 1) * (tm // 2)) for i in range(2)]
    pp = [_rms(_dot(p_ref[rows, :].astype(_BF16), wp_ref[...]), gpost_ref[...]) for rows in halves]
    h1 = []
    for rows in halves:
        acc = _dot(ya_ref[rows, :], wo_ref[:A_WIDTH, :])
        acc += _dot(yb_ref[rows, :], wo_ref[A_WIDTH:A_WIDTH + B_WIDTH, :])
        acc += _dot(yc_ref[rows, :], wo_ref[A_WIDTH + B_WIDTH:, :])
        h1.append(h_ref[rows, :] + acc)
    logit = [_dot(_rms(x, gple_ref[...]).astype(_BF16), wg_ref[...]) for x in h1]
    for i, rows in enumerate(halves):
        o_ref[rows, :] = h1[i] + jax.nn.sigmoid(logit[i]) * pp[i]


def _out_ple(ya, yb, yc, h, p, lw, layer, tm):
    t = h.shape[0]
    row = lambda i: (i, 0)
    vec = lambda n: pl.BlockSpec((None, 1, n), lambda i: (layer, 0, 0))
    mat = lambda r, c: _resident((None, r, c), lambda i: (layer, 0, 0))
    return pl.pallas_call(
        _out_ple_kernel,
        grid=(t // tm,),
        in_specs=[
            pl.BlockSpec((tm, A_WIDTH), row),
            pl.BlockSpec((tm, B_WIDTH), row),
            pl.BlockSpec((tm, C_WIDTH), row),
            pl.BlockSpec((tm, D_MODEL), row),
            pl.BlockSpec((None, tm, PLE_DIM), lambda i: (layer, i, 0)),
            mat(MIX_WIDTH, D_MODEL), vec(D_MODEL), mat(D_MODEL, D_MODEL), mat(PLE_DIM, D_MODEL),
            vec(D_MODEL),
        ],
        out_specs=pl.BlockSpec((tm, D_MODEL), row),
        out_shape=jax.ShapeDtypeStruct((t, D_MODEL), _F32),
        compiler_params=_params(1),
        name="out_ple",
    )(ya, yb, yc, h, p, lw['w_out'], lw['ple_norm'], lw['w_ple_gate'], lw['w_ple_proj'],
      lw['ple_post_norm'])


def _swap_halves(x):
    half = x.shape[-1] // 2
    return jnp.concatenate([x[..., half:], x[..., :half]], axis=-1)


def _prepare_weights(W):
    depth = W['w_in'].shape[0]
    w_in = W['w_in']
    assert w_in.shape[-1] == IN_HI_START + IN_HI_WIDTH
    bkr = w_in[:, :, IN_LO_WIDTH:IN_HI_START]
    w_in_kr = jnp.concatenate([bkr, _swap_halves(bkr)], axis=-1).astype(_BF16)

    wq = W['b_w_uq'].reshape(depth, B_Q_LORA, B_HEADS, B_QK)
    wq = jnp.concatenate([wq, _swap_halves(wq[..., B_NOPE:])], axis=-1)
    wq = wq.reshape(depth, B_Q_LORA, B_HEADS * B_QK_PAD).astype(_BF16)
    gq = W['b_q_norm']
    gk = W['b_k_norm']
    row = lambda g: g[:, None, :].astype(_F32)
    return {
        'norm_in': row(W['norm_in']),
        'w_in_lo': w_in[:, :, :IN_LO_WIDTH].astype(_BF16), 'w_in_hi': w_in[:, :, IN_HI_START:].astype(_BF16),
        'w_in_kr': w_in_kr,
        'a_q_norm': row(W['a_q_norm']), 'a_k_norm': row(W['a_k_norm']), 'a_sink': W['a_sink'].astype(_F32),
        'b_cq_norm': row(W['b_cq_norm']), 'b_ckv_norm': row(W['b_ckv_norm']),
        'b_w_uq': wq, 'b_w_ukv': W['b_w_ukv'].astype(_BF16),
        'b_q_gain': row(jnp.concatenate([gq, _swap_halves(gq[:, B_NOPE:])], axis=-1)),
        'b_k_gain_nope': row(gk[:, :B_NOPE]),
        'b_k_gain_rope': row(jnp.concatenate([gk[:, B_NOPE:], _swap_halves(gk[:, B_NOPE:])], axis=-1)),
        'c_q_norm': row(W['c_q_norm']), 'c_k_norm': row(W['c_k_norm']),
        'w_out': W['w_out'].astype(_BF16), 'ple_norm': row(W['ple_norm']),
        'w_ple_gate': W['w_ple_gate'].astype(_BF16), 'w_ple_proj': W['w_ple_proj'].astype(_BF16),
        'ple_post_norm': row(W['ple_post_norm']),
    }


def _rope_tables(seq):
    half = B_ROPE // 2
    inv = ROPE_THETA ** (-jnp.arange(half, dtype=_F32) / half)
    ang = jnp.arange(seq).astype(_F32)[:, None] * inv[None, :]
    cos, sin = jnp.cos(ang), jnp.sin(ang)
    zeros = jnp.zeros((seq, B_ROPE), _F32)
    return {'cos2': jnp.concatenate([cos, cos, zeros], axis=-1),
            'sin2': jnp.concatenate([-sin, sin, zeros], axis=-1)}


def _na_bias_table(rpb, seq):
    rows = seq // GRID_W
    wr = min(NA_ROWS_MAX, rows)
    depth = rpb.shape[0]
    c = np.arange(GRID_W)
    cs = np.clip(c - NA_COLS // 2, 0, GRID_W - NA_COLS)
    col_valid = (c[None, :] >= cs[:, None]) & (c[None, :] < cs[:, None] + NA_COLS)
    dc = np.clip(c[None, :] - c[:, None] + NA_COLS - 1, 0, 2 * NA_COLS - 2)
    by_dc = jnp.where(col_valid, rpb.astype(_F32)[:, :, :, dc] * LOG2E, NEG_INF)
    by_dc = by_dc.transpose(0, 1, 3, 2, 4)
    _, _, patterns = _na_tiles(rows)
    top = NA_ROWS_MAX - 1

    def masked(n):
        return jnp.full((depth, C_HEADS, GRID_W, n, GRID_W), NEG_INF, _F32)

    slabs = []
    for pattern in patterns:
        per_row = []
        for first, q_off in pattern:
            lo = first - q_off + top
            per_row.append(jnp.concatenate(
                [masked(first), by_dc[:, :, :, lo:lo + wr], masked(NA_SPAN_ROWS - first - wr)], axis=3))
        slab = jnp.stack(per_row, axis=2)
        slabs.append(slab.reshape(depth, C_HEADS, NA_TILE_ROWS * GRID_W, NA_SPAN_ROWS * GRID_W))
    return jnp.stack(slabs, axis=2)


def _trunk(x, p, lw, tables, na_bias):
    batch, seq, _ = x.shape
    depth = p.shape[0]
    t = batch * seq
    h = x.reshape(t, D_MODEL)
    p = p.reshape(depth, t, PLE_DIM)
    for layer in range(depth):
        aq, akv, gates, cqkv, bq, bk, bv = _in_proj(h, lw, tables, layer, seq, IN_PROJ_ROWS)
        ya = _win_attn(aq, akv, gates, lw['a_sink'][layer], batch, seq)
        yb = _mla_attn(bq, bk, bv, gates, batch, seq, MLA_QUERY_ROWS)
        yc = _na_attn(cqkv, gates, na_bias, batch, seq, layer)
        h = _out_ple(ya, yb, yc, h, p, lw, layer, OUT_PLE_ROWS)
    return h.reshape(batch, seq, D_MODEL)


def kernel(x_prompt, x_sample, p_prompt, p_sample, norm_in, w_in, a_q_norm, a_k_norm, a_sink, b_cq_norm, b_ckv_norm, b_w_uq, b_w_ukv, b_q_norm, b_k_norm, c_q_norm, c_k_norm, c_rpb, w_out, ple_norm, w_ple_gate, w_ple_proj, ple_post_norm):
    lw = _prepare_weights({
        'norm_in': norm_in, 'w_in': w_in, 'a_q_norm': a_q_norm, 'a_k_norm': a_k_norm, 'a_sink': a_sink,
        'b_cq_norm': b_cq_norm, 'b_ckv_norm': b_ckv_norm, 'b_w_uq': b_w_uq, 'b_w_ukv': b_w_ukv,
        'b_q_norm': b_q_norm, 'b_k_norm': b_k_norm, 'c_q_norm': c_q_norm, 'c_k_norm': c_k_norm,
        'w_out': w_out, 'ple_norm': ple_norm, 'w_ple_gate': w_ple_gate, 'w_ple_proj': w_ple_proj,
        'ple_post_norm': ple_post_norm,
    })
    tables = {}
    outs = []
    for x, p in ((x_prompt, p_prompt), (x_sample, p_sample)):
        seq = x.shape[1]
        if seq not in tables:
            tables[seq] = (_rope_tables(seq), _na_bias_table(c_rpb, seq))
        outs.append(_trunk(x, p, lw, *tables[seq]))
    return tuple(outs)
```

```python
import functools

import jax
import jax.numpy as jnp
import numpy as np
from jax import lax
from jax.experimental import pallas as pl
from jax.experimental.pallas import tpu as pltpu

D_MODEL = 2048
PLE_DIM = 256
HEAD_DIM = 128
A_HEADS = 6
A_KV_HEADS = 2
A_GROUP = A_HEADS // A_KV_HEADS
A_WINDOW = 128
A_BLOCK = 128
B_HEADS = 6
B_Q_LORA = 512
B_KV_LORA = 512
B_NOPE = 128
B_ROPE = 64
B_V = 128
B_QK = B_NOPE + B_ROPE
ROPE_THETA = 10000.0
C_HEADS = 4
GRID_W = 64
NA_ROWS_MAX = 8
NA_COLS = 16
A_WIDTH = A_HEADS * HEAD_DIM
A_KV_WIDTH = A_KV_HEADS * HEAD_DIM
B_WIDTH = B_HEADS * B_V
C_WIDTH = C_HEADS * HEAD_DIM
MIX_WIDTH = A_WIDTH + B_WIDTH + C_WIDTH
EPS = 1e-6
NEG_INF = -1e30
LOG2E = 1.4426950408889634

B_QK_PAD = 2 * HEAD_DIM
IN_LO_WIDTH = 2 * A_WIDTH + 2 * A_KV_WIDTH + B_Q_LORA + B_KV_LORA
IN_HI_START = IN_LO_WIDTH + B_ROPE
IN_HI_WIDTH = B_WIDTH + 4 * C_WIDTH
LO_A_Q = 0
LO_A_KV = LO_A_Q + A_WIDTH
LO_A_Z = LO_A_KV + 2 * A_KV_WIDTH
LO_B_LAT = LO_A_Z + A_WIDTH
HI_B_Z = 0
HI_C_QKV = HI_B_Z + B_WIDTH
HI_C_Z = HI_C_QKV + 3 * C_WIDTH

VMEM_LIMIT_BYTES = 56 * 1024 * 1024
IN_PROJ_ROWS = 256
OUT_PLE_ROWS = 512
MLA_QUERY_ROWS = 256
MLA_HEADS_PER_STEP = 3
NA_HEADS_PER_STEP = 2
NA_TILE_ROWS = 4
NA_SPAN_ROWS = 12

_BF16 = jnp.bfloat16
_F32 = jnp.float32


def _params(n_grid_dims):
    return pltpu.CompilerParams(
        dimension_semantics=("arbitrary",) * n_grid_dims,
        vmem_limit_bytes=VMEM_LIMIT_BYTES)


def _resident(shape, index_map):
    return pl.BlockSpec(shape, index_map, pipeline_mode=pl.Buffered(1))


def _rms(x, g):
    return x * lax.rsqrt(jnp.mean(x * x, axis=-1, keepdims=True) + EPS) * g


def _dot(a, b):
    return jnp.dot(a, b, preferred_element_type=_F32)


def _dot_nt(a, b):
    return lax.dot_general(a, b, (((1,), (1,)), ((), ())), preferred_element_type=_F32)


def _in_proj_kernel(h_ref, gin_ref, wlo_ref, whi_ref, wkr_ref, gaq_ref, gak_ref, gbq_ref, gbkv_ref,
                    gcq_ref, gck_ref,
                    wq_ref, wkv_ref, gq_ref, gkn_ref, gkr_ref, cos_ref, sin_ref,
                    aq_ref, akv_ref, gates_ref, cqkv_ref, bq_ref, bk_ref, bv_ref):
    u = _rms(h_ref[...], gin_ref[...]).astype(_BF16)
    scale = HEAD_DIM ** -0.5 * LOG2E
    b_scale = B_QK ** -0.5 * LOG2E

    def seg(w_ref, start, width):
        return _dot(u, w_ref[:, start:start + width])

    def heads(x, g, n, s=None):
        outs = []
        for i in range(n):
            y = _rms(x[:, i * HEAD_DIM:(i + 1) * HEAD_DIM], g)
            outs.append(y if s is None else y * s)
        return outs

    x = seg(wlo_ref, LO_B_LAT, B_Q_LORA + B_KV_LORA)
    kr2 = _dot(u, wkr_ref[...])
    cqn = _rms(x[:, :B_Q_LORA], gbq_ref[...]).astype(_BF16)
    ckvn = _rms(x[:, B_Q_LORA:], gbkv_ref[...]).astype(_BF16)

    out_col = 0
    for w_ref, start, width in ((wlo_ref, LO_A_Z, A_WIDTH), (whi_ref, HI_B_Z, B_WIDTH),
                                (whi_ref, HI_C_Z, C_WIDTH)):
        z = seg(w_ref, start, width)
        gates_ref[:, out_col:out_col + width] = (z * jax.nn.sigmoid(z)).astype(_BF16)
        out_col += width

    xq = _dot(cqn, wq_ref[...])
    xkv = _dot(ckvn, wkv_ref[...])

    xa = seg(wlo_ref, LO_A_Q, A_WIDTH)
    xakv = seg(wlo_ref, LO_A_KV, 2 * A_KV_WIDTH)
    xc = seg(whi_ref, HI_C_QKV, 3 * C_WIDTH)

    cos2 = cos_ref[...]
    sin2 = sin_ref[...]
    rope_lanes = lax.broadcasted_iota(jnp.int32, (1, HEAD_DIM), 1) < B_ROPE

    def rope(c):
        return c * cos2 + pltpu.roll(c, B_ROPE, 1) * sin2

    def sumsq(y, lanes=None):
        y2 = y * y
        return jnp.sum(y2 if lanes is None else jnp.where(lanes, y2, 0.0), axis=-1, keepdims=True)

    ss_kr = sumsq(kr2, rope_lanes)
    k_rope = rope(kr2 * gkr_ref[...])
    for hd in range(B_HEADS):
        lo, mid, hi = hd * B_QK_PAD, hd * B_QK_PAD + HEAD_DIM, (hd + 1) * B_QK_PAD
        a = xq[:, lo:mid]
        c = xq[:, mid:hi]
        r = lax.rsqrt((sumsq(a) + sumsq(c, rope_lanes)) * (1.0 / B_QK) + EPS)
        bq_ref[:, lo:mid] = (a * r * gq_ref[:, :HEAD_DIM] * b_scale).astype(_BF16)
        bq_ref[:, mid:hi] = (rope(c * r * gq_ref[:, HEAD_DIM:]) * b_scale).astype(_BF16)
        kn = xkv[:, lo:mid]
        r = lax.rsqrt((sumsq(kn) + ss_kr) * (1.0 / B_QK) + EPS)
        bk_ref[:, lo:mid] = (kn * r * gkn_ref[...]).astype(_BF16)
        bk_ref[:, mid:hi] = (k_rope * r).astype(_BF16)
        bv_ref[:, hd * B_V:(hd + 1) * B_V] = xkv[:, mid:hi].astype(_BF16)

    for i, y in enumerate(heads(xa, gaq_ref[...], A_HEADS, scale)):
        aq_ref[:, i * HEAD_DIM:(i + 1) * HEAD_DIM] = y.astype(_BF16)
    for i, y in enumerate(heads(xakv[:, :A_KV_WIDTH], gak_ref[...], A_KV_HEADS)):
        akv_ref[:, i * HEAD_DIM:(i + 1) * HEAD_DIM] = y.astype(_BF16)
    akv_ref[:, A_KV_WIDTH:] = xakv[:, A_KV_WIDTH:].astype(_BF16)
    for i, y in enumerate(heads(xc[:, :C_WIDTH], gcq_ref[...], C_HEADS, scale)):
        cqkv_ref[:, i * HEAD_DIM:(i + 1) * HEAD_DIM] = y.astype(_BF16)
    for i, y in enumerate(heads(xc[:, C_WIDTH:2 * C_WIDTH], gck_ref[...], C_HEADS)):
        cqkv_ref[:, C_WIDTH + i * HEAD_DIM:C_WIDTH + (i + 1) * HEAD_DIM] = y.astype(_BF16)
    cqkv_ref[:, 2 * C_WIDTH:] = xc[:, 2 * C_WIDTH:].astype(_BF16)


def _in_proj(h, lw, tables, layer, seq, tm):
    t = h.shape[0]
    row = lambda i: (i, 0)
    pos = lambda i: (i % (seq // tm), 0)
    vec = lambda n: pl.BlockSpec((None, 1, n), lambda i: (layer, 0, 0))
    mat = lambda r, c: _resident((None, r, c), lambda i: (layer, 0, 0))
    widths = (A_WIDTH, 2 * A_KV_WIDTH, MIX_WIDTH, 3 * C_WIDTH, B_HEADS * B_QK_PAD, B_HEADS * B_QK_PAD,
              B_WIDTH)
    return pl.pallas_call(
        _in_proj_kernel,
        grid=(t // tm,),
        in_specs=[
            pl.BlockSpec((tm, D_MODEL), row),
            vec(D_MODEL),
            mat(D_MODEL, IN_LO_WIDTH), mat(D_MODEL, IN_HI_WIDTH), mat(D_MODEL, 2 * B_ROPE),
            vec(HEAD_DIM), vec(HEAD_DIM), vec(B_Q_LORA), vec(B_KV_LORA), vec(HEAD_DIM), vec(HEAD_DIM),
            mat(B_Q_LORA, B_HEADS * B_QK_PAD), mat(B_KV_LORA, B_HEADS * (B_NOPE + B_V)),
            vec(B_QK_PAD), vec(HEAD_DIM), vec(2 * B_ROPE),
            pl.BlockSpec((tm, 2 * B_ROPE), pos),
            pl.BlockSpec((tm, 2 * B_ROPE), pos),
        ],
        out_specs=[pl.BlockSpec((tm, w), row) for w in widths],
        out_shape=[jax.ShapeDtypeStruct((t, w), _BF16) for w in widths],
        compiler_params=_params(1),
        name="in_proj",
    )(h, lw['norm_in'], lw['w_in_lo'], lw['w_in_hi'], lw['w_in_kr'], lw['a_q_norm'], lw['a_k_norm'],
      lw['b_cq_norm'], lw['b_ckv_norm'], lw['c_q_norm'], lw['c_k_norm'], lw['b_w_uq'], lw['b_w_ukv'], lw['b_q_gain'], lw['b_k_gain_nope'],
      lw['b_k_gain_rope'], tables['cos2'], tables['sin2'])


def _fill_v_ext(vext_ref, v_ref):
    heads, rows, ext = vext_ref.shape
    width = v_ref.shape[1] // heads
    for h in range(heads):
        vext_ref[h, :, :width] = v_ref[:, h * width:(h + 1) * width]
        vext_ref[h, :, width:] = jnp.ones((rows, ext - width), vext_ref.dtype)


def _pipelined(n, logits, finish, depth):
    queue = [logits(i) for i in range(min(depth, n))]
    for i in range(n):
        s = queue.pop(0)
        if i + depth < n:
            queue.append(logits(i + depth))
        finish(i, s)


def _mla_attn_kernel(q_ref, k_ref, v_ref, g_ref, o_ref, vext_ref, *, tq):
    heads = vext_ref.shape[0]
    n_chunks = q_ref.shape[0] // tq
    _fill_v_ext(vext_ref, v_ref)

    def logits(i):
        h, c = divmod(i, n_chunks)
        qk_cols = slice(h * B_QK_PAD, (h + 1) * B_QK_PAD)
        return _dot_nt(q_ref[c * tq:(c + 1) * tq, qk_cols], k_ref[:, qk_cols])

    def finish(i, s):
        h, c = divmod(i, n_chunks)
        e = jnp.exp2(s - jnp.max(s, axis=-1, keepdims=True)).astype(_BF16)
        acc = _dot(e, vext_ref[h])
        rows = slice(c * tq, (c + 1) * tq)
        cols = slice(h * B_V, (h + 1) * B_V)
        o = acc[:, :B_V] / acc[:, B_V:]
        o_ref[rows, cols] = (o * g_ref[rows, cols].astype(_F32)).astype(_BF16)

    _pipelined(heads * n_chunks, logits, finish, depth=1)


def _mla_attn(q, k, v, gates, batch, seq, tq):
    t = q.shape[0]
    heads = MLA_HEADS_PER_STEP
    gate_blk = A_WIDTH // (heads * B_V)
    return pl.pallas_call(
        functools.partial(_mla_attn_kernel, tq=tq),
        grid=(batch, B_HEADS // heads),
        in_specs=[
            pl.BlockSpec((seq, heads * B_QK_PAD), lambda b, hg: (b, hg)),
            pl.BlockSpec((seq, heads * B_QK_PAD), lambda b, hg: (b, hg)),
            pl.BlockSpec((seq, heads * B_V), lambda b, hg: (b, hg)),
            pl.BlockSpec((seq, heads * B_V), lambda b, hg: (b, gate_blk + hg)),
        ],
        out_specs=pl.BlockSpec((seq, heads * B_V), lambda b, hg: (b, hg)),
        out_shape=jax.ShapeDtypeStruct((t, B_WIDTH), _BF16),
        scratch_shapes=[pltpu.VMEM((heads, seq, 2 * B_V), _BF16)],
        compiler_params=_params(2),
        name="mla_attn",
    )(q, k, v, gates)


def _win_attn_kernel(sink_ref, slope_ref, q_ref, k_ref, v_ref, g_ref, o_ref, bias_ref, vext_ref):
    seq = q_ref.shape[0]
    nb = seq // A_BLOCK
    span = 3 * A_BLOCK
    _fill_v_ext(vext_ref, v_ref)

    @pl.when(pl.program_id(0) == 0)
    def _():
        qi = lax.broadcasted_iota(jnp.int32, (A_BLOCK, span), 0)
        si = lax.broadcasted_iota(jnp.int32, (A_BLOCK, span), 1)
        for variant in range(3):
            dist = jnp.abs(si - qi - variant * A_BLOCK)
            valid = dist <= A_WINDOW
            for head in range(A_HEADS):
                slope = slope_ref[head] * LOG2E
                bias_ref[head // A_GROUP, variant, (head % A_GROUP) * A_BLOCK:(head % A_GROUP + 1) * A_BLOCK, :] = (
                    jnp.where(valid, -(slope * dist.astype(_F32)), NEG_INF))

    sinks = [sink_ref[head] * LOG2E for head in range(A_HEADS)]

    def key_start(j):
        return min(max((j - 1) * A_BLOCK, 0), seq - span)

    def logits(i):
        kvh, j = divmod(i, nb)
        qrows = slice(j * A_BLOCK, (j + 1) * A_BLOCK)
        qs = jnp.concatenate(
            [q_ref[qrows, (kvh * A_GROUP + g) * HEAD_DIM:(kvh * A_GROUP + g + 1) * HEAD_DIM]
             for g in range(A_GROUP)], axis=0)
        return _dot_nt(qs, k_ref[key_start(j):key_start(j) + span, kvh * HEAD_DIM:(kvh + 1) * HEAD_DIM])

    def finish(i, s):
        kvh, j = divmod(i, nb)
        start = key_start(j)
        s = s + bias_ref[kvh, (j * A_BLOCK - start) // A_BLOCK]
        probs, sink_terms = [], []
        for g in range(A_GROUP):
            sink = sinks[kvh * A_GROUP + g]
            sg = s[g * A_BLOCK:(g + 1) * A_BLOCK]
            m = jnp.maximum(jnp.max(sg, axis=-1, keepdims=True), sink)
            probs.append(jnp.exp2(sg - m).astype(_BF16))
            sink_terms.append(jnp.exp2(sink - m))
        acc = _dot(jnp.concatenate(probs, axis=0), vext_ref[kvh, start:start + span, :])
        qrows = slice(j * A_BLOCK, (j + 1) * A_BLOCK)
        for g in range(A_GROUP):
            ag = acc[g * A_BLOCK:(g + 1) * A_BLOCK]
            o = ag[:, :HEAD_DIM] / (ag[:, HEAD_DIM:] + sink_terms[g])
            cols = slice((kvh * A_GROUP + g) * HEAD_DIM, (kvh * A_GROUP + g + 1) * HEAD_DIM)
            o_ref[qrows, cols] = (o * g_ref[qrows, cols].astype(_F32)).astype(_BF16)

    _pipelined(A_KV_HEADS * nb, logits, finish, depth=2)


def _win_attn(aq, akv, gates, sink, batch, seq):
    t = aq.shape[0]
    assert seq >= 3 * A_BLOCK
    slopes = jnp.exp2(-8.0 * jnp.arange(1, A_HEADS + 1, dtype=_F32) / A_HEADS)
    return pl.pallas_call(
        _win_attn_kernel,
        grid=(batch,),
        in_specs=[
            pl.BlockSpec(memory_space=pltpu.SMEM),
            pl.BlockSpec(memory_space=pltpu.SMEM),
            pl.BlockSpec((seq, A_WIDTH), lambda b: (b, 0)),
            pl.BlockSpec((seq, A_KV_WIDTH), lambda b: (b, 0)),
            pl.BlockSpec((seq, A_KV_WIDTH), lambda b: (b, 1)),
            pl.BlockSpec((seq, A_WIDTH), lambda b: (b, 0)),
        ],
        out_specs=pl.BlockSpec((seq, A_WIDTH), lambda b: (b, 0)),
        out_shape=jax.ShapeDtypeStruct((t, A_WIDTH), _BF16),
        scratch_shapes=[
            pltpu.VMEM((A_KV_HEADS, 3, A_GROUP * A_BLOCK, 3 * A_BLOCK), _F32),
            pltpu.VMEM((A_KV_HEADS, seq, 2 * HEAD_DIM), _BF16),
        ],
        compiler_params=_params(1),
        name="win_attn",
    )(sink, slopes, aq, akv, akv, gates)


def _na_tiles(rows):
    wr = min(NA_ROWS_MAX, rows)
    assert rows % NA_TILE_ROWS == 0 and rows >= NA_SPAN_ROWS
    starts, tile_pattern, patterns = [], [], []
    for t in range(rows // NA_TILE_ROWS):
        ws = min(max(NA_TILE_ROWS * t - wr // 2, 0), rows - NA_SPAN_ROWS)
        pattern = []
        for i in range(NA_TILE_ROWS):
            r = NA_TILE_ROWS * t + i
            rs = min(max(r - wr // 2, 0), rows - wr)
            assert ws <= rs and rs + wr <= ws + NA_SPAN_ROWS
            pattern.append((rs - ws, r - ws))
        pattern = tuple(pattern)
        if pattern not in patterns:
            patterns.append(pattern)
        starts.append(ws)
        tile_pattern.append(patterns.index(pattern))
    return starts, tile_pattern, patterns


def _na_attn_kernel(q_ref, k_ref, v_ref, g_ref, bias_ref, o_ref, vext_ref):
    seq = q_ref.shape[0]
    starts, tile_pattern, _ = _na_tiles(seq // GRID_W)
    n_tiles = len(starts)
    tq = NA_TILE_ROWS * GRID_W
    span = NA_SPAN_ROWS * GRID_W
    _fill_v_ext(vext_ref, v_ref)

    heads = vext_ref.shape[0]

    def keys(t):
        return slice(starts[t] * GRID_W, starts[t] * GRID_W + span)

    def logits(i):
        h, t = divmod(i, n_tiles)
        cols = slice(h * HEAD_DIM, (h + 1) * HEAD_DIM)
        return _dot_nt(q_ref[t * tq:(t + 1) * tq, cols], k_ref[keys(t), cols])

    def finish(i, s):
        h, t = divmod(i, n_tiles)
        s = s + bias_ref[h, tile_pattern[t]]
        e = jnp.exp2(s - jnp.max(s, axis=-1, keepdims=True)).astype(_BF16)
        acc = _dot(e, vext_ref[h, keys(t), :])
        rows = slice(t * tq, (t + 1) * tq)
        cols = slice(h * HEAD_DIM, (h + 1) * HEAD_DIM)
        o = acc[:, :HEAD_DIM] / acc[:, HEAD_DIM:]
        o_ref[rows, cols] = (o * g_ref[rows, cols].astype(_F32)).astype(_BF16)

    _pipelined(heads * n_tiles, logits, finish, depth=2)


def _na_attn(cqkv, gates, bias, batch, seq, layer):
    t = cqkv.shape[0]
    heads = NA_HEADS_PER_STEP
    width = heads * HEAD_DIM
    groups = C_HEADS // heads
    gate_blk = (A_WIDTH + B_WIDTH) // width
    blk = lambda off: pl.BlockSpec((seq, width), lambda hg, b: (b, off + hg))
    return pl.pallas_call(
        _na_attn_kernel,
        grid=(groups, batch),
        in_specs=[
            blk(0), blk(groups), blk(2 * groups), blk(gate_blk),
            pl.BlockSpec((None, heads) + bias.shape[2:], lambda hg, b: (layer, hg, 0, 0, 0)),
        ],
        out_specs=pl.BlockSpec((seq, width), lambda hg, b: (b, hg)),
        out_shape=jax.ShapeDtypeStruct((t, C_WIDTH), _BF16),
        scratch_shapes=[pltpu.VMEM((heads, seq, 2 * HEAD_DIM), _BF16)],
        compiler_params=_params(2),
        name="na_attn",
    )(cqkv, cqkv, cqkv, gates, bias)


def _out_ple_kernel(ya_ref, yb_ref, yc_ref, h_ref, p_ref, wo_ref, gple_ref, wg_ref, wp_ref, gpost_ref,
                    o_ref):
    tm = h_ref.shape[0]
    halves = [slice(i * (tm // 2), (i + 1) * (tm // 2)) for i in range(2)]
    pp = [_rms(_dot(p_ref[rows, :].astype(_BF16), wp_ref[...]), gpost_ref[...]) for rows in halves]
    h1 = []
    for rows in halves:
        acc = _dot(ya_ref[rows, :], wo_ref[:A_WIDTH, :])
        acc += _dot(yb_ref[rows, :], wo_ref[A_WIDTH:A_WIDTH + B_WIDTH, :])
        acc += _dot(yc_ref[rows, :], wo_ref[A_WIDTH + B_WIDTH:, :])
        h1.append(h_ref[rows, :] + acc)
    logit = [_dot(_rms(x, gple_ref[...]).astype(_BF16), wg_ref[...]) for x in h1]
    for i, rows in enumerate(halves):
        o_ref[rows, :] = h1[i] + jax.nn.sigmoid(logit[i]) * pp[i]


def _out_ple(ya, yb, yc, h, p, lw, layer, tm):
    t = h.shape[0]
    row = lambda i: (i, 0)
    vec = lambda n: pl.BlockSpec((None, 1, n), lambda i: (layer, 0, 0))
    mat = lambda r, c: _resident((None, r, c), lambda i: (layer, 0, 0))
    return pl.pallas_call(
        _out_ple_kernel,
        grid=(t // tm,),
        in_specs=[
            pl.BlockSpec((tm, A_WIDTH), row),
            pl.BlockSpec((tm, B_WIDTH), row),
            pl.BlockSpec((tm, C_WIDTH), row),
            pl.BlockSpec((tm, D_MODEL), row),
            pl.BlockSpec((None, tm, PLE_DIM), lambda i: (layer, i, 0)),
            mat(MIX_WIDTH, D_MODEL), vec(D_MODEL), mat(D_MODEL, D_MODEL), mat(PLE_DIM, D_MODEL),
            vec(D_MODEL),
        ],
        out_specs=pl.BlockSpec((tm, D_MODEL), row),
        out_shape=jax.ShapeDtypeStruct((t, D_MODEL), _F32),
        compiler_params=_params(1),
        name="out_ple",
    )(ya, yb, yc, h, p, lw['w_out'], lw['ple_norm'], lw['w_ple_gate'], lw['w_ple_proj'],
      lw['ple_post_norm'])


def _swap_halves(x):
    half = x.shape[-1] // 2
    return jnp.concatenate([x[..., half:], x[..., :half]], axis=-1)


def _prepare_weights(W):
    depth = W['w_in'].shape[0]
    w_in = W['w_in']
    assert w_in.shape[-1] == IN_HI_START + IN_HI_WIDTH
    bkr = w_in[:, :, IN_LO_WIDTH:IN_HI_START]
    w_in_kr = jnp.concatenate([bkr, _swap_halves(bkr)], axis=-1).astype(_BF16)

    wq = W['b_w_uq'].reshape(depth, B_Q_LORA, B_HEADS, B_QK)
    wq = jnp.concatenate([wq, _swap_halves(wq[..., B_NOPE:])], axis=-1)
    wq = wq.reshape(depth, B_Q_LORA, B_HEADS * B_QK_PAD).astype(_BF16)
    gq = W['b_q_norm']
    gk = W['b_k_norm']
    row = lambda g: g[:, None, :].astype(_F32)
    return {
        'norm_in': row(W['norm_in']),
        'w_in_lo': w_in[:, :, :IN_LO_WIDTH].astype(_BF16), 'w_in_hi': w_in[:, :, IN_HI_START:].astype(_BF16),
        'w_in_kr': w_in_kr,
        'a_q_norm': row(W['a_q_norm']), 'a_k_norm': row(W['a_k_norm']), 'a_sink': W['a_sink'].astype(_F32),
        'b_cq_norm': row(W['b_cq_norm']), 'b_ckv_norm': row(W['b_ckv_norm']),
        'b_w_uq': wq, 'b_w_ukv': W['b_w_ukv'].astype(_BF16),
        'b_q_gain': row(jnp.concatenate([gq, _swap_halves(gq[:, B_NOPE:])], axis=-1)),
        'b_k_gain_nope': row(gk[:, :B_NOPE]),
        'b_k_gain_rope': row(jnp.concatenate([gk[:, B_NOPE:], _swap_halves(gk[:, B_NOPE:])], axis=-1)),
        'c_q_norm': row(W['c_q_norm']), 'c_k_norm': row(W['c_k_norm']),
        'w_out': W['w_out'].astype(_BF16), 'ple_norm': row(W['ple_norm']),
        'w_ple_gate': W['w_ple_gate'].astype(_BF16), 'w_ple_proj': W['w_ple_proj'].astype(_BF16),
        'ple_post_norm': row(W['ple_post_norm']),
    }


def _rope_tables(seq):
    half = B_ROPE // 2
    inv = ROPE_THETA ** (-jnp.arange(half, dtype=_F32) / half)
    ang = jnp.arange(seq).astype(_F32)[:, None] * inv[None, :]
    cos, sin = jnp.cos(ang), jnp.sin(ang)
    zeros = jnp.zeros((seq, B_ROPE), _F32)
    return {'cos2': jnp.concatenate([cos, cos, zeros], axis=-1),
            'sin2': jnp.concatenate([-sin, sin, zeros], axis=-1)}


def _na_bias_table(rpb, seq):
    rows = seq // GRID_W
    wr = min(NA_ROWS_MAX, rows)
    depth = rpb.shape[0]
    c = np.arange(GRID_W)
    cs = np.clip(c - NA_COLS // 2, 0, GRID_W - NA_COLS)
    col_valid = (c[None, :] >= cs[:, None]) & (c[None, :] < cs[:, None] + NA_COLS)
    dc = np.clip(c[None, :] - c[:, None] + NA_COLS - 1, 0, 2 * NA_COLS - 2)
    by_dc = jnp.where(col_valid, rpb.astype(_F32)[:, :, :, dc] * LOG2E, NEG_INF)
    by_dc = by_dc.transpose(0, 1, 3, 2, 4)
    _, _, patterns = _na_tiles(rows)
    top = NA_ROWS_MAX - 1

    def masked(n):
        return jnp.full((depth, C_HEADS, GRID_W, n, GRID_W), NEG_INF, _F32)

    slabs = []
    for pattern in patterns:
        per_row = []
        for first, q_off in pattern:
            lo = first - q_off + top
            per_row.append(jnp.concatenate(
                [masked(first), by_dc[:, :, :, lo:lo + wr], masked(NA_SPAN_ROWS - first - wr)], axis=3))
        slab = jnp.stack(per_row, axis=2)
        slabs.append(slab.reshape(depth, C_HEADS, NA_TILE_ROWS * GRID_W, NA_SPAN_ROWS * GRID_W))
    return jnp.stack(slabs, axis=2)


def _trunk(x, p, lw, tables, na_bias):
    batch, seq, _ = x.shape
    depth = p.shape[0]
    t = batch * seq
    h = x.reshape(t, D_MODEL)
    p = p.reshape(depth, t, PLE_DIM)
    for layer in range(depth):
        aq, akv, gates, cqkv, bq, bk, bv = _in_proj(h, lw, tables, layer, seq, IN_PROJ_ROWS)
        ya = _win_attn(aq, akv, gates, lw['a_sink'][layer], batch, seq)
        yb = _mla_attn(bq, bk, bv, gates, batch, seq, MLA_QUERY_ROWS)
        yc = _na_attn(cqkv, gates, na_bias, batch, seq, layer)
        h = _out_ple(ya, yb, yc, h, p, lw, layer, OUT_PLE_ROWS)
    return h.reshape(batch, seq, D_MODEL)


def kernel(x_prompt, x_sample, p_prompt, p_sample, norm_in, w_in, a_q_norm, a_k_norm, a_sink, b_cq_norm, b_ckv_norm, b_w_uq, b_w_ukv, b_q_norm, b_k_norm, c_q_norm, c_k_norm, c_rpb, w_out, ple_norm, w_ple_gate, w_ple_proj, ple_post_norm):
    lw = _prepare_weights({
        'norm_in': norm_in, 'w_in': w_in, 'a_q_norm': a_q_norm, 'a_k_norm': a_k_norm, 'a_sink': a_sink,
        'b_cq_norm': b_cq_norm, 'b_ckv_norm': b_ckv_norm, 'b_w_uq': b_w_uq, 'b_w_ukv': b_w_ukv,
        'b_q_norm': b_q_norm, 'b_k_norm': b_k_norm, 'c_q_norm': c_q_norm, 'c_k_norm': c_k_norm,
        'w_out': w_out, 'ple_norm': ple_norm, 'w_ple_gate': w_ple_gate, 'w_ple_proj': w_ple_proj,
        'ple_post_norm': ple_post_norm,
    })
    tables = {}
    outs = []
    for x, p in ((x_prompt, p_prompt), (x_sample, p_sample)):
        seq = x.shape[1]
        if seq not in tables:
            tables[seq] = (_rope_tables(seq), _na_bias_table(c_rpb, seq))
        outs.append(_trunk(x, p, lw, *tables[seq]))
    return tuple(outs)
```

```python
import functools

import jax
import jax.numpy as jnp
import numpy as np
from jax import lax
from jax.experimental import pallas as pl
from jax.experimental.pallas import tpu as pltpu

D_MODEL = 2048
PLE_DIM = 256
HEAD_DIM = 128
A_HEADS = 6
A_KV_HEADS = 2
A_GROUP = A_HEADS // A_KV_HEADS
A_WINDOW = 128
A_BLOCK = 128
B_HEADS = 6
B_Q_LORA = 512
B_KV_LORA = 512
B_NOPE = 128
B_ROPE = 64
B_V = 128
B_QK = B_NOPE + B_ROPE
ROPE_THETA = 10000.0
C_HEADS = 4
GRID_W = 64
NA_ROWS_MAX = 8
NA_COLS = 16
A_WIDTH = A_HEADS * HEAD_DIM
A_KV_WIDTH = A_KV_HEADS * HEAD_DIM
B_WIDTH = B_HEADS * B_V
C_WIDTH = C_HEADS * HEAD_DIM
MIX_WIDTH = A_WIDTH + B_WIDTH + C_WIDTH
EPS = 1e-6
NEG_INF = -1e30
LOG2E = 1.4426950408889634

B_QK_PAD = 2 * HEAD_DIM
IN_LO_WIDTH = 2 * A_WIDTH + 2 * A_KV_WIDTH + B_Q_LORA + B_KV_LORA
IN_HI_START = IN_LO_WIDTH + B_ROPE
IN_HI_WIDTH = B_WIDTH + 4 * C_WIDTH
LO_A_Q = 0
LO_A_KV = LO_A_Q + A_WIDTH
LO_A_Z = LO_A_KV + 2 * A_KV_WIDTH
LO_B_LAT = LO_A_Z + A_WIDTH
HI_B_Z = 0
HI_C_QKV = HI_B_Z + B_WIDTH
HI_C_Z = HI_C_QKV + 3 * C_WIDTH

VMEM_LIMIT_BYTES = 56 * 1024 * 1024
IN_PROJ_ROWS = 256
OUT_PLE_ROWS = 512
MLA_QUERY_ROWS = 256
MLA_HEADS_PER_STEP = 3
NA_HEADS_PER_STEP = 2
NA_TILE_ROWS = 4
NA_SPAN_ROWS = 12
W_IN_SPLIT_ROWS = 256

_BF16 = jnp.bfloat16
_F32 = jnp.float32


def _params(n_grid_dims):
    return pltpu.CompilerParams(
        dimension_semantics=("arbitrary",) * n_grid_dims,
        vmem_limit_bytes=VMEM_LIMIT_BYTES)


def _resident(shape, index_map):
    return pl.BlockSpec(shape, index_map, pipeline_mode=pl.Buffered(1))


def _rms(x, g):
    return x * lax.rsqrt(jnp.mean(x * x, axis=-1, keepdims=True) + EPS) * g


def _dot(a, b):
    return jnp.dot(a, b, preferred_element_type=_F32)


def _dot_nt(a, b):
    return lax.dot_general(a, b, (((1,), (1,)), ((), ())), preferred_element_type=_F32)


def _in_proj_kernel(h_ref, gin_ref, wlo_ref, whi_ref, wkr_ref, gaq_ref, gak_ref, gbq_ref, gbkv_ref,
                    gcq_ref, gck_ref,
                    wq_ref, wkv_ref, gq_ref, gkn_ref, gkr_ref, cos_ref, sin_ref,
                    aq_ref, akv_ref, gates_ref, cqkv_ref, bq_ref, bk_ref, bv_ref):
    u = _rms(h_ref[...], gin_ref[...]).astype(_BF16)
    scale = HEAD_DIM ** -0.5 * LOG2E
    b_scale = B_QK ** -0.5 * LOG2E

    def seg(w_ref, start, width):
        return _dot(u, w_ref[:, start:start + width])

    def heads(x, g, n, s=None):
        outs = []
        for i in range(n):
            y = _rms(x[:, i * HEAD_DIM:(i + 1) * HEAD_DIM], g)
            outs.append(y if s is None else y * s)
        return outs

    x = seg(wlo_ref, LO_B_LAT, B_Q_LORA + B_KV_LORA)
    kr2 = _dot(u, wkr_ref[...])
    cqn = _rms(x[:, :B_Q_LORA], gbq_ref[...]).astype(_BF16)
    ckvn = _rms(x[:, B_Q_LORA:], gbkv_ref[...]).astype(_BF16)

    out_col = 0
    for w_ref, start, width in ((wlo_ref, LO_A_Z, A_WIDTH), (whi_ref, HI_B_Z, B_WIDTH),
                                (whi_ref, HI_C_Z, C_WIDTH)):
        z = seg(w_ref, start, width)
        gates_ref[:, out_col:out_col + width] = (z * jax.nn.sigmoid(z)).astype(_BF16)
        out_col += width

    xq = _dot(cqn, wq_ref[...])
    xkv = _dot(ckvn, wkv_ref[...])

    xa = seg(wlo_ref, LO_A_Q, A_WIDTH)
    xakv = seg(wlo_ref, LO_A_KV, 2 * A_KV_WIDTH)
    xc = seg(whi_ref, HI_C_QKV, 3 * C_WIDTH)

    cos2 = cos_ref[...]
    sin2 = sin_ref[...]
    rope_lanes = lax.broadcasted_iota(jnp.int32, (1, HEAD_DIM), 1) < B_ROPE

    def rope(c):
        return c * cos2 + pltpu.roll(c, B_ROPE, 1) * sin2

    def sumsq(y, lanes=None):
        y2 = y * y
        return jnp.sum(y2 if lanes is None else jnp.where(lanes, y2, 0.0), axis=-1, keepdims=True)

    ss_kr = sumsq(kr2, rope_lanes)
    k_rope = rope(kr2 * gkr_ref[...])
    for hd in range(B_HEADS):
        lo, mid, hi = hd * B_QK_PAD, hd * B_QK_PAD + HEAD_DIM, (hd + 1) * B_QK_PAD
        a = xq[:, lo:mid]
        c = xq[:, mid:hi]
        r = lax.rsqrt((sumsq(a) + sumsq(c, rope_lanes)) * (1.0 / B_QK) + EPS)
        bq_ref[:, lo:mid] = (a * r * gq_ref[:, :HEAD_DIM] * b_scale).astype(_BF16)
        bq_ref[:, mid:hi] = (rope(c * r * gq_ref[:, HEAD_DIM:]) * b_scale).astype(_BF16)
        kn = xkv[:, lo:mid]
        r = lax.rsqrt((sumsq(kn) + ss_kr) * (1.0 / B_QK) + EPS)
        bk_ref[:, lo:mid] = (kn * r * gkn_ref[...]).astype(_BF16)
        bk_ref[:, mid:hi] = (k_rope * r).astype(_BF16)
        bv_ref[:, hd * B_V:(hd + 1) * B_V] = xkv[:, mid:hi].astype(_BF16)

    for i, y in enumerate(heads(xa, gaq_ref[...], A_HEADS, scale)):
        aq_ref[:, i * HEAD_DIM:(i + 1) * HEAD_DIM] = y.astype(_BF16)
    for i, y in enumerate(heads(xakv[:, :A_KV_WIDTH], gak_ref[...], A_KV_HEADS)):
        akv_ref[:, i * HEAD_DIM:(i + 1) * HEAD_DIM] = y.astype(_BF16)
    akv_ref[:, A_KV_WIDTH:] = xakv[:, A_KV_WIDTH:].astype(_BF16)
    for i, y in enumerate(heads(xc[:, :C_WIDTH], gcq_ref[...], C_HEADS, scale)):
        cqkv_ref[:, i * HEAD_DIM:(i + 1) * HEAD_DIM] = y.astype(_BF16)
    for i, y in enumerate(heads(xc[:, C_WIDTH:2 * C_WIDTH], gck_ref[...], C_HEADS)):
        cqkv_ref[:, C_WIDTH + i * HEAD_DIM:C_WIDTH + (i + 1) * HEAD_DIM] = y.astype(_BF16)
    cqkv_ref[:, 2 * C_WIDTH:] = xc[:, 2 * C_WIDTH:].astype(_BF16)


def _in_proj(h, lw, tables, layer, seq, tm):
    t = h.shape[0]
    row = lambda i: (i, 0)
    pos = lambda i: (i % (seq // tm), 0)
    vec = lambda n: pl.BlockSpec((None, 1, n), lambda i: (layer, 0, 0))
    mat = lambda r, c: _resident((None, r, c), lambda i: (layer, 0, 0))
    widths = (A_WIDTH, 2 * A_KV_WIDTH, MIX_WIDTH, 3 * C_WIDTH, B_HEADS * B_QK_PAD, B_HEADS * B_QK_PAD,
              B_WIDTH)
    return pl.pallas_call(
        _in_proj_kernel,
        grid=(t // tm,),
        in_specs=[
            pl.BlockSpec((tm, D_MODEL), row),
            vec(D_MODEL),
            mat(D_MODEL, IN_LO_WIDTH), mat(D_MODEL, IN_HI_WIDTH), mat(D_MODEL, 2 * B_ROPE),
            vec(HEAD_DIM), vec(HEAD_DIM), vec(B_Q_LORA), vec(B_KV_LORA), vec(HEAD_DIM), vec(HEAD_DIM),
            mat(B_Q_LORA, B_HEADS * B_QK_PAD), mat(B_KV_LORA, B_HEADS * (B_NOPE + B_V)),
            vec(B_QK_PAD), vec(HEAD_DIM), vec(2 * B_ROPE),
            pl.BlockSpec((tm, 2 * B_ROPE), pos),
            pl.BlockSpec((tm, 2 * B_ROPE), pos),
        ],
        out_specs=[pl.BlockSpec((tm, w), row) for w in widths],
        out_shape=[jax.ShapeDtypeStruct((t, w), _BF16) for w in widths],
        compiler_params=_params(1),
        name="in_proj",
    )(h, lw['norm_in'], lw['w_in_lo'], lw['w_in_hi'], lw['w_in_kr'], lw['a_q_norm'], lw['a_k_norm'],
      lw['b_cq_norm'], lw['b_ckv_norm'], lw['c_q_norm'], lw['c_k_norm'], lw['b_w_uq'], lw['b_w_ukv'], lw['b_q_gain'], lw['b_k_gain_nope'],
      lw['b_k_gain_rope'], tables['cos2'], tables['sin2'])


def _fill_v_ext(vext_ref, v_ref):
    heads, rows, ext = vext_ref.shape
    width = v_ref.shape[1] // heads
    for h in range(heads):
        vext_ref[h, :, :width] = v_ref[:, h * width:(h + 1) * width]
        vext_ref[h, :, width:] = jnp.ones((rows, ext - width), vext_ref.dtype)


def _pipelined(n, logits, finish, depth):
    queue = [logits(i) for i in range(min(depth, n))]
    for i in range(n):
        s = queue.pop(0)
        if i + depth < n:
            queue.append(logits(i + depth))
        finish(i, s)


def _mla_attn_kernel(q_ref, k_ref, v_ref, g_ref, o_ref, vext_ref, *, tq):
    heads = vext_ref.shape[0]
    n_chunks = q_ref.shape[0] // tq
    _fill_v_ext(vext_ref, v_ref)

    def logits(i):
        h, c = divmod(i, n_chunks)
        qk_cols = slice(h * B_QK_PAD, (h + 1) * B_QK_PAD)
        return _dot_nt(q_ref[c * tq:(c + 1) * tq, qk_cols], k_ref[:, qk_cols])

    def finish(i, s):
        h, c = divmod(i, n_chunks)
        e = jnp.exp2(s - jnp.max(s, axis=-1, keepdims=True)).astype(_BF16)
        acc = _dot(e, vext_ref[h])
        rows = slice(c * tq, (c + 1) * tq)
        cols = slice(h * B_V, (h + 1) * B_V)
        o = acc[:, :B_V] / acc[:, B_V:]
        o_ref[rows, cols] = (o * g_ref[rows, cols].astype(_F32)).astype(_BF16)

    _pipelined(heads * n_chunks, logits, finish, depth=1)


def _mla_attn(q, k, v, gates, batch, seq, tq):
    t = q.shape[0]
    heads = MLA_HEADS_PER_STEP
    gate_blk = A_WIDTH // (heads * B_V)
    return pl.pallas_call(
        functools.partial(_mla_attn_kernel, tq=tq),
        grid=(batch, B_HEADS // heads),
        in_specs=[
            pl.BlockSpec((seq, heads * B_QK_PAD), lambda b, hg: (b, hg)),
            pl.BlockSpec((seq, heads * B_QK_PAD), lambda b, hg: (b, hg)),
            pl.BlockSpec((seq, heads * B_V), lambda b, hg: (b, hg)),
            pl.BlockSpec((seq, heads * B_V), lambda b, hg: (b, gate_blk + hg)),
        ],
        out_specs=pl.BlockSpec((seq, heads * B_V), lambda b, hg: (b, hg)),
        out_shape=jax.ShapeDtypeStruct((t, B_WIDTH), _BF16),
        scratch_shapes=[pltpu.VMEM((heads, seq, 2 * B_V), _BF16)],
        compiler_params=_params(2),
        name="mla_attn",
    )(q, k, v, gates)


def _win_attn_kernel(sink_ref, slope_ref, q_ref, k_ref, v_ref, g_ref, o_ref, bias_ref, vext_ref):
    seq = q_ref.shape[0]
    nb = seq // A_BLOCK
    span = 3 * A_BLOCK
    _fill_v_ext(vext_ref, v_ref)

    @pl.when(pl.program_id(0) == 0)
    def _():
        qi = lax.broadcasted_iota(jnp.int32, (A_BLOCK, span), 0)
        si = lax.broadcasted_iota(jnp.int32, (A_BLOCK, span), 1)
        for variant in range(3):
            dist = jnp.abs(si - qi - variant * A_BLOCK)
            valid = dist <= A_WINDOW
            for head in range(A_HEADS):
                slope = slope_ref[head] * LOG2E
                bias_ref[head // A_GROUP, variant, (head % A_GROUP) * A_BLOCK:(head % A_GROUP + 1) * A_BLOCK, :] = (
                    jnp.where(valid, -(slope * dist.astype(_F32)), NEG_INF))

    sinks = [sink_ref[head] * LOG2E for head in range(A_HEADS)]

    def key_start(j):
        return min(max((j - 1) * A_BLOCK, 0), seq - span)

    def logits(i):
        kvh, j = divmod(i, nb)
        qrows = slice(j * A_BLOCK, (j + 1) * A_BLOCK)
        qs = jnp.concatenate(
            [q_ref[qrows, (kvh * A_GROUP + g) * HEAD_DIM:(kvh * A_GROUP + g + 1) * HEAD_DIM]
             for g in range(A_GROUP)], axis=0)
        return _dot_nt(qs, k_ref[key_start(j):key_start(j) + span, kvh * HEAD_DIM:(kvh + 1) * HEAD_DIM])

    def finish(i, s):
        kvh, j = divmod(i, nb)
        start = key_start(j)
        s = s + bias_ref[kvh, (j * A_BLOCK - start) // A_BLOCK]
        probs, sink_terms = [], []
        for g in range(A_GROUP):
            sink = sinks[kvh * A_GROUP + g]
            sg = s[g * A_BLOCK:(g + 1) * A_BLOCK]
            m = jnp.maximum(jnp.max(sg, axis=-1, keepdims=True), sink)
            probs.append(jnp.exp2(sg - m).astype(_BF16))
            sink_terms.append(jnp.exp2(sink - m))
        acc = _dot(jnp.concatenate(probs, axis=0), vext_ref[kvh, start:start + span, :])
        qrows = slice(j * A_BLOCK, (j + 1) * A_BLOCK)
        for g in range(A_GROUP):
            ag = acc[g * A_BLOCK:(g + 1) * A_BLOCK]
            o = ag[:, :HEAD_DIM] / (ag[:, HEAD_DIM:] + sink_terms[g])
            cols = slice((kvh * A_GROUP + g) * HEAD_DIM, (kvh * A_GROUP + g + 1) * HEAD_DIM)
            o_ref[qrows, cols] = (o * g_ref[qrows, cols].astype(_F32)).astype(_BF16)

    _pipelined(A_KV_HEADS * nb, logits, finish, depth=2)


def _win_attn(aq, akv, gates, sink, batch, seq):
    t = aq.shape[0]
    assert seq >= 3 * A_BLOCK
    slopes = jnp.exp2(-8.0 * jnp.arange(1, A_HEADS + 1, dtype=_F32) / A_HEADS)
    return pl.pallas_call(
        _win_attn_kernel,
        grid=(batch,),
        in_specs=[
            pl.BlockSpec(memory_space=pltpu.SMEM),
            pl.BlockSpec(memory_space=pltpu.SMEM),
            pl.BlockSpec((seq, A_WIDTH), lambda b: (b, 0)),
            pl.BlockSpec((seq, A_KV_WIDTH), lambda b: (b, 0)),
            pl.BlockSpec((seq, A_KV_WIDTH), lambda b: (b, 1)),
            pl.BlockSpec((seq, A_WIDTH), lambda b: (b, 0)),
        ],
        out_specs=pl.BlockSpec((seq, A_WIDTH), lambda b: (b, 0)),
        out_shape=jax.ShapeDtypeStruct((t, A_WIDTH), _BF16),
        scratch_shapes=[
            pltpu.VMEM((A_KV_HEADS, 3, A_GROUP * A_BLOCK, 3 * A_BLOCK), _F32),
            pltpu.VMEM((A_KV_HEADS, seq, 2 * HEAD_DIM), _BF16),
        ],
        compiler_params=_params(1),
        name="win_attn",
    )(sink, slopes, aq, akv, akv, gates)


def _na_tiles(rows):
    wr = min(NA_ROWS_MAX, rows)
    assert rows % NA_TILE_ROWS == 0 and rows >= NA_SPAN_ROWS
    starts, tile_pattern, patterns = [], [], []
    for t in range(rows // NA_TILE_ROWS):
        ws = min(max(NA_TILE_ROWS * t - wr // 2, 0), rows - NA_SPAN_ROWS)
        pattern = []
        for i in range(NA_TILE_ROWS):
            r = NA_TILE_ROWS * t + i
            rs = min(max(r - wr // 2, 0), rows - wr)
            assert ws <= rs and rs + wr <= ws + NA_SPAN_ROWS
            pattern.append((rs - ws, r - ws))
        pattern = tuple(pattern)
        if pattern not in patterns:
            patterns.append(pattern)
        starts.append(ws)
        tile_pattern.append(patterns.index(pattern))
    return starts, tile_pattern, patterns


def _na_pair_entry(first, q_off, pair, wr):
    n_dr = 2 * NA_ROWS_MAX - 1
    j0 = 2 * pair
    valid0 = first <= j0 < first + wr
    valid1 = first <= j0 + 1 < first + wr
    dr0 = j0 - q_off + NA_ROWS_MAX - 1
    if valid0 and valid1:
        return dr0
    if valid1:
        return (n_dr - 1) + dr0 + 1
    if valid0:
        return (n_dr - 1) + n_dr + dr0
    return (n_dr - 1) + 2 * n_dr


def _na_attn_kernel(q_ref, k_ref, v_ref, g_ref, bias_ref, o_ref, vext_ref):
    seq = q_ref.shape[0]
    rows = seq // GRID_W
    starts, tile_pattern, patterns = _na_tiles(rows)
    n_tiles = len(starts)
    tq = NA_TILE_ROWS * GRID_W
    span = NA_SPAN_ROWS * GRID_W
    wr = min(NA_ROWS_MAX, rows)
    _fill_v_ext(vext_ref, v_ref)

    heads = vext_ref.shape[0]

    def keys(t):
        return slice(starts[t] * GRID_W, starts[t] * GRID_W + span)

    def logits(i):
        h, t = divmod(i, n_tiles)
        cols = slice(h * HEAD_DIM, (h + 1) * HEAD_DIM)
        return _dot_nt(q_ref[t * tq:(t + 1) * tq, cols], k_ref[keys(t), cols])

    def add_bias(h, t, s):
        out_rows = []
        for i, (first, q_off) in enumerate(patterns[tile_pattern[t]]):
            out_rows.append(jnp.concatenate(
                [s[i * GRID_W:(i + 1) * GRID_W, m * 2 * GRID_W:(m + 1) * 2 * GRID_W]
                 + bias_ref[h, _na_pair_entry(first, q_off, m, wr)]
                 for m in range(NA_SPAN_ROWS // 2)], axis=1))
        return jnp.concatenate(out_rows, axis=0)

    def finish(i, s):
        h, t = divmod(i, n_tiles)
        s = add_bias(h, t, s)
        e = jnp.exp2(s - jnp.max(s, axis=-1, keepdims=True)).astype(_BF16)
        acc = _dot(e, vext_ref[h, keys(t), :])
        rows = slice(t * tq, (t + 1) * tq)
        cols = slice(h * HEAD_DIM, (h + 1) * HEAD_DIM)
        o = acc[:, :HEAD_DIM] / acc[:, HEAD_DIM:]
        o_ref[rows, cols] = (o * g_ref[rows, cols].astype(_F32)).astype(_BF16)

    _pipelined(heads * n_tiles, logits, finish, depth=2)


def _na_attn(cqkv, gates, bias, batch, seq, layer):
    t = cqkv.shape[0]
    heads = NA_HEADS_PER_STEP
    width = heads * HEAD_DIM
    groups = C_HEADS // heads
    gate_blk = (A_WIDTH + B_WIDTH) // width
    blk = lambda off: pl.BlockSpec((seq, width), lambda hg, b: (b, off + hg))
    return pl.pallas_call(
        _na_attn_kernel,
        grid=(groups, batch),
        in_specs=[
            blk(0), blk(groups), blk(2 * groups), blk(gate_blk),
            pl.BlockSpec((None, heads) + bias.shape[2:], lambda hg, b: (layer, hg, 0, 0, 0)),
        ],
        out_specs=pl.BlockSpec((seq, width), lambda hg, b: (b, hg)),
        out_shape=jax.ShapeDtypeStruct((t, C_WIDTH), _BF16),
        scratch_shapes=[pltpu.VMEM((heads, seq, 2 * HEAD_DIM), _BF16)],
        compiler_params=_params(2),
        name="na_attn",
    )(cqkv, cqkv, cqkv, gates, bias)


def _out_ple_kernel(ya_ref, yb_ref, yc_ref, h_ref, p_ref, wo_ref, gple_ref, wg_ref, wp_ref, gpost_ref,
                    o_ref):
    tm = h_ref.shape[0]
    halves = [slice(i * (tm // 2), (i + 1) * (tm // 2)) for i in range(2)]
    pp = [_rms(_dot(p_ref[rows, :].astype(_BF16), wp_ref[...]), gpost_ref[...]) for rows in halves]
    h1 = []
    for rows in halves:
        acc = _dot(ya_ref[rows, :], wo_ref[:A_WIDTH, :])
        acc += _dot(yb_ref[rows, :], wo_ref[A_WIDTH:A_WIDTH + B_WIDTH, :])
        acc += _dot(yc_ref[rows, :], wo_ref[A_WIDTH + B_WIDTH:, :])
        h1.append(h_ref[rows, :] + acc)
    logit = [_dot(_rms(x, gple_ref[...]).astype(_BF16), wg_ref[...]) for x in h1]
    for i, rows in enumerate(halves):
        o_ref[rows, :] = h1[i] + jax.nn.sigmoid(logit[i]) * pp[i]


def _out_ple(ya, yb, yc, h, p, lw, layer, tm):
    t = h.shape[0]
    row = lambda i: (i, 0)
    vec = lambda n: pl.BlockSpec((None, 1, n), lambda i: (layer, 0, 0))
    mat = lambda r, c: _resident((None, r, c), lambda i: (layer, 0, 0))
    return pl.pallas_call(
        _out_ple_kernel,
        grid=(t // tm,),
        in_specs=[
            pl.BlockSpec((tm, A_WIDTH), row),
            pl.BlockSpec((tm, B_WIDTH), row),
            pl.BlockSpec((tm, C_WIDTH), row),
            pl.BlockSpec((tm, D_MODEL), row),
            pl.BlockSpec((None, tm, PLE_DIM), lambda i: (layer, i, 0)),
            mat(MIX_WIDTH, D_MODEL), vec(D_MODEL), mat(D_MODEL, D_MODEL), mat(PLE_DIM, D_MODEL),
            vec(D_MODEL),
        ],
        out_specs=pl.BlockSpec((tm, D_MODEL), row),
        out_shape=jax.ShapeDtypeStruct((t, D_MODEL), _F32),
        compiler_params=_params(1),
        name="out_ple",
    )(ya, yb, yc, h, p, lw['w_out'], lw['ple_norm'], lw['w_ple_gate'], lw['w_ple_proj'],
      lw['ple_post_norm'])


def _swap_halves(x):
    half = x.shape[-1] // 2
    return jnp.concatenate([x[..., half:], x[..., :half]], axis=-1)


def _split_w_in_kernel(w_ref, lo_ref, hi_ref, kr_ref):
    half = B_ROPE // 2
    lo_ref[...] = w_ref[:, :IN_LO_WIDTH].astype(_BF16)
    hi_ref[...] = w_ref[:, IN_HI_START:].astype(_BF16)
    kr_ref[:, :B_ROPE] = w_ref[:, IN_LO_WIDTH:IN_HI_START].astype(_BF16)
    kr_ref[:, B_ROPE:B_ROPE + half] = w_ref[:, IN_LO_WIDTH + half:IN_HI_START].astype(_BF16)
    kr_ref[:, B_ROPE + half:] = w_ref[:, IN_LO_WIDTH:IN_LO_WIDTH + half].astype(_BF16)


def _split_w_in(w_in):
    depth, d_in, width = w_in.shape
    assert width == IN_HI_START + IN_HI_WIDTH
    rows = W_IN_SPLIT_ROWS
    out_widths = (IN_LO_WIDTH, IN_HI_WIDTH, 2 * B_ROPE)
    return pl.pallas_call(
        _split_w_in_kernel,
        grid=(depth, d_in // rows),
        in_specs=[pl.BlockSpec((None, rows, width), lambda l, i: (l, i, 0))],
        out_specs=[pl.BlockSpec((None, rows, w), lambda l, i: (l, i, 0)) for w in out_widths],
        out_shape=[jax.ShapeDtypeStruct((depth, d_in, w), _BF16) for w in out_widths],
        compiler_params=_params(2),
        name="split_w_in",
    )(w_in)


def _prepare_weights(W):
    depth = W['w_in'].shape[0]
    w_in_lo, w_in_hi, w_in_kr = _split_w_in(W['w_in'])

    wq = W['b_w_uq'].reshape(depth, B_Q_LORA, B_HEADS, B_QK)
    wq = jnp.concatenate([wq, _swap_halves(wq[..., B_NOPE:])], axis=-1)
    wq = wq.reshape(depth, B_Q_LORA, B_HEADS * B_QK_PAD).astype(_BF16)
    gq = W['b_q_norm']
    gk = W['b_k_norm']
    row = lambda g: g[:, None, :].astype(_F32)
    return {
        'norm_in': row(W['norm_in']),
        'w_in_lo': w_in_lo, 'w_in_hi': w_in_hi, 'w_in_kr': w_in_kr,
        'a_q_norm': row(W['a_q_norm']), 'a_k_norm': row(W['a_k_norm']), 'a_sink': W['a_sink'].astype(_F32),
        'b_cq_norm': row(W['b_cq_norm']), 'b_ckv_norm': row(W['b_ckv_norm']),
        'b_w_uq': wq, 'b_w_ukv': W['b_w_ukv'].astype(_BF16),
        'b_q_gain': row(jnp.concatenate([gq, _swap_halves(gq[:, B_NOPE:])], axis=-1)),
        'b_k_gain_nope': row(gk[:, :B_NOPE]),
        'b_k_gain_rope': row(jnp.concatenate([gk[:, B_NOPE:], _swap_halves(gk[:, B_NOPE:])], axis=-1)),
        'c_q_norm': row(W['c_q_norm']), 'c_k_norm': row(W['c_k_norm']),
        'w_out': W['w_out'].astype(_BF16), 'ple_norm': row(W['ple_norm']),
        'w_ple_gate': W['w_ple_gate'].astype(_BF16), 'w_ple_proj': W['w_ple_proj'].astype(_BF16),
        'ple_post_norm': row(W['ple_post_norm']),
    }


def _rope_tables(seq):
    half = B_ROPE // 2
    inv = ROPE_THETA ** (-jnp.arange(half, dtype=_F32) / half)
    ang = jnp.arange(seq).astype(_F32)[:, None] * inv[None, :]
    cos, sin = jnp.cos(ang), jnp.sin(ang)
    zeros = jnp.zeros((seq, B_ROPE), _F32)
    return {'cos2': jnp.concatenate([cos, cos, zeros], axis=-1),
            'sin2': jnp.concatenate([-sin, sin, zeros], axis=-1)}


def _na_bias_table(rpb):
    c = np.arange(GRID_W)
    cs = np.clip(c - NA_COLS // 2, 0, GRID_W - NA_COLS)
    col_valid = (c[None, :] >= cs[:, None]) & (c[None, :] < cs[:, None] + NA_COLS)
    dc = np.clip(c[None, :] - c[:, None] + NA_COLS - 1, 0, 2 * NA_COLS - 2)
    by_dc = jnp.where(col_valid, rpb.astype(_F32)[:, :, :, dc] * LOG2E, NEG_INF)
    masked = jnp.full_like(by_dc, NEG_INF)
    return jnp.concatenate([
        jnp.concatenate([by_dc[:, :, :-1], by_dc[:, :, 1:]], axis=-1),
        jnp.concatenate([masked, by_dc], axis=-1),
        jnp.concatenate([by_dc, masked], axis=-1),
        jnp.concatenate([masked[:, :, :1], masked[:, :, :1]], axis=-1),
    ], axis=2)


def _trunk(x, p, lw, tables, na_bias):
    batch, seq, _ = x.shape
    depth = p.shape[0]
    t = batch * seq
    h = x.reshape(t, D_MODEL)
    p = p.reshape(depth, t, PLE_DIM)
    for layer in range(depth):
        aq, akv, gates, cqkv, bq, bk, bv = _in_proj(h, lw, tables, layer, seq, IN_PROJ_ROWS)
        ya = _win_attn(aq, akv, gates, lw['a_sink'][layer], batch, seq)
        yb = _mla_attn(bq, bk, bv, gates, batch, seq, MLA_QUERY_ROWS)
        yc = _na_attn(cqkv, gates, na_bias, batch, seq, layer)
        h = _out_ple(ya, yb, yc, h, p, lw, layer, OUT_PLE_ROWS)
    return h.reshape(batch, seq, D_MODEL)


def kernel(x_prompt, x_sample, p_prompt, p_sample, norm_in, w_in, a_q_norm, a_k_norm, a_sink, b_cq_norm, b_ckv_norm, b_w_uq, b_w_ukv, b_q_norm, b_k_norm, c_q_norm, c_k_norm, c_rpb, w_out, ple_norm, w_ple_gate, w_ple_proj, ple_post_norm):
    lw = _prepare_weights({
        'norm_in': norm_in, 'w_in': w_in, 'a_q_norm': a_q_norm, 'a_k_norm': a_k_norm, 'a_sink': a_sink,
        'b_cq_norm': b_cq_norm, 'b_ckv_norm': b_ckv_norm, 'b_w_uq': b_w_uq, 'b_w_ukv': b_w_ukv,
        'b_q_norm': b_q_norm, 'b_k_norm': b_k_norm, 'c_q_norm': c_q_norm, 'c_k_norm': c_k_norm,
        'w_out': w_out, 'ple_norm': ple_norm, 'w_ple_gate': w_ple_gate, 'w_ple_proj': w_ple_proj,
        'ple_post_norm': ple_post_norm,
    })
    na_bias = _na_bias_table(c_rpb)
    rope = {}
    outs = []
    for x, p in ((x_prompt, p_prompt), (x_sample, p_sample)):
        seq = x.shape[1]
        if seq not in rope:
            rope[seq] = _rope_tables(seq)
        outs.append(_trunk(x, p, lw, rope[seq], na_bias))
    return tuple(outs)
```

```python
import functools

import jax
import jax.numpy as jnp
import numpy as np
from jax import lax
from jax.experimental import pallas as pl
from jax.experimental.pallas import tpu as pltpu

D_MODEL = 2048
PLE_DIM = 256
HEAD_DIM = 128
A_HEADS = 6
A_KV_HEADS = 2
A_GROUP = A_HEADS // A_KV_HEADS
A_WINDOW = 128
A_BLOCK = 128
B_HEADS = 6
B_Q_LORA = 512
B_KV_LORA = 512
B_NOPE = 128
B_ROPE = 64
B_V = 128
B_QK = B_NOPE + B_ROPE
ROPE_THETA = 10000.0
C_HEADS = 4
GRID_W = 64
NA_ROWS_MAX = 8
NA_COLS = 16
A_WIDTH = A_HEADS * HEAD_DIM
A_KV_WIDTH = A_KV_HEADS * HEAD_DIM
B_WIDTH = B_HEADS * B_V
C_WIDTH = C_HEADS * HEAD_DIM
MIX_WIDTH = A_WIDTH + B_WIDTH + C_WIDTH
EPS = 1e-6
NEG_INF = -1e30
LOG2E = 1.4426950408889634

B_QK_PAD = 2 * HEAD_DIM
IN_LO_WIDTH = 2 * A_WIDTH + 2 * A_KV_WIDTH + B_Q_LORA + B_KV_LORA
IN_HI_START = IN_LO_WIDTH + B_ROPE
IN_HI_WIDTH = B_WIDTH + 4 * C_WIDTH
LO_A_Q = 0
LO_A_KV = LO_A_Q + A_WIDTH
LO_A_Z = LO_A_KV + 2 * A_KV_WIDTH
LO_B_LAT = LO_A_Z + A_WIDTH
HI_B_Z = 0
HI_C_QKV = HI_B_Z + B_WIDTH
HI_C_Z = HI_C_QKV + 3 * C_WIDTH

VMEM_LIMIT_BYTES = 56 * 1024 * 1024
IN_PROJ_ROWS = 256
OUT_PLE_ROWS = 512
MLA_QUERY_ROWS = 256
MLA_HEADS_PER_STEP = 3
NA_HEADS_PER_STEP = 2
NA_TILE_ROWS = 4
NA_SPAN_ROWS = 12

_BF16 = jnp.bfloat16
_F32 = jnp.float32


def _params(n_grid_dims):
    return pltpu.CompilerParams(
        dimension_semantics=("arbitrary",) * n_grid_dims,
        vmem_limit_bytes=VMEM_LIMIT_BYTES)


def _resident(shape, index_map):
    return pl.BlockSpec(shape, index_map, pipeline_mode=pl.Buffered(1))


def _rms(x, g):
    return x * lax.rsqrt(jnp.mean(x * x, axis=-1, keepdims=True) + EPS) * g


def _dot(a, b):
    return jnp.dot(a, b, preferred_element_type=_F32)


def _dot_nt(a, b):
    return lax.dot_general(a, b, (((1,), (1,)), ((), ())), preferred_element_type=_F32)


def _in_proj_kernel(h_ref, gin_ref, wlo_ref, whi_ref, wkr_ref, gaq_ref, gak_ref, gbq_ref, gbkv_ref,
                    gcq_ref, gck_ref,
                    wq_ref, wkv_ref, gq_ref, gkn_ref, gkr_ref, cos_ref, sin_ref,
                    aq_ref, akv_ref, gates_ref, cqkv_ref, bq_ref, bk_ref, bv_ref):
    u = _rms(h_ref[...], gin_ref[...]).astype(_BF16)
    scale = HEAD_DIM ** -0.5 * LOG2E
    b_scale = B_QK ** -0.5 * LOG2E

    def seg(w_ref, start, width):
        return _dot(u, w_ref[:, start:start + width])

    def heads(x, g, n, s=None):
        outs = []
        for i in range(n):
            y = _rms(x[:, i * HEAD_DIM:(i + 1) * HEAD_DIM], g)
            outs.append(y if s is None else y * s)
        return outs

    x = seg(wlo_ref, LO_B_LAT, B_Q_LORA + B_KV_LORA)
    kr2 = _dot(u, wkr_ref[...])
    cqn = _rms(x[:, :B_Q_LORA], gbq_ref[...]).astype(_BF16)
    ckvn = _rms(x[:, B_Q_LORA:], gbkv_ref[...]).astype(_BF16)

    out_col = 0
    for w_ref, start, width in ((wlo_ref, LO_A_Z, A_WIDTH), (whi_ref, HI_B_Z, B_WIDTH),
                                (whi_ref, HI_C_Z, C_WIDTH)):
        z = seg(w_ref, start, width)
        gates_ref[:, out_col:out_col + width] = (z * jax.nn.sigmoid(z)).astype(_BF16)
        out_col += width

    xq = _dot(cqn, wq_ref[...])
    xkv = _dot(ckvn, wkv_ref[...])

    xa = seg(wlo_ref, LO_A_Q, A_WIDTH)
    xakv = seg(wlo_ref, LO_A_KV, 2 * A_KV_WIDTH)
    xc = seg(whi_ref, HI_C_QKV, 3 * C_WIDTH)

    cos2 = cos_ref[...]
    sin2 = sin_ref[...]
    rope_lanes = lax.broadcasted_iota(jnp.int32, (1, HEAD_DIM), 1) < B_ROPE

    def rope(c):
        return c * cos2 + pltpu.roll(c, B_ROPE, 1) * sin2

    def sumsq(y, lanes=None):
        y2 = y * y
        return jnp.sum(y2 if lanes is None else jnp.where(lanes, y2, 0.0), axis=-1, keepdims=True)

    ss_kr = sumsq(kr2, rope_lanes)
    k_rope = rope(kr2 * gkr_ref[...])
    for hd in range(B_HEADS):
        lo, mid, hi = hd * B_QK_PAD, hd * B_QK_PAD + HEAD_DIM, (hd + 1) * B_QK_PAD
        a = xq[:, lo:mid]
        c = xq[:, mid:hi]
        r = lax.rsqrt((sumsq(a) + sumsq(c, rope_lanes)) * (1.0 / B_QK) + EPS)
        bq_ref[:, lo:mid] = (a * r * gq_ref[:, :HEAD_DIM] * b_scale).astype(_BF16)
        bq_ref[:, mid:hi] = (rope(c * r * gq_ref[:, HEAD_DIM:]) * b_scale).astype(_BF16)
        kn = xkv[:, lo:mid]
        r = lax.rsqrt((sumsq(kn) + ss_kr) * (1.0 / B_QK) + EPS)
        bk_ref[:, lo:mid] = (kn * r * gkn_ref[...]).astype(_BF16)
        bk_ref[:, mid:hi] = (k_rope * r).astype(_BF16)
        bv_ref[:, hd * B_V:(hd + 1) * B_V] = xkv[:, mid:hi].astype(_BF16)

    for i, y in enumerate(heads(xa, gaq_ref[...], A_HEADS, scale)):
        aq_ref[:, i * HEAD_DIM:(i + 1) * HEAD_DIM] = y.astype(_BF16)
    for i, y in enumerate(heads(xakv[:, :A_KV_WIDTH], gak_ref[...], A_KV_HEADS)):
        akv_ref[:, i * HEAD_DIM:(i + 1) * HEAD_DIM] = y.astype(_BF16)
    akv_ref[:, A_KV_WIDTH:] = xakv[:, A_KV_WIDTH:].astype(_BF16)
    for i, y in enumerate(heads(xc[:, :C_WIDTH], gcq_ref[...], C_HEADS, scale)):
        cqkv_ref[:, i * HEAD_DIM:(i + 1) * HEAD_DIM] = y.astype(_BF16)
    for i, y in enumerate(heads(xc[:, C_WIDTH:2 * C_WIDTH], gck_ref[...], C_HEADS)):
        cqkv_ref[:, C_WIDTH + i * HEAD_DIM:C_WIDTH + (i + 1) * HEAD_DIM] = y.astype(_BF16)
    cqkv_ref[:, 2 * C_WIDTH:] = xc[:, 2 * C_WIDTH:].astype(_BF16)


def _in_proj(h, lw, tables, layer, seq, tm):
    t = h.shape[0]
    row = lambda i: (i, 0)
    pos = lambda i: (i % (seq // tm), 0)
    vec = lambda n: pl.BlockSpec((None, 1, n), lambda i: (layer, 0, 0))
    mat = lambda r, c: _resident((None, r, c), lambda i: (layer, 0, 0))
    widths = (A_WIDTH, 2 * A_KV_WIDTH, MIX_WIDTH, 3 * C_WIDTH, B_HEADS * B_QK_PAD, B_HEADS * B_QK_PAD,
              B_WIDTH)
    return pl.pallas_call(
        _in_proj_kernel,
        grid=(t // tm,),
        in_specs=[
            pl.BlockSpec((tm, D_MODEL), row),
            vec(D_MODEL),
            mat(D_MODEL, IN_LO_WIDTH), mat(D_MODEL, IN_HI_WIDTH), mat(D_MODEL, 2 * B_ROPE),
            vec(HEAD_DIM), vec(HEAD_DIM), vec(B_Q_LORA), vec(B_KV_LORA), vec(HEAD_DIM), vec(HEAD_DIM),
            mat(B_Q_LORA, B_HEADS * B_QK_PAD), mat(B_KV_LORA, B_HEADS * (B_NOPE + B_V)),
            vec(B_QK_PAD), vec(HEAD_DIM), vec(2 * B_ROPE),
            pl.BlockSpec((tm, 2 * B_ROPE), pos),
            pl.BlockSpec((tm, 2 * B_ROPE), pos),
        ],
        out_specs=[pl.BlockSpec((tm, w), row) for w in widths],
        out_shape=[jax.ShapeDtypeStruct((t, w), _BF16) for w in widths],
        compiler_params=_params(1),
        name="in_proj",
    )(h, lw['norm_in'], lw['w_in_lo'], lw['w_in_hi'], lw['w_in_kr'], lw['a_q_norm'], lw['a_k_norm'],
      lw['b_cq_norm'], lw['b_ckv_norm'], lw['c_q_norm'], lw['c_k_norm'], lw['b_w_uq'], lw['b_w_ukv'], lw['b_q_gain'], lw['b_k_gain_nope'],
      lw['b_k_gain_rope'], tables['cos2'], tables['sin2'])


def _fill_v_ext(vext_ref, v_ref):
    heads, rows, ext = vext_ref.shape
    width = v_ref.shape[1] // heads
    for h in range(heads):
        vext_ref[h, :, :width] = v_ref[:, h * width:(h + 1) * width]
        vext_ref[h, :, width:] = jnp.ones((rows, ext - width), vext_ref.dtype)


def _pipelined(n, logits, finish, depth):
    queue = [logits(i) for i in range(min(depth, n))]
    for i in range(n):
        s = queue.pop(0)
        if i + depth < n:
            queue.append(logits(i + depth))
        finish(i, s)


def _mla_attn_kernel(q_ref, k_ref, v_ref, g_ref, o_ref, vext_ref, *, tq):
    heads = vext_ref.shape[0]
    n_chunks = q_ref.shape[0] // tq
    _fill_v_ext(vext_ref, v_ref)

    def logits(i):
        h, c = divmod(i, n_chunks)
        qk_cols = slice(h * B_QK_PAD, (h + 1) * B_QK_PAD)
        return _dot_nt(q_ref[c * tq:(c + 1) * tq, qk_cols], k_ref[:, qk_cols])

    def finish(i, s):
        h, c = divmod(i, n_chunks)
        e = jnp.exp2(s - jnp.max(s, axis=-1, keepdims=True)).astype(_BF16)
        acc = _dot(e, vext_ref[h])
        rows = slice(c * tq, (c + 1) * tq)
        cols = slice(h * B_V, (h + 1) * B_V)
        o = acc[:, :B_V] / acc[:, B_V:]
        o_ref[rows, cols] = (o * g_ref[rows, cols].astype(_F32)).astype(_BF16)

    _pipelined(heads * n_chunks, logits, finish, depth=1)


def _mla_attn(q, k, v, gates, batch, seq, tq):
    t = q.shape[0]
    heads = MLA_HEADS_PER_STEP
    gate_blk = A_WIDTH // (heads * B_V)
    return pl.pallas_call(
        functools.partial(_mla_attn_kernel, tq=tq),
        grid=(batch, B_HEADS // heads),
        in_specs=[
            pl.BlockSpec((seq, heads * B_QK_PAD), lambda b, hg: (b, hg)),
            pl.BlockSpec((seq, heads * B_QK_PAD), lambda b, hg: (b, hg)),
            pl.BlockSpec((seq, heads * B_V), lambda b, hg: (b, hg)),
            pl.BlockSpec((seq, heads * B_V), lambda b, hg: (b, gate_blk + hg)),
        ],
        out_specs=pl.BlockSpec((seq, heads * B_V), lambda b, hg: (b, hg)),
        out_shape=jax.ShapeDtypeStruct((t, B_WIDTH), _BF16),
        scratch_shapes=[pltpu.VMEM((heads, seq, 2 * B_V), _BF16)],
        compiler_params=_params(2),
        name="mla_attn",
    )(q, k, v, gates)


def _win_attn_kernel(sink_ref, slope_ref, q_ref, k_ref, v_ref, g_ref, o_ref, bias_ref, vext_ref):
    seq = q_ref.shape[0]
    nb = seq // A_BLOCK
    span = 3 * A_BLOCK
    _fill_v_ext(vext_ref, v_ref)

    @pl.when(pl.program_id(0) == 0)
    def _():
        qi = lax.broadcasted_iota(jnp.int32, (A_BLOCK, span), 0)
        si = lax.broadcasted_iota(jnp.int32, (A_BLOCK, span), 1)
        for variant in range(3):
            dist = jnp.abs(si - qi - variant * A_BLOCK)
            valid = dist <= A_WINDOW
            for head in range(A_HEADS):
                slope = slope_ref[head] * LOG2E
                bias_ref[head // A_GROUP, variant, (head % A_GROUP) * A_BLOCK:(head % A_GROUP + 1) * A_BLOCK, :] = (
                    jnp.where(valid, -(slope * dist.astype(_F32)), NEG_INF))

    sinks = [sink_ref[head] * LOG2E for head in range(A_HEADS)]

    def key_start(j):
        return min(max((j - 1) * A_BLOCK, 0), seq - span)

    def logits(i):
        kvh, j = divmod(i, nb)
        qrows = slice(j * A_BLOCK, (j + 1) * A_BLOCK)
        qs = jnp.concatenate(
            [q_ref[qrows, (kvh * A_GROUP + g) * HEAD_DIM:(kvh * A_GROUP + g + 1) * HEAD_DIM]
             for g in range(A_GROUP)], axis=0)
        return _dot_nt(qs, k_ref[key_start(j):key_start(j) + span, kvh * HEAD_DIM:(kvh + 1) * HEAD_DIM])

    def finish(i, s):
        kvh, j = divmod(i, nb)
        start = key_start(j)
        s = s + bias_ref[kvh, (j * A_BLOCK - start) // A_BLOCK]
        probs, sink_terms = [], []
        for g in range(A_GROUP):
            sink = sinks[kvh * A_GROUP + g]
            sg = s[g * A_BLOCK:(g + 1) * A_BLOCK]
            m = jnp.maximum(jnp.max(sg, axis=-1, keepdims=True), sink)
            probs.append(jnp.exp2(sg - m).astype(_BF16))
            sink_terms.append(jnp.exp2(sink - m))
        acc = _dot(jnp.concatenate(probs, axis=0), vext_ref[kvh, start:start + span, :])
        qrows = slice(j * A_BLOCK, (j + 1) * A_BLOCK)
        for g in range(A_GROUP):
            ag = acc[g * A_BLOCK:(g + 1) * A_BLOCK]
            o = ag[:, :HEAD_DIM] / (ag[:, HEAD_DIM:] + sink_terms[g])
            cols = slice((kvh * A_GROUP + g) * HEAD_DIM, (kvh * A_GROUP + g + 1) * HEAD_DIM)
            o_ref[qrows, cols] = (o * g_ref[qrows, cols].astype(_F32)).astype(_BF16)

    _pipelined(A_KV_HEADS * nb, logits, finish, depth=2)


def _win_attn(aq, akv, gates, sink, batch, seq):
    t = aq.shape[0]
    assert seq >= 3 * A_BLOCK
    slopes = jnp.exp2(-8.0 * jnp.arange(1, A_HEADS + 1, dtype=_F32) / A_HEADS)
    return pl.pallas_call(
        _win_attn_kernel,
        grid=(batch,),
        in_specs=[
            pl.BlockSpec(memory_space=pltpu.SMEM),
            pl.BlockSpec(memory_space=pltpu.SMEM),
            pl.BlockSpec((seq, A_WIDTH), lambda b: (b, 0)),
            pl.BlockSpec((seq, A_KV_WIDTH), lambda b: (b, 0)),
            pl.BlockSpec((seq, A_KV_WIDTH), lambda b: (b, 1)),
            pl.BlockSpec((seq, A_WIDTH), lambda b: (b, 0)),
        ],
        out_specs=pl.BlockSpec((seq, A_WIDTH), lambda b: (b, 0)),
        out_shape=jax.ShapeDtypeStruct((t, A_WIDTH), _BF16),
        scratch_shapes=[
            pltpu.VMEM((A_KV_HEADS, 3, A_GROUP * A_BLOCK, 3 * A_BLOCK), _F32),
            pltpu.VMEM((A_KV_HEADS, seq, 2 * HEAD_DIM), _BF16),
        ],
        compiler_params=_params(1),
        name="win_attn",
    )(sink, slopes, aq, akv, akv, gates)


def _na_tiles(rows):
    wr = min(NA_ROWS_MAX, rows)
    assert rows % NA_TILE_ROWS == 0 and rows >= NA_SPAN_ROWS
    starts, tile_pattern, patterns = [], [], []
    for t in range(rows // NA_TILE_ROWS):
        ws = min(max(NA_TILE_ROWS * t - wr // 2, 0), rows - NA_SPAN_ROWS)
        pattern = []
        for i in range(NA_TILE_ROWS):
            r = NA_TILE_ROWS * t + i
            rs = min(max(r - wr // 2, 0), rows - wr)
            assert ws <= rs and rs + wr <= ws + NA_SPAN_ROWS
            pattern.append((rs - ws, r - ws))
        pattern = tuple(pattern)
        if pattern not in patterns:
            patterns.append(pattern)
        starts.append(ws)
        tile_pattern.append(patterns.index(pattern))
    return starts, tile_pattern, patterns


def _na_pair_entry(first, q_off, pair, wr):
    n_dr = 2 * NA_ROWS_MAX - 1
    j0 = 2 * pair
    valid0 = first <= j0 < first + wr
    valid1 = first <= j0 + 1 < first + wr
    dr0 = j0 - q_off + NA_ROWS_MAX - 1
    if valid0 and valid1:
        return dr0
    if valid1:
        return (n_dr - 1) + dr0 + 1
    if valid0:
        return (n_dr - 1) + n_dr + dr0
    return (n_dr - 1) + 2 * n_dr


def _na_attn_kernel(q_ref, k_ref, v_ref, g_ref, bias_ref, o_ref, vext_ref):
    seq = q_ref.shape[0]
    rows = seq // GRID_W
    starts, tile_pattern, patterns = _na_tiles(rows)
    n_tiles = len(starts)
    tq = NA_TILE_ROWS * GRID_W
    span = NA_SPAN_ROWS * GRID_W
    wr = min(NA_ROWS_MAX, rows)
    _fill_v_ext(vext_ref, v_ref)

    heads = vext_ref.shape[0]

    def keys(t):
        return slice(starts[t] * GRID_W, starts[t] * GRID_W + span)

    def logits(i):
        h, t = divmod(i, n_tiles)
        cols = slice(h * HEAD_DIM, (h + 1) * HEAD_DIM)
        return _dot_nt(q_ref[t * tq:(t + 1) * tq, cols], k_ref[keys(t), cols])

    def add_bias(h, t, s):
        out_rows = []
        for i, (first, q_off) in enumerate(patterns[tile_pattern[t]]):
            out_rows.append(jnp.concatenate(
                [s[i * GRID_W:(i + 1) * GRID_W, m * 2 * GRID_W:(m + 1) * 2 * GRID_W]
                 + bias_ref[h, _na_pair_entry(first, q_off, m, wr)]
                 for m in range(NA_SPAN_ROWS // 2)], axis=1))
        return jnp.concatenate(out_rows, axis=0)

    def finish(i, s):
        h, t = divmod(i, n_tiles)
        s = add_bias(h, t, s)
        e = jnp.exp2(s - jnp.max(s, axis=-1, keepdims=True)).astype(_BF16)
        acc = _dot(e, vext_ref[h, keys(t), :])
        rows = slice(t * tq, (t + 1) * tq)
        cols = slice(h * HEAD_DIM, (h + 1) * HEAD_DIM)
        o = acc[:, :HEAD_DIM] / acc[:, HEAD_DIM:]
        o_ref[rows, cols] = (o * g_ref[rows, cols].astype(_F32)).astype(_BF16)

    _pipelined(heads * n_tiles, logits, finish, depth=2)


def _na_attn(cqkv, gates, bias, batch, seq, layer):
    t = cqkv.shape[0]
    heads = NA_HEADS_PER_STEP
    width = heads * HEAD_DIM
    groups = C_HEADS // heads
    gate_blk = (A_WIDTH + B_WIDTH) // width
    blk = lambda off: pl.BlockSpec((seq, width), lambda hg, b: (b, off + hg))
    return pl.pallas_call(
        _na_attn_kernel,
        grid=(groups, batch),
        in_specs=[
            blk(0), blk(groups), blk(2 * groups), blk(gate_blk),
            pl.BlockSpec((None, heads) + bias.shape[2:], lambda hg, b: (layer, hg, 0, 0, 0)),
        ],
        out_specs=pl.BlockSpec((seq, width), lambda hg, b: (b, hg)),
        out_shape=jax.ShapeDtypeStruct((t, C_WIDTH), _BF16),
        scratch_shapes=[pltpu.VMEM((heads, seq, 2 * HEAD_DIM), _BF16)],
        compiler_params=_params(2),
        name="na_attn",
    )(cqkv, cqkv, cqkv, gates, bias)


def _out_ple_kernel(ya_ref, yb_ref, yc_ref, h_ref, p_ref, wo_ref, gple_ref, wg_ref, wp_ref, gpost_ref,
                    o_ref):
    tm = h_ref.shape[0]
    halves = [slice(i * (tm // 2), (i + 1) * (tm // 2)) for i in range(2)]
    pp = [_rms(_dot(p_ref[rows, :].astype(_BF16), wp_ref[...]), gpost_ref[...]) for rows in halves]
    h1 = []
    for rows in halves:
        acc = _dot(ya_ref[rows, :], wo_ref[:A_WIDTH, :])
        acc += _dot(yb_ref[rows, :], wo_ref[A_WIDTH:A_WIDTH + B_WIDTH, :])
        acc += _dot(yc_ref[rows, :], wo_ref[A_WIDTH + B_WIDTH:, :])
        h1.append(h_ref[rows, :] + acc)
    logit = [_dot(_rms(x, gple_ref[...]).astype(_BF16), wg_ref[...]) for x in h1]
    for i, rows in enumerate(halves):
        o_ref[rows, :] = h1[i] + jax.nn.sigmoid(logit[i]) * pp[i]


def _out_ple(ya, yb, yc, h, p, lw, layer, tm):
    t = h.shape[0]
    row = lambda i: (i, 0)
    vec = lambda n: pl.BlockSpec((None, 1, n), lambda i: (layer, 0, 0))
    mat = lambda r, c: _resident((None, r, c), lambda i: (layer, 0, 0))
    return pl.pallas_call(
        _out_ple_kernel,
        grid=(t // tm,),
        in_specs=[
            pl.BlockSpec((tm, A_WIDTH), row),
            pl.BlockSpec((tm, B_WIDTH), row),
            pl.BlockSpec((tm, C_WIDTH), row),
            pl.BlockSpec((tm, D_MODEL), row),
            pl.BlockSpec((None, tm, PLE_DIM), lambda i: (layer, i, 0)),
            mat(MIX_WIDTH, D_MODEL), vec(D_MODEL), mat(D_MODEL, D_MODEL), mat(PLE_DIM, D_MODEL),
            vec(D_MODEL),
        ],
        out_specs=pl.BlockSpec((tm, D_MODEL), row),
        out_shape=jax.ShapeDtypeStruct((t, D_MODEL), _F32),
        compiler_params=_params(1),
        name="out_ple",
    )(ya, yb, yc, h, p, lw['w_out'], lw['ple_norm'], lw['w_ple_gate'], lw['w_ple_proj'],
      lw['ple_post_norm'])


def _swap_halves(x):
    half = x.shape[-1] // 2
    return jnp.concatenate([x[..., half:], x[..., :half]], axis=-1)


def _prepare_weights(W):
    depth = W['w_in'].shape[0]
    w_in_t = jnp.swapaxes(W['w_in'], 1, 2)
    assert w_in_t.shape[1] == IN_HI_START + IN_HI_WIDTH
    piece = lambda rows: jnp.swapaxes(rows, 1, 2).astype(_BF16)
    bkr_t = w_in_t[:, IN_LO_WIDTH:IN_HI_START]
    half = B_ROPE // 2
    w_in_lo = piece(w_in_t[:, :IN_LO_WIDTH])
    w_in_hi = piece(w_in_t[:, IN_HI_START:])
    w_in_kr = piece(jnp.concatenate([bkr_t, bkr_t[:, half:], bkr_t[:, :half]], axis=1))

    wq = W['b_w_uq'].reshape(depth, B_Q_LORA, B_HEADS, B_QK)
    wq = jnp.concatenate([wq, _swap_halves(wq[..., B_NOPE:])], axis=-1)
    wq = wq.reshape(depth, B_Q_LORA, B_HEADS * B_QK_PAD).astype(_BF16)
    gq = W['b_q_norm']
    gk = W['b_k_norm']
    row = lambda g: g[:, None, :].astype(_F32)
    return {
        'norm_in': row(W['norm_in']),
        'w_in_lo': w_in_lo, 'w_in_hi': w_in_hi, 'w_in_kr': w_in_kr,
        'a_q_norm': row(W['a_q_norm']), 'a_k_norm': row(W['a_k_norm']), 'a_sink': W['a_sink'].astype(_F32),
        'b_cq_norm': row(W['b_cq_norm']), 'b_ckv_norm': row(W['b_ckv_norm']),
        'b_w_uq': wq, 'b_w_ukv': W['b_w_ukv'].astype(_BF16),
        'b_q_gain': row(jnp.concatenate([gq, _swap_halves(gq[:, B_NOPE:])], axis=-1)),
        'b_k_gain_nope': row(gk[:, :B_NOPE]),
        'b_k_gain_rope': row(jnp.concatenate([gk[:, B_NOPE:], _swap_halves(gk[:, B_NOPE:])], axis=-1)),
        'c_q_norm': row(W['c_q_norm']), 'c_k_norm': row(W['c_k_norm']),
        'w_out': W['w_out'].astype(_BF16), 'ple_norm': row(W['ple_norm']),
        'w_ple_gate': W['w_ple_gate'].astype(_BF16), 'w_ple_proj': W['w_ple_proj'].astype(_BF16),
        'ple_post_norm': row(W['ple_post_norm']),
    }


def _rope_tables(seq):
    half = B_ROPE // 2
    inv = ROPE_THETA ** (-jnp.arange(half, dtype=_F32) / half)
    ang = jnp.arange(seq).astype(_F32)[:, None] * inv[None, :]
    cos, sin = jnp.cos(ang), jnp.sin(ang)
    zeros = jnp.zeros((seq, B_ROPE), _F32)
    return {'cos2': jnp.concatenate([cos, cos, zeros], axis=-1),
            'sin2': jnp.concatenate([-sin, sin, zeros], axis=-1)}


def _na_bias_table(rpb):
    c = np.arange(GRID_W)
    cs = np.clip(c - NA_COLS // 2, 0, GRID_W - NA_COLS)
    col_valid = (c[None, :] >= cs[:, None]) & (c[None, :] < cs[:, None] + NA_COLS)
    dc = np.clip(c[None, :] - c[:, None] + NA_COLS - 1, 0, 2 * NA_COLS - 2)
    by_dc = jnp.where(col_valid, rpb.astype(_F32)[:, :, :, dc] * LOG2E, NEG_INF)
    masked = jnp.full_like(by_dc, NEG_INF)
    return jnp.concatenate([
        jnp.concatenate([by_dc[:, :, :-1], by_dc[:, :, 1:]], axis=-1),
        jnp.concatenate([masked, by_dc], axis=-1),
        jnp.concatenate([by_dc, masked], axis=-1),
        jnp.concatenate([masked[:, :, :1], masked[:, :, :1]], axis=-1),
    ], axis=2)


def _trunk(x, p, lw, tables, na_bias):
    batch, seq, _ = x.shape
    depth = p.shape[0]
    t = batch * seq
    h = x.reshape(t, D_MODEL)
    p = p.reshape(depth, t, PLE_DIM)
    for layer in range(depth):
        aq, akv, gates, cqkv, bq, bk, bv = _in_proj(h, lw, tables, layer, seq, IN_PROJ_ROWS)
        ya = _win_attn(aq, akv, gates, lw['a_sink'][layer], batch, seq)
        yb = _mla_attn(bq, bk, bv, gates, batch, seq, MLA_QUERY_ROWS)
        yc = _na_attn(cqkv, gates, na_bias, batch, seq, layer)
        h = _out_ple(ya, yb, yc, h, p, lw, layer, OUT_PLE_ROWS)
    return h.reshape(batch, seq, D_MODEL)


def kernel(x_prompt, x_sample, p_prompt, p_sample, norm_in, w_in, a_q_norm, a_k_norm, a_sink, b_cq_norm, b_ckv_norm, b_w_uq, b_w_ukv, b_q_norm, b_k_norm, c_q_norm, c_k_norm, c_rpb, w_out, ple_norm, w_ple_gate, w_ple_proj, ple_post_norm):
    lw = _prepare_weights({
        'norm_in': norm_in, 'w_in': w_in, 'a_q_norm': a_q_norm, 'a_k_norm': a_k_norm, 'a_sink': a_sink,
        'b_cq_norm': b_cq_norm, 'b_ckv_norm': b_ckv_norm, 'b_w_uq': b_w_uq, 'b_w_ukv': b_w_ukv,
        'b_q_norm': b_q_norm, 'b_k_norm': b_k_norm, 'c_q_norm': c_q_norm, 'c_k_norm': c_k_norm,
        'w_out': w_out, 'ple_norm': ple_norm, 'w_ple_gate': w_ple_gate, 'w_ple_proj': w_ple_proj,
        'ple_post_norm': ple_post_norm,
    })
    na_bias = _na_bias_table(c_rpb)
    rope = {}
    outs = []
    for x, p in ((x_prompt, p_prompt), (x_sample, p_sample)):
        seq = x.shape[1]
        if seq not in rope:
            rope[seq] = _rope_tables(seq)
        outs.append(_trunk(x, p, lw, rope[seq], na_bias))
    return tuple(outs)
```

```python
import functools

import jax
import jax.numpy as jnp
import numpy as np
from jax import lax
from jax.experimental import pallas as pl
from jax.experimental.pallas import tpu as pltpu

D_MODEL = 2048
PLE_DIM = 256
HEAD_DIM = 128
A_HEADS = 6
A_KV_HEADS = 2
A_GROUP = A_HEADS // A_KV_HEADS
A_WINDOW = 128
A_BLOCK = 128
B_HEADS = 6
B_Q_LORA = 512
B_KV_LORA = 512
B_NOPE = 128
B_ROPE = 64
B_V = 128
B_QK = B_NOPE + B_ROPE
ROPE_THETA = 10000.0
C_HEADS = 4
GRID_W = 64
NA_ROWS_MAX = 8
NA_COLS = 16
A_WIDTH = A_HEADS * HEAD_DIM
A_KV_WIDTH = A_KV_HEADS * HEAD_DIM
B_WIDTH = B_HEADS * B_V
C_WIDTH = C_HEADS * HEAD_DIM
MIX_WIDTH = A_WIDTH + B_WIDTH + C_WIDTH
EPS = 1e-6
NEG_INF = -1e30
LOG2E = 1.4426950408889634

B_QK_PAD = 2 * HEAD_DIM
IN_LO_WIDTH = 2 * A_WIDTH + 2 * A_KV_WIDTH + B_Q_LORA + B_KV_LORA
IN_HI_START = IN_LO_WIDTH + B_ROPE
IN_HI_WIDTH = B_WIDTH + 4 * C_WIDTH
LO_A_Q = 0
LO_A_KV = LO_A_Q + A_WIDTH
LO_A_Z = LO_A_KV + 2 * A_KV_WIDTH
LO_B_LAT = LO_A_Z + A_WIDTH
HI_B_Z = 0
HI_C_QKV = HI_B_Z + B_WIDTH
HI_C_Z = HI_C_QKV + 3 * C_WIDTH

VMEM_LIMIT_BYTES = 56 * 1024 * 1024
IN_PROJ_ROWS = 256
OUT_PLE_ROWS = 512
MLA_QUERY_ROWS = 256
MLA_HEADS_PER_STEP = 3
NA_HEADS_PER_STEP = 4
NA_TILE_ROWS = 4
NA_SPAN_ROWS = 12

_BF16 = jnp.bfloat16
_F32 = jnp.float32


def _params(n_grid_dims):
    return pltpu.CompilerParams(
        dimension_semantics=("arbitrary",) * n_grid_dims,
        vmem_limit_bytes=VMEM_LIMIT_BYTES)


def _resident(shape, index_map):
    return pl.BlockSpec(shape, index_map, pipeline_mode=pl.Buffered(1))


def _rms(x, g):
    return x * lax.rsqrt(jnp.mean(x * x, axis=-1, keepdims=True) + EPS) * g


def _dot(a, b):
    return jnp.dot(a, b, preferred_element_type=_F32)


def _dot_nt(a, b):
    return lax.dot_general(a, b, (((1,), (1,)), ((), ())), preferred_element_type=_F32)


def _in_proj_kernel(h_ref, gin_ref, wlo_ref, whi_ref, wkr_ref, gaq_ref, gak_ref, gbq_ref, gbkv_ref,
                    gcq_ref, gck_ref,
                    wq_ref, wkv_ref, gq_ref, gkn_ref, gkr_ref, cos_ref, sin_ref,
                    aq_ref, akv_ref, gates_ref, cqkv_ref, bq_ref, bk_ref, bv_ref):
    u = _rms(h_ref[...], gin_ref[...]).astype(_BF16)
    scale = HEAD_DIM ** -0.5 * LOG2E
    b_scale = B_QK ** -0.5 * LOG2E

    def seg(w_ref, start, width):
        return _dot(u, w_ref[:, start:start + width])

    def heads(x, g, n, s=None):
        outs = []
        for i in range(n):
            y = _rms(x[:, i * HEAD_DIM:(i + 1) * HEAD_DIM], g)
            outs.append(y if s is None else y * s)
        return outs

    x = seg(wlo_ref, LO_B_LAT, B_Q_LORA + B_KV_LORA)
    kr2 = _dot(u, wkr_ref[...])
    cqn = _rms(x[:, :B_Q_LORA], gbq_ref[...]).astype(_BF16)
    ckvn = _rms(x[:, B_Q_LORA:], gbkv_ref[...]).astype(_BF16)

    out_col = 0
    for w_ref, start, width in ((wlo_ref, LO_A_Z, A_WIDTH), (whi_ref, HI_B_Z, B_WIDTH),
                                (whi_ref, HI_C_Z, C_WIDTH)):
        z = seg(w_ref, start, width)
        gates_ref[:, out_col:out_col + width] = (z * jax.nn.sigmoid(z)).astype(_BF16)
        out_col += width

    xq = _dot(cqn, wq_ref[...])
    xkv = _dot(ckvn, wkv_ref[...])

    xa = seg(wlo_ref, LO_A_Q, A_WIDTH)
    xakv = seg(wlo_ref, LO_A_KV, 2 * A_KV_WIDTH)
    xc = seg(whi_ref, HI_C_QKV, 3 * C_WIDTH)

    cos2 = cos_ref[...]
    sin2 = sin_ref[...]
    rope_lanes = lax.broadcasted_iota(jnp.int32, (1, HEAD_DIM), 1) < B_ROPE

    def rope(c):
        return c * cos2 + pltpu.roll(c, B_ROPE, 1) * sin2

    def sumsq(y, lanes=None):
        y2 = y * y
        return jnp.sum(y2 if lanes is None else jnp.where(lanes, y2, 0.0), axis=-1, keepdims=True)

    ss_kr = sumsq(kr2, rope_lanes)
    k_rope = rope(kr2 * gkr_ref[...])
    for hd in range(B_HEADS):
        lo, mid, hi = hd * B_QK_PAD, hd * B_QK_PAD + HEAD_DIM, (hd + 1) * B_QK_PAD
        a = xq[:, lo:mid]
        c = xq[:, mid:hi]
        r = lax.rsqrt((sumsq(a) + sumsq(c, rope_lanes)) * (1.0 / B_QK) + EPS)
        bq_ref[:, lo:mid] = (a * r * gq_ref[:, :HEAD_DIM] * b_scale).astype(_BF16)
        bq_ref[:, mid:hi] = (rope(c * r * gq_ref[:, HEAD_DIM:]) * b_scale).astype(_BF16)
        kn = xkv[:, lo:mid]
        r = lax.rsqrt((sumsq(kn) + ss_kr) * (1.0 / B_QK) + EPS)
        bk_ref[:, lo:mid] = (kn * r * gkn_ref[...]).astype(_BF16)
        bk_ref[:, mid:hi] = (k_rope * r).astype(_BF16)
        bv_ref[:, hd * B_V:(hd + 1) * B_V] = xkv[:, mid:hi].astype(_BF16)

    for i, y in enumerate(heads(xa, gaq_ref[...], A_HEADS, scale)):
        aq_ref[:, i * HEAD_DIM:(i + 1) * HEAD_DIM] = y.astype(_BF16)
    for i, y in enumerate(heads(xakv[:, :A_KV_WIDTH], gak_ref[...], A_KV_HEADS)):
        akv_ref[:, i * HEAD_DIM:(i + 1) * HEAD_DIM] = y.astype(_BF16)
    akv_ref[:, A_KV_WIDTH:] = xakv[:, A_KV_WIDTH:].astype(_BF16)
    for i, y in enumerate(heads(xc[:, :C_WIDTH], gcq_ref[...], C_HEADS, scale)):
        cqkv_ref[:, i * HEAD_DIM:(i + 1) * HEAD_DIM] = y.astype(_BF16)
    for i, y in enumerate(heads(xc[:, C_WIDTH:2 * C_WIDTH], gck_ref[...], C_HEADS)):
        cqkv_ref[:, C_WIDTH + i * HEAD_DIM:C_WIDTH + (i + 1) * HEAD_DIM] = y.astype(_BF16)
    cqkv_ref[:, 2 * C_WIDTH:] = xc[:, 2 * C_WIDTH:].astype(_BF16)


def _in_proj(h, lw, tables, layer, seq, tm):
    t = h.shape[0]
    row = lambda i: (i, 0)
    pos = lambda i: (i % (seq // tm), 0)
    vec = lambda n: pl.BlockSpec((None, 1, n), lambda i: (layer, 0, 0))
    mat = lambda r, c: _resident((None, r, c), lambda i: (layer, 0, 0))
    widths = (A_WIDTH, 2 * A_KV_WIDTH, MIX_WIDTH, 3 * C_WIDTH, B_HEADS * B_QK_PAD, B_HEADS * B_QK_PAD,
              B_WIDTH)
    return pl.pallas_call(
        _in_proj_kernel,
        grid=(t // tm,),
        in_specs=[
            pl.BlockSpec((tm, D_MODEL), row),
            vec(D_MODEL),
            mat(D_MODEL, IN_LO_WIDTH), mat(D_MODEL, IN_HI_WIDTH), mat(D_MODEL, 2 * B_ROPE),
            vec(HEAD_DIM), vec(HEAD_DIM), vec(B_Q_LORA), vec(B_KV_LORA), vec(HEAD_DIM), vec(HEAD_DIM),
            mat(B_Q_LORA, B_HEADS * B_QK_PAD), mat(B_KV_LORA, B_HEADS * (B_NOPE + B_V)),
            vec(B_QK_PAD), vec(HEAD_DIM), vec(2 * B_ROPE),
            pl.BlockSpec((tm, 2 * B_ROPE), pos),
            pl.BlockSpec((tm, 2 * B_ROPE), pos),
        ],
        out_specs=[pl.BlockSpec((tm, w), row) for w in widths],
        out_shape=[jax.ShapeDtypeStruct((t, w), _BF16) for w in widths],
        compiler_params=_params(1),
        name="in_proj",
    )(h, lw['norm_in'], lw['w_in_lo'], lw['w_in_hi'], lw['w_in_kr'], lw['a_q_norm'], lw['a_k_norm'],
      lw['b_cq_norm'], lw['b_ckv_norm'], lw['c_q_norm'], lw['c_k_norm'], lw['b_w_uq'], lw['b_w_ukv'], lw['b_q_gain'], lw['b_k_gain_nope'],
      lw['b_k_gain_rope'], tables['cos2'], tables['sin2'])


def _fill_v_ext(vext_ref, v_ref):
    heads, rows, ext = vext_ref.shape
    width = v_ref.shape[1] // heads
    for h in range(heads):
        vext_ref[h, :, :width] = v_ref[:, h * width:(h + 1) * width]
        vext_ref[h, :, width:] = jnp.ones((rows, ext - width), vext_ref.dtype)


def _pipelined(n, logits, finish, depth):
    queue = [logits(i) for i in range(min(depth, n))]
    for i in range(n):
        s = queue.pop(0)
        if i + depth < n:
            queue.append(logits(i + depth))
        finish(i, s)


def _mla_attn_kernel(q_ref, k_ref, v_ref, g_ref, o_ref, vext_ref, *, tq):
    heads = vext_ref.shape[0]
    n_chunks = q_ref.shape[0] // tq
    _fill_v_ext(vext_ref, v_ref)

    def logits(i):
        h, c = divmod(i, n_chunks)
        qk_cols = slice(h * B_QK_PAD, (h + 1) * B_QK_PAD)
        return _dot_nt(q_ref[c * tq:(c + 1) * tq, qk_cols], k_ref[:, qk_cols])

    def finish(i, s):
        h, c = divmod(i, n_chunks)
        e = jnp.exp2(s - jnp.max(s, axis=-1, keepdims=True)).astype(_BF16)
        acc = _dot(e, vext_ref[h])
        rows = slice(c * tq, (c + 1) * tq)
        cols = slice(h * B_V, (h + 1) * B_V)
        o = acc[:, :B_V] / acc[:, B_V:]
        o_ref[rows, cols] = (o * g_ref[rows, cols].astype(_F32)).astype(_BF16)

    _pipelined(heads * n_chunks, logits, finish, depth=1)


def _mla_attn(q, k, v, gates, batch, seq, tq):
    t = q.shape[0]
    heads = MLA_HEADS_PER_STEP
    gate_blk = A_WIDTH // (heads * B_V)
    return pl.pallas_call(
        functools.partial(_mla_attn_kernel, tq=tq),
        grid=(batch, B_HEADS // heads),
        in_specs=[
            pl.BlockSpec((seq, heads * B_QK_PAD), lambda b, hg: (b, hg)),
            pl.BlockSpec((seq, heads * B_QK_PAD), lambda b, hg: (b, hg)),
            pl.BlockSpec((seq, heads * B_V), lambda b, hg: (b, hg)),
            pl.BlockSpec((seq, heads * B_V), lambda b, hg: (b, gate_blk + hg)),
        ],
        out_specs=pl.BlockSpec((seq, heads * B_V), lambda b, hg: (b, hg)),
        out_shape=jax.ShapeDtypeStruct((t, B_WIDTH), _BF16),
        scratch_shapes=[pltpu.VMEM((heads, seq, 2 * B_V), _BF16)],
        compiler_params=_params(2),
        name="mla_attn",
    )(q, k, v, gates)


def _win_attn_kernel(sink_ref, slope_ref, q_ref, k_ref, v_ref, g_ref, o_ref, bias_ref, vext_ref):
    seq = q_ref.shape[0]
    nb = seq // A_BLOCK
    span = 3 * A_BLOCK
    _fill_v_ext(vext_ref, v_ref)

    @pl.when(pl.program_id(0) == 0)
    def _():
        qi = lax.broadcasted_iota(jnp.int32, (A_BLOCK, span), 0)
        si = lax.broadcasted_iota(jnp.int32, (A_BLOCK, span), 1)
        for variant in range(3):
            dist = jnp.abs(si - qi - variant * A_BLOCK)
            valid = dist <= A_WINDOW
            for head in range(A_HEADS):
                slope = slope_ref[head] * LOG2E
                bias_ref[head // A_GROUP, variant, (head % A_GROUP) * A_BLOCK:(head % A_GROUP + 1) * A_BLOCK, :] = (
                    jnp.where(valid, -(slope * dist.astype(_F32)), NEG_INF))

    sinks = [sink_ref[head] * LOG2E for head in range(A_HEADS)]

    def key_start(j):
        return min(max((j - 1) * A_BLOCK, 0), seq - span)

    def logits(i):
        kvh, j = divmod(i, nb)
        qrows = slice(j * A_BLOCK, (j + 1) * A_BLOCK)
        qs = jnp.concatenate(
            [q_ref[qrows, (kvh * A_GROUP + g) * HEAD_DIM:(kvh * A_GROUP + g + 1) * HEAD_DIM]
             for g in range(A_GROUP)], axis=0)
        return _dot_nt(qs, k_ref[key_start(j):key_start(j) + span, kvh * HEAD_DIM:(kvh + 1) * HEAD_DIM])

    def finish(i, s):
        kvh, j = divmod(i, nb)
        start = key_start(j)
        s = s + bias_ref[kvh, (j * A_BLOCK - start) // A_BLOCK]
        probs, sink_terms = [], []
        for g in range(A_GROUP):
            sink = sinks[kvh * A_GROUP + g]
            sg = s[g * A_BLOCK:(g + 1) * A_BLOCK]
            m = jnp.maximum(jnp.max(sg, axis=-1, keepdims=True), sink)
            probs.append(jnp.exp2(sg - m).astype(_BF16))
            sink_terms.append(jnp.exp2(sink - m))
        acc = _dot(jnp.concatenate(probs, axis=0), vext_ref[kvh, start:start + span, :])
        qrows = slice(j * A_BLOCK, (j + 1) * A_BLOCK)
        for g in range(A_GROUP):
            ag = acc[g * A_BLOCK:(g + 1) * A_BLOCK]
            o = ag[:, :HEAD_DIM] / (ag[:, HEAD_DIM:] + sink_terms[g])
            cols = slice((kvh * A_GROUP + g) * HEAD_DIM, (kvh * A_GROUP + g + 1) * HEAD_DIM)
            o_ref[qrows, cols] = (o * g_ref[qrows, cols].astype(_F32)).astype(_BF16)

    _pipelined(A_KV_HEADS * nb, logits, finish, depth=2)


def _win_attn(aq, akv, gates, sink, batch, seq):
    t = aq.shape[0]
    assert seq >= 3 * A_BLOCK
    slopes = jnp.exp2(-8.0 * jnp.arange(1, A_HEADS + 1, dtype=_F32) / A_HEADS)
    return pl.pallas_call(
        _win_attn_kernel,
        grid=(batch,),
        in_specs=[
            pl.BlockSpec(memory_space=pltpu.SMEM),
            pl.BlockSpec(memory_space=pltpu.SMEM),
            pl.BlockSpec((seq, A_WIDTH), lambda b: (b, 0)),
            pl.BlockSpec((seq, A_KV_WIDTH), lambda b: (b, 0)),
            pl.BlockSpec((seq, A_KV_WIDTH), lambda b: (b, 1)),
            pl.BlockSpec((seq, A_WIDTH), lambda b: (b, 0)),
        ],
        out_specs=pl.BlockSpec((seq, A_WIDTH), lambda b: (b, 0)),
        out_shape=jax.ShapeDtypeStruct((t, A_WIDTH), _BF16),
        scratch_shapes=[
            pltpu.VMEM((A_KV_HEADS, 3, A_GROUP * A_BLOCK, 3 * A_BLOCK), _F32),
            pltpu.VMEM((A_KV_HEADS, seq, 2 * HEAD_DIM), _BF16),
        ],
        compiler_params=_params(1),
        name="win_attn",
    )(sink, slopes, aq, akv, akv, gates)


def _na_tiles(rows):
    wr = min(NA_ROWS_MAX, rows)
    assert rows % NA_TILE_ROWS == 0 and rows >= NA_SPAN_ROWS
    starts, tile_pattern, patterns = [], [], []
    for t in range(rows // NA_TILE_ROWS):
        ws = min(max(NA_TILE_ROWS * t - wr // 2, 0), rows - NA_SPAN_ROWS)
        pattern = []
        for i in range(NA_TILE_ROWS):
            r = NA_TILE_ROWS * t + i
            rs = min(max(r - wr // 2, 0), rows - wr)
            assert ws <= rs and rs + wr <= ws + NA_SPAN_ROWS
            pattern.append((rs - ws, r - ws))
        pattern = tuple(pattern)
        if pattern not in patterns:
            patterns.append(pattern)
        starts.append(ws)
        tile_pattern.append(patterns.index(pattern))
    return starts, tile_pattern, patterns


def _na_pair_entry(first, q_off, pair, wr):
    n_dr = 2 * NA_ROWS_MAX - 1
    j0 = 2 * pair
    valid0 = first <= j0 < first + wr
    valid1 = first <= j0 + 1 < first + wr
    dr0 = j0 - q_off + NA_ROWS_MAX - 1
    if valid0 and valid1:
        return dr0
    if valid1:
        return (n_dr - 1) + dr0 + 1
    if valid0:
        return (n_dr - 1) + n_dr + dr0
    return (n_dr - 1) + 2 * n_dr


def _na_attn_kernel(q_ref, k_ref, v_ref, g_ref, bias_ref, o_ref, vext_ref):
    seq = q_ref.shape[0]
    rows = seq // GRID_W
    starts, tile_pattern, patterns = _na_tiles(rows)
    n_tiles = len(starts)
    tq = NA_TILE_ROWS * GRID_W
    span = NA_SPAN_ROWS * GRID_W
    wr = min(NA_ROWS_MAX, rows)
    _fill_v_ext(vext_ref, v_ref)

    heads = vext_ref.shape[0]

    def keys(t):
        return slice(starts[t] * GRID_W, starts[t] * GRID_W + span)

    def logits(i):
        h, t = divmod(i, n_tiles)
        cols = slice(h * HEAD_DIM, (h + 1) * HEAD_DIM)
        return _dot_nt(q_ref[t * tq:(t + 1) * tq, cols], k_ref[keys(t), cols])

    def add_bias(h, t, s):
        out_rows = []
        for i, (first, q_off) in enumerate(patterns[tile_pattern[t]]):
            out_rows.append(jnp.concatenate(
                [s[i * GRID_W:(i + 1) * GRID_W, m * 2 * GRID_W:(m + 1) * 2 * GRID_W]
                 + bias_ref[h, _na_pair_entry(first, q_off, m, wr)]
                 for m in range(NA_SPAN_ROWS // 2)], axis=1))
        return jnp.concatenate(out_rows, axis=0)

    def finish(i, s):
        h, t = divmod(i, n_tiles)
        s = add_bias(h, t, s)
        e = jnp.exp2(s - jnp.max(s, axis=-1, keepdims=True)).astype(_BF16)
        acc = _dot(e, vext_ref[h, keys(t), :])
        rows = slice(t * tq, (t + 1) * tq)
        cols = slice(h * HEAD_DIM, (h + 1) * HEAD_DIM)
        o = acc[:, :HEAD_DIM] / acc[:, HEAD_DIM:]
        o_ref[rows, cols] = (o * g_ref[rows, cols].astype(_F32)).astype(_BF16)

    _pipelined(heads * n_tiles, logits, finish, depth=2)


def _na_attn(cqkv, gates, bias, batch, seq, layer):
    t = cqkv.shape[0]
    heads = NA_HEADS_PER_STEP
    width = heads * HEAD_DIM
    groups = C_HEADS // heads
    gate_blk = (A_WIDTH + B_WIDTH) // width
    blk = lambda off: pl.BlockSpec((seq, width), lambda hg, b: (b, off + hg))
    return pl.pallas_call(
        _na_attn_kernel,
        grid=(groups, batch),
        in_specs=[
            blk(0), blk(groups), blk(2 * groups), blk(gate_blk),
            pl.BlockSpec((None, heads) + bias.shape[2:], lambda hg, b: (layer, hg, 0, 0, 0)),
        ],
        out_specs=pl.BlockSpec((seq, width), lambda hg, b: (b, hg)),
        out_shape=jax.ShapeDtypeStruct((t, C_WIDTH), _BF16),
        scratch_shapes=[pltpu.VMEM((heads, seq, 2 * HEAD_DIM), _BF16)],
        compiler_params=_params(2),
        name="na_attn",
    )(cqkv, cqkv, cqkv, gates, bias)


def _out_ple_kernel(ya_ref, yb_ref, yc_ref, h_ref, p_ref, wo_ref, gple_ref, wg_ref, wp_ref, gpost_ref,
                    o_ref):
    tm = h_ref.shape[0]
    halves = [slice(i * (tm // 2), (i + 1) * (tm // 2)) for i in range(2)]
    pp = [_rms(_dot(p_ref[rows, :].astype(_BF16), wp_ref[...]), gpost_ref[...]) for rows in halves]
    h1 = []
    for rows in halves:
        acc = _dot(ya_ref[rows, :], wo_ref[:A_WIDTH, :])
        acc += _dot(yb_ref[rows, :], wo_ref[A_WIDTH:A_WIDTH + B_WIDTH, :])
        acc += _dot(yc_ref[rows, :], wo_ref[A_WIDTH + B_WIDTH:, :])
        h1.append(h_ref[rows, :] + acc)
    logit = [_dot(_rms(x, gple_ref[...]).astype(_BF16), wg_ref[...]) for x in h1]
    for i, rows in enumerate(halves):
        o_ref[rows, :] = h1[i] + jax.nn.sigmoid(logit[i]) * pp[i]


def _out_ple(ya, yb, yc, h, p, lw, layer, tm):
    t = h.shape[0]
    row = lambda i: (i, 0)
    vec = lambda n: pl.BlockSpec((None, 1, n), lambda i: (layer, 0, 0))
    mat = lambda r, c: _resident((None, r, c), lambda i: (layer, 0, 0))
    return pl.pallas_call(
        _out_ple_kernel,
        grid=(t // tm,),
        in_specs=[
            pl.BlockSpec((tm, A_WIDTH), row),
            pl.BlockSpec((tm, B_WIDTH), row),
            pl.BlockSpec((tm, C_WIDTH), row),
            pl.BlockSpec((tm, D_MODEL), row),
            pl.BlockSpec((None, tm, PLE_DIM), lambda i: (layer, i, 0)),
            mat(MIX_WIDTH, D_MODEL), vec(D_MODEL), mat(D_MODEL, D_MODEL), mat(PLE_DIM, D_MODEL),
            vec(D_MODEL),
        ],
        out_specs=pl.BlockSpec((tm, D_MODEL), row),
        out_shape=jax.ShapeDtypeStruct((t, D_MODEL), _F32),
        compiler_params=_params(1),
        name="out_ple",
    )(ya, yb, yc, h, p, lw['w_out'], lw['ple_norm'], lw['w_ple_gate'], lw['w_ple_proj'],
      lw['ple_post_norm'])


def _swap_halves(x):
    half = x.shape[-1] // 2
    return jnp.concatenate([x[..., half:], x[..., :half]], axis=-1)


def _prepare_weights(W):
    depth = W['w_in'].shape[0]
    w_in_t = jnp.swapaxes(W['w_in'], 1, 2)
    assert w_in_t.shape[1] == IN_HI_START + IN_HI_WIDTH
    piece = lambda rows: jnp.swapaxes(rows, 1, 2).astype(_BF16)
    bkr_t = w_in_t[:, IN_LO_WIDTH:IN_HI_START]
    half = B_ROPE // 2
    w_in_lo = piece(w_in_t[:, :IN_LO_WIDTH])
    w_in_hi = piece(w_in_t[:, IN_HI_START:])
    w_in_kr = piece(jnp.concatenate([bkr_t, bkr_t[:, half:], bkr_t[:, :half]], axis=1))

    wq = W['b_w_uq'].reshape(depth, B_Q_LORA, B_HEADS, B_QK)
    wq = jnp.concatenate([wq, _swap_halves(wq[..., B_NOPE:])], axis=-1)
    wq = wq.reshape(depth, B_Q_LORA, B_HEADS * B_QK_PAD).astype(_BF16)
    gq = W['b_q_norm']
    gk = W['b_k_norm']
    row = lambda g: g[:, None, :].astype(_F32)
    return {
        'norm_in': row(W['norm_in']),
        'w_in_lo': w_in_lo, 'w_in_hi': w_in_hi, 'w_in_kr': w_in_kr,
        'a_q_norm': row(W['a_q_norm']), 'a_k_norm': row(W['a_k_norm']), 'a_sink': W['a_sink'].astype(_F32),
        'b_cq_norm': row(W['b_cq_norm']), 'b_ckv_norm': row(W['b_ckv_norm']),
        'b_w_uq': wq, 'b_w_ukv': W['b_w_ukv'].astype(_BF16),
        'b_q_gain': row(jnp.concatenate([gq, _swap_halves(gq[:, B_NOPE:])], axis=-1)),
        'b_k_gain_nope': row(gk[:, :B_NOPE]),
        'b_k_gain_rope': row(jnp.concatenate([gk[:, B_NOPE:], _swap_halves(gk[:, B_NOPE:])], axis=-1)),
        'c_q_norm': row(W['c_q_norm']), 'c_k_norm': row(W['c_k_norm']),
        'w_out': W['w_out'].astype(_BF16), 'ple_norm': row(W['ple_norm']),
        'w_ple_gate': W['w_ple_gate'].astype(_BF16), 'w_ple_proj': W['w_ple_proj'].astype(_BF16),
        'ple_post_norm': row(W['ple_post_norm']),
    }


def _rope_tables(seq):
    half = B_ROPE // 2
    inv = ROPE_THETA ** (-jnp.arange(half, dtype=_F32) / half)
    ang = jnp.arange(seq).astype(_F32)[:, None] * inv[None, :]
    cos, sin = jnp.cos(ang), jnp.sin(ang)
    zeros = jnp.zeros((seq, B_ROPE), _F32)
    return {'cos2': jnp.concatenate([cos, cos, zeros], axis=-1),
            'sin2': jnp.concatenate([-sin, sin, zeros], axis=-1)}


def _na_bias_table(rpb):
    c = np.arange(GRID_W)
    cs = np.clip(c - NA_COLS // 2, 0, GRID_W - NA_COLS)
    col_valid = (c[None, :] >= cs[:, None]) & (c[None, :] < cs[:, None] + NA_COLS)
    pad = GRID_W - NA_COLS
    padded = jnp.pad(rpb.astype(_F32), ((0, 0), (0, 0), (0, 0), (pad, pad)), mode='edge')
    by_dc = jnp.stack([padded[..., GRID_W - 1 - qc:2 * GRID_W - 1 - qc] for qc in range(GRID_W)], axis=3)
    by_dc = jnp.where(col_valid, by_dc * LOG2E, NEG_INF)
    masked = jnp.full_like(by_dc, NEG_INF)
    return jnp.concatenate([
        jnp.concatenate([by_dc[:, :, :-1], by_dc[:, :, 1:]], axis=-1),
        jnp.concatenate([masked, by_dc], axis=-1),
        jnp.concatenate([by_dc, masked], axis=-1),
        jnp.concatenate([masked[:, :, :1], masked[:, :, :1]], axis=-1),
    ], axis=2)


def _trunk(x, p, lw, tables, na_bias):
    batch, seq, _ = x.shape
    depth = p.shape[0]
    t = batch * seq
    h = x.reshape(t, D_MODEL)
    p = p.reshape(depth, t, PLE_DIM)
    for layer in range(depth):
        aq, akv, gates, cqkv, bq, bk, bv = _in_proj(h, lw, tables, layer, seq, IN_PROJ_ROWS)
        ya = _win_attn(aq, akv, gates, lw['a_sink'][layer], batch, seq)
        yb = _mla_attn(bq, bk, bv, gates, batch, seq, MLA_QUERY_ROWS)
        yc = _na_attn(cqkv, gates, na_bias, batch, seq, layer)
        h = _out_ple(ya, yb, yc, h, p, lw, layer, OUT_PLE_ROWS)
    return h.reshape(batch, seq, D_MODEL)


def kernel(x_prompt, x_sample, p_prompt, p_sample, norm_in, w_in, a_q_norm, a_k_norm, a_sink, b_cq_norm, b_ckv_norm, b_w_uq, b_w_ukv, b_q_norm, b_k_norm, c_q_norm, c_k_norm, c_rpb, w_out, ple_norm, w_ple_gate, w_ple_proj, ple_post_norm):
    lw = _prepare_weights({
        'norm_in': norm_in, 'w_in': w_in, 'a_q_norm': a_q_norm, 'a_k_norm': a_k_norm, 'a_sink': a_sink,
        'b_cq_norm': b_cq_norm, 'b_ckv_norm': b_ckv_norm, 'b_w_uq': b_w_uq, 'b_w_ukv': b_w_ukv,
        'b_q_norm': b_q_norm, 'b_k_norm': b_k_norm, 'c_q_norm': c_q_norm, 'c_k_norm': c_k_norm,
        'w_out': w_out, 'ple_norm': ple_norm, 'w_ple_gate': w_ple_gate, 'w_ple_proj': w_ple_proj,
        'ple_post_norm': ple_post_norm,
    })
    na_bias = _na_bias_table(c_rpb)
    rope = {}
    outs = []
    for x, p in ((x_prompt, p_prompt), (x_sample, p_sample)):
        seq = x.shape[1]
        if seq not in rope:
            rope[seq] = _rope_tables(seq)
        outs.append(_trunk(x, p, lw, rope[seq], na_bias))
    return tuple(outs)
```

```python
import functools

import jax
import jax.numpy as jnp
import numpy as np
from jax import lax
from jax.experimental import pallas as pl
from jax.experimental.pallas import tpu as pltpu

D_MODEL = 2048
PLE_DIM = 256
HEAD_DIM = 128
A_HEADS = 6
A_KV_HEADS = 2
A_GROUP = A_HEADS // A_KV_HEADS
A_WINDOW = 128
A_BLOCK = 128
B_HEADS = 6
B_Q_LORA = 512
B_KV_LORA = 512
B_NOPE = 128
B_ROPE = 64
B_V = 128
B_QK = B_NOPE + B_ROPE
ROPE_THETA = 10000.0
C_HEADS = 4
GRID_W = 64
NA_ROWS_MAX = 8
NA_COLS = 16
A_WIDTH = A_HEADS * HEAD_DIM
A_KV_WIDTH = A_KV_HEADS * HEAD_DIM
B_WIDTH = B_HEADS * B_V
C_WIDTH = C_HEADS * HEAD_DIM
MIX_WIDTH = A_WIDTH + B_WIDTH + C_WIDTH
EPS = 1e-6
NEG_INF = -1e30
LOG2E = 1.4426950408889634

B_QK_PAD = 2 * HEAD_DIM
IN_LO_WIDTH = 2 * A_WIDTH + 2 * A_KV_WIDTH + B_Q_LORA + B_KV_LORA
IN_HI_START = IN_LO_WIDTH + B_ROPE
IN_HI_WIDTH = B_WIDTH + 4 * C_WIDTH
LO_A_Q = 0
LO_A_KV = LO_A_Q + A_WIDTH
LO_A_Z = LO_A_KV + 2 * A_KV_WIDTH
LO_B_LAT = LO_A_Z + A_WIDTH
HI_B_Z = 0
HI_C_QKV = HI_B_Z + B_WIDTH
HI_C_Z = HI_C_QKV + 3 * C_WIDTH

VMEM_LIMIT_BYTES = 56 * 1024 * 1024
IN_PROJ_ROWS = 256
OUT_PLE_ROWS = 512
MLA_QUERY_ROWS = 256
MLA_HEADS_PER_STEP = 3
NA_HEADS_PER_STEP = 4
NA_TILE_ROWS = 4
NA_SPAN_ROWS = 12

_BF16 = jnp.bfloat16
_F32 = jnp.float32


def _params(n_grid_dims):
    return pltpu.CompilerParams(
        dimension_semantics=("arbitrary",) * n_grid_dims,
        vmem_limit_bytes=VMEM_LIMIT_BYTES)


def _resident(shape, index_map):
    return pl.BlockSpec(shape, index_map, pipeline_mode=pl.Buffered(1))


def _rms(x, g):
    return x * lax.rsqrt(jnp.mean(x * x, axis=-1, keepdims=True) + EPS) * g


def _dot(a, b):
    return jnp.dot(a, b, preferred_element_type=_F32)


def _dot_nt(a, b):
    return lax.dot_general(a, b, (((1,), (1,)), ((), ())), preferred_element_type=_F32)


def _in_proj_kernel(h_ref, gin_ref, wlo_ref, wkr_ref, gaq_ref, gak_ref, gbq_ref, gbkv_ref,
                    gcq_ref, gck_ref,
                    wq_ref, wkv_ref, gq_ref, gkn_ref, gkr_ref, cos_ref, sin_ref,
                    aq_ref, akv_ref, gates_ref, cqkv_ref, bq_ref, bk_ref, bv_ref):
    u = _rms(h_ref[...], gin_ref[...]).astype(_BF16)
    scale = HEAD_DIM ** -0.5 * LOG2E
    b_scale = B_QK ** -0.5 * LOG2E

    def seg(w_ref, start, width):
        base = IN_HI_START if w_ref is whi_ref else 0
        return _dot_nt(u, wlo_ref[base + start:base + start + width, :])

    whi_ref = "upper piece"

    def heads(x, g, n, s=None):
        outs = []
        for i in range(n):
            y = _rms(x[:, i * HEAD_DIM:(i + 1) * HEAD_DIM], g)
            outs.append(y if s is None else y * s)
        return outs

    x = seg(wlo_ref, LO_B_LAT, B_Q_LORA + B_KV_LORA)
    kr2 = _dot_nt(u, wkr_ref[...])
    cqn = _rms(x[:, :B_Q_LORA], gbq_ref[...]).astype(_BF16)
    ckvn = _rms(x[:, B_Q_LORA:], gbkv_ref[...]).astype(_BF16)

    out_col = 0
    for w_ref, start, width in ((wlo_ref, LO_A_Z, A_WIDTH), (whi_ref, HI_B_Z, B_WIDTH),
                                (whi_ref, HI_C_Z, C_WIDTH)):
        z = seg(w_ref, start, width)
        gates_ref[:, out_col:out_col + width] = (z * jax.nn.sigmoid(z)).astype(_BF16)
        out_col += width

    xq = _dot(cqn, wq_ref[...])
    xkv = _dot(ckvn, wkv_ref[...])

    xa = seg(wlo_ref, LO_A_Q, A_WIDTH)
    xakv = seg(wlo_ref, LO_A_KV, 2 * A_KV_WIDTH)
    xc = seg(whi_ref, HI_C_QKV, 3 * C_WIDTH)

    cos2 = cos_ref[...]
    sin2 = sin_ref[...]
    rope_lanes = lax.broadcasted_iota(jnp.int32, (1, HEAD_DIM), 1) < B_ROPE

    def rope(c):
        return c * cos2 + pltpu.roll(c, B_ROPE, 1) * sin2

    def sumsq(y, lanes=None):
        y2 = y * y
        return jnp.sum(y2 if lanes is None else jnp.where(lanes, y2, 0.0), axis=-1, keepdims=True)

    ss_kr = sumsq(kr2, rope_lanes)
    k_rope = rope(kr2 * gkr_ref[...])
    for hd in range(B_HEADS):
        lo, mid, hi = hd * B_QK_PAD, hd * B_QK_PAD + HEAD_DIM, (hd + 1) * B_QK_PAD
        a = xq[:, lo:mid]
        c = xq[:, mid:hi]
        r = lax.rsqrt((sumsq(a) + sumsq(c, rope_lanes)) * (1.0 / B_QK) + EPS)
        bq_ref[:, lo:mid] = (a * r * gq_ref[:, :HEAD_DIM] * b_scale).astype(_BF16)
        bq_ref[:, mid:hi] = (rope(c * r * gq_ref[:, HEAD_DIM:]) * b_scale).astype(_BF16)
        kn = xkv[:, lo:mid]
        r = lax.rsqrt((sumsq(kn) + ss_kr) * (1.0 / B_QK) + EPS)
        bk_ref[:, lo:mid] = (kn * r * gkn_ref[...]).astype(_BF16)
        bk_ref[:, mid:hi] = (k_rope * r).astype(_BF16)
        bv_ref[:, hd * B_V:(hd + 1) * B_V] = xkv[:, mid:hi].astype(_BF16)

    for i, y in enumerate(heads(xa, gaq_ref[...], A_HEADS, scale)):
        aq_ref[:, i * HEAD_DIM:(i + 1) * HEAD_DIM] = y.astype(_BF16)
    for i, y in enumerate(heads(xakv[:, :A_KV_WIDTH], gak_ref[...], A_KV_HEADS)):
        akv_ref[:, i * HEAD_DIM:(i + 1) * HEAD_DIM] = y.astype(_BF16)
    akv_ref[:, A_KV_WIDTH:] = xakv[:, A_KV_WIDTH:].astype(_BF16)
    for i, y in enumerate(heads(xc[:, :C_WIDTH], gcq_ref[...], C_HEADS, scale)):
        cqkv_ref[:, i * HEAD_DIM:(i + 1) * HEAD_DIM] = y.astype(_BF16)
    for i, y in enumerate(heads(xc[:, C_WIDTH:2 * C_WIDTH], gck_ref[...], C_HEADS)):
        cqkv_ref[:, C_WIDTH + i * HEAD_DIM:C_WIDTH + (i + 1) * HEAD_DIM] = y.astype(_BF16)
    cqkv_ref[:, 2 * C_WIDTH:] = xc[:, 2 * C_WIDTH:].astype(_BF16)


def _in_proj(h, lw, tables, layer, seq, tm):
    t = h.shape[0]
    row = lambda i: (i, 0)
    pos = lambda i: (i % (seq // tm), 0)
    vec = lambda n: pl.BlockSpec((None, 1, n), lambda i: (layer, 0, 0))
    mat = lambda r, c: _resident((None, r, c), lambda i: (layer, 0, 0))
    widths = (A_WIDTH, 2 * A_KV_WIDTH, MIX_WIDTH, 3 * C_WIDTH, B_HEADS * B_QK_PAD, B_HEADS * B_QK_PAD,
              B_WIDTH)
    return pl.pallas_call(
        _in_proj_kernel,
        grid=(t // tm,),
        in_specs=[
            pl.BlockSpec((tm, D_MODEL), row),
            vec(D_MODEL),
            mat(IN_HI_START + IN_HI_WIDTH, D_MODEL), mat(2 * B_ROPE, D_MODEL),
            vec(HEAD_DIM), vec(HEAD_DIM), vec(B_Q_LORA), vec(B_KV_LORA), vec(HEAD_DIM), vec(HEAD_DIM),
            mat(B_Q_LORA, B_HEADS * B_QK_PAD), mat(B_KV_LORA, B_HEADS * (B_NOPE + B_V)),
            vec(B_QK_PAD), vec(HEAD_DIM), vec(2 * B_ROPE),
            pl.BlockSpec((tm, 2 * B_ROPE), pos),
            pl.BlockSpec((tm, 2 * B_ROPE), pos),
        ],
        out_specs=[pl.BlockSpec((tm, w), row) for w in widths],
        out_shape=[jax.ShapeDtypeStruct((t, w), _BF16) for w in widths],
        compiler_params=_params(1),
        name="in_proj",
    )(h, lw['norm_in'], lw['w_in_t'], lw['w_in_kr_t'], lw['a_q_norm'], lw['a_k_norm'],
      lw['b_cq_norm'], lw['b_ckv_norm'], lw['c_q_norm'], lw['c_k_norm'], lw['b_w_uq'], lw['b_w_ukv'], lw['b_q_gain'], lw['b_k_gain_nope'],
      lw['b_k_gain_rope'], tables['cos2'], tables['sin2'])


def _fill_v_ext(vext_ref, v_ref):
    heads, rows, ext = vext_ref.shape
    width = v_ref.shape[1] // heads
    for h in range(heads):
        vext_ref[h, :, :width] = v_ref[:, h * width:(h + 1) * width]
        vext_ref[h, :, width:] = jnp.ones((rows, ext - width), vext_ref.dtype)


def _pipelined(n, logits, finish, depth):
    queue = [logits(i) for i in range(min(depth, n))]
    for i in range(n):
        s = queue.pop(0)
        if i + depth < n:
            queue.append(logits(i + depth))
        finish(i, s)


def _mla_attn_kernel(q_ref, k_ref, v_ref, g_ref, o_ref, vext_ref, *, tq):
    heads = vext_ref.shape[0]
    n_chunks = q_ref.shape[0] // tq
    _fill_v_ext(vext_ref, v_ref)

    def logits(i):
        h, c = divmod(i, n_chunks)
        qk_cols = slice(h * B_QK_PAD, (h + 1) * B_QK_PAD)
        return _dot_nt(q_ref[c * tq:(c + 1) * tq, qk_cols], k_ref[:, qk_cols])

    def finish(i, s):
        h, c = divmod(i, n_chunks)
        e = jnp.exp2(s - jnp.max(s, axis=-1, keepdims=True)).astype(_BF16)
        acc = _dot(e, vext_ref[h])
        rows = slice(c * tq, (c + 1) * tq)
        cols = slice(h * B_V, (h + 1) * B_V)
        o = acc[:, :B_V] / acc[:, B_V:]
        o_ref[rows, cols] = (o * g_ref[rows, cols].astype(_F32)).astype(_BF16)

    _pipelined(heads * n_chunks, logits, finish, depth=1)


def _mla_attn(q, k, v, gates, batch, seq, tq):
    t = q.shape[0]
    heads = MLA_HEADS_PER_STEP
    gate_blk = A_WIDTH // (heads * B_V)
    return pl.pallas_call(
        functools.partial(_mla_attn_kernel, tq=tq),
        grid=(batch, B_HEADS // heads),
        in_specs=[
            pl.BlockSpec((seq, heads * B_QK_PAD), lambda b, hg: (b, hg)),
            pl.BlockSpec((seq, heads * B_QK_PAD), lambda b, hg: (b, hg)),
            pl.BlockSpec((seq, heads * B_V), lambda b, hg: (b, hg)),
            pl.BlockSpec((seq, heads * B_V), lambda b, hg: (b, gate_blk + hg)),
        ],
        out_specs=pl.BlockSpec((seq, heads * B_V), lambda b, hg: (b, hg)),
        out_shape=jax.ShapeDtypeStruct((t, B_WIDTH), _BF16),
        scratch_shapes=[pltpu.VMEM((heads, seq, 2 * B_V), _BF16)],
        compiler_params=_params(2),
        name="mla_attn",
    )(q, k, v, gates)


def _win_attn_kernel(sink_ref, slope_ref, q_ref, k_ref, v_ref, g_ref, o_ref, bias_ref, vext_ref):
    seq = q_ref.shape[0]
    nb = seq // A_BLOCK
    span = 3 * A_BLOCK
    _fill_v_ext(vext_ref, v_ref)

    @pl.when(pl.program_id(0) == 0)
    def _():
        qi = lax.broadcasted_iota(jnp.int32, (A_BLOCK, span), 0)
        si = lax.broadcasted_iota(jnp.int32, (A_BLOCK, span), 1)
        for variant in range(3):
            dist = jnp.abs(si - qi - variant * A_BLOCK)
            valid = dist <= A_WINDOW
            for head in range(A_HEADS):
                slope = slope_ref[head] * LOG2E
                bias_ref[head // A_GROUP, variant, (head % A_GROUP) * A_BLOCK:(head % A_GROUP + 1) * A_BLOCK, :] = (
                    jnp.where(valid, -(slope * dist.astype(_F32)), NEG_INF))

    sinks = [sink_ref[head] * LOG2E for head in range(A_HEADS)]

    def key_start(j):
        return min(max((j - 1) * A_BLOCK, 0), seq - span)

    def logits(i):
        kvh, j = divmod(i, nb)
        qrows = slice(j * A_BLOCK, (j + 1) * A_BLOCK)
        qs = jnp.concatenate(
            [q_ref[qrows, (kvh * A_GROUP + g) * HEAD_DIM:(kvh * A_GROUP + g + 1) * HEAD_DIM]
             for g in range(A_GROUP)], axis=0)
        return _dot_nt(qs, k_ref[key_start(j):key_start(j) + span, kvh * HEAD_DIM:(kvh + 1) * HEAD_DIM])

    def finish(i, s):
        kvh, j = divmod(i, nb)
        start = key_start(j)
        s = s + bias_ref[kvh, (j * A_BLOCK - start) // A_BLOCK]
        probs, sink_terms = [], []
        for g in range(A_GROUP):
            sink = sinks[kvh * A_GROUP + g]
            sg = s[g * A_BLOCK:(g + 1) * A_BLOCK]
            m = jnp.maximum(jnp.max(sg, axis=-1, keepdims=True), sink)
            probs.append(jnp.exp2(sg - m).astype(_BF16))
            sink_terms.append(jnp.exp2(sink - m))
        acc = _dot(jnp.concatenate(probs, axis=0), vext_ref[kvh, start:start + span, :])
        qrows = slice(j * A_BLOCK, (j + 1) * A_BLOCK)
        for g in range(A_GROUP):
            ag = acc[g * A_BLOCK:(g + 1) * A_BLOCK]
            o = ag[:, :HEAD_DIM] / (ag[:, HEAD_DIM:] + sink_terms[g])
            cols = slice((kvh * A_GROUP + g) * HEAD_DIM, (kvh * A_GROUP + g + 1) * HEAD_DIM)
            o_ref[qrows, cols] = (o * g_ref[qrows, cols].astype(_F32)).astype(_BF16)

    _pipelined(A_KV_HEADS * nb, logits, finish, depth=2)


def _win_attn(aq, akv, gates, sink, batch, seq):
    t = aq.shape[0]
    assert seq >= 3 * A_BLOCK
    slopes = jnp.exp2(-8.0 * jnp.arange(1, A_HEADS + 1, dtype=_F32) / A_HEADS)
    return pl.pallas_call(
        _win_attn_kernel,
        grid=(batch,),
        in_specs=[
            pl.BlockSpec(memory_space=pltpu.SMEM),
            pl.BlockSpec(memory_space=pltpu.SMEM),
            pl.BlockSpec((seq, A_WIDTH), lambda b: (b, 0)),
            pl.BlockSpec((seq, A_KV_WIDTH), lambda b: (b, 0)),
            pl.BlockSpec((seq, A_KV_WIDTH), lambda b: (b, 1)),
            pl.BlockSpec((seq, A_WIDTH), lambda b: (b, 0)),
        ],
        out_specs=pl.BlockSpec((seq, A_WIDTH), lambda b: (b, 0)),
        out_shape=jax.ShapeDtypeStruct((t, A_WIDTH), _BF16),
        scratch_shapes=[
            pltpu.VMEM((A_KV_HEADS, 3, A_GROUP * A_BLOCK, 3 * A_BLOCK), _F32),
            pltpu.VMEM((A_KV_HEADS, seq, 2 * HEAD_DIM), _BF16),
        ],
        compiler_params=_params(1),
        name="win_attn",
    )(sink, slopes, aq, akv, akv, gates)


def _na_tiles(rows):
    wr = min(NA_ROWS_MAX, rows)
    assert rows % NA_TILE_ROWS == 0 and rows >= NA_SPAN_ROWS
    starts, tile_pattern, patterns = [], [], []
    for t in range(rows // NA_TILE_ROWS):
        ws = min(max(NA_TILE_ROWS * t - wr // 2, 0), rows - NA_SPAN_ROWS)
        pattern = []
        for i in range(NA_TILE_ROWS):
            r = NA_TILE_ROWS * t + i
            rs = min(max(r - wr // 2, 0), rows - wr)
            assert ws <= rs and rs + wr <= ws + NA_SPAN_ROWS
            pattern.append((rs - ws, r - ws))
        pattern = tuple(pattern)
        if pattern not in patterns:
            patterns.append(pattern)
        starts.append(ws)
        tile_pattern.append(patterns.index(pattern))
    return starts, tile_pattern, patterns


def _na_pair_entry(first, q_off, pair, wr):
    n_dr = 2 * NA_ROWS_MAX - 1
    j0 = 2 * pair
    valid0 = first <= j0 < first + wr
    valid1 = first <= j0 + 1 < first + wr
    dr0 = j0 - q_off + NA_ROWS_MAX - 1
    if valid0 and valid1:
        return dr0
    if valid1:
        return (n_dr - 1) + dr0 + 1
    if valid0:
        return (n_dr - 1) + n_dr + dr0
    return (n_dr - 1) + 2 * n_dr


def _na_attn_kernel(q_ref, k_ref, v_ref, g_ref, bias_ref, o_ref, vext_ref):
    seq = q_ref.shape[0]
    rows = seq // GRID_W
    starts, tile_pattern, patterns = _na_tiles(rows)
    n_tiles = len(starts)
    tq = NA_TILE_ROWS * GRID_W
    span = NA_SPAN_ROWS * GRID_W
    wr = min(NA_ROWS_MAX, rows)
    _fill_v_ext(vext_ref, v_ref)

    heads = vext_ref.shape[0]

    def keys(t):
        return slice(starts[t] * GRID_W, starts[t] * GRID_W + span)

    def logits(i):
        h, t = divmod(i, n_tiles)
        cols = slice(h * HEAD_DIM, (h + 1) * HEAD_DIM)
        return _dot_nt(q_ref[t * tq:(t + 1) * tq, cols], k_ref[keys(t), cols])

    def add_bias(h, t, s):
        out_rows = []
        for i, (first, q_off) in enumerate(patterns[tile_pattern[t]]):
            out_rows.append(jnp.concatenate(
                [s[i * GRID_W:(i + 1) * GRID_W, m * 2 * GRID_W:(m + 1) * 2 * GRID_W]
                 + bias_ref[h, _na_pair_entry(first, q_off, m, wr)]
                 for m in range(NA_SPAN_ROWS // 2)], axis=1))
        return jnp.concatenate(out_rows, axis=0)

    def finish(i, s):
        h, t = divmod(i, n_tiles)
        s = add_bias(h, t, s)
        e = jnp.exp2(s - jnp.max(s, axis=-1, keepdims=True)).astype(_BF16)
        acc = _dot(e, vext_ref[h, keys(t), :])
        rows = slice(t * tq, (t + 1) * tq)
        cols = slice(h * HEAD_DIM, (h + 1) * HEAD_DIM)
        o = acc[:, :HEAD_DIM] / acc[:, HEAD_DIM:]
        o_ref[rows, cols] = (o * g_ref[rows, cols].astype(_F32)).astype(_BF16)

    _pipelined(heads * n_tiles, logits, finish, depth=2)


def _na_attn(cqkv, gates, bias, batch, seq, layer):
    t = cqkv.shape[0]
    heads = NA_HEADS_PER_STEP
    width = heads * HEAD_DIM
    groups = C_HEADS // heads
    gate_blk = (A_WIDTH + B_WIDTH) // width
    blk = lambda off: pl.BlockSpec((seq, width), lambda hg, b: (b, off + hg))
    return pl.pallas_call(
        _na_attn_kernel,
        grid=(groups, batch),
        in_specs=[
            blk(0), blk(groups), blk(2 * groups), blk(gate_blk),
            pl.BlockSpec((None, heads) + bias.shape[2:], lambda hg, b: (layer, hg, 0, 0, 0)),
        ],
        out_specs=pl.BlockSpec((seq, width), lambda hg, b: (b, hg)),
        out_shape=jax.ShapeDtypeStruct((t, C_WIDTH), _BF16),
        scratch_shapes=[pltpu.VMEM((heads, seq, 2 * HEAD_DIM), _BF16)],
        compiler_params=_params(2),
        name="na_attn",
    )(cqkv, cqkv, cqkv, gates, bias)


def _out_ple_kernel(ya_ref, yb_ref, yc_ref, h_ref, p_ref, wo_ref, gple_ref, wg_ref, wp_ref, gpost_ref,
                    o_ref):
    tm = h_ref.shape[0]
    halves = [slice(i * (tm // 2), (i + 1) * (tm // 2)) for i in range(2)]
    pp = [_rms(_dot(p_ref[rows, :].astype(_BF16), wp_ref[...]), gpost_ref[...]) for rows in halves]
    h1 = []
    for rows in halves:
        acc = _dot(ya_ref[rows, :], wo_ref[:A_WIDTH, :])
        acc += _dot(yb_ref[rows, :], wo_ref[A_WIDTH:A_WIDTH + B_WIDTH, :])
        acc += _dot(yc_ref[rows, :], wo_ref[A_WIDTH + B_WIDTH:, :])
        h1.append(h_ref[rows, :] + acc)
    logit = [_dot(_rms(x, gple_ref[...]).astype(_BF16), wg_ref[...]) for x in h1]
    for i, rows in enumerate(halves):
        o_ref[rows, :] = h1[i] + jax.nn.sigmoid(logit[i]) * pp[i]


def _out_ple(ya, yb, yc, h, p, lw, layer, tm):
    t = h.shape[0]
    row = lambda i: (i, 0)
    vec = lambda n: pl.BlockSpec((None, 1, n), lambda i: (layer, 0, 0))
    mat = lambda r, c: _resident((None, r, c), lambda i: (layer, 0, 0))
    return pl.pallas_call(
        _out_ple_kernel,
        grid=(t // tm,),
        in_specs=[
            pl.BlockSpec((tm, A_WIDTH), row),
            pl.BlockSpec((tm, B_WIDTH), row),
            pl.BlockSpec((tm, C_WIDTH), row),
            pl.BlockSpec((tm, D_MODEL), row),
            pl.BlockSpec((None, tm, PLE_DIM), lambda i: (layer, i, 0)),
            mat(MIX_WIDTH, D_MODEL), vec(D_MODEL), mat(D_MODEL, D_MODEL), mat(PLE_DIM, D_MODEL),
            vec(D_MODEL),
        ],
        out_specs=pl.BlockSpec((tm, D_MODEL), row),
        out_shape=jax.ShapeDtypeStruct((t, D_MODEL), _F32),
        compiler_params=_params(1),
        name="out_ple",
    )(ya, yb, yc, h, p, lw['w_out'], lw['ple_norm'], lw['w_ple_gate'], lw['w_ple_proj'],
      lw['ple_post_norm'])


def _swap_halves(x):
    half = x.shape[-1] // 2
    return jnp.concatenate([x[..., half:], x[..., :half]], axis=-1)


def _prepare_weights(W):
    depth = W['w_in'].shape[0]
    w_in_t = jnp.swapaxes(W['w_in'], 1, 2).astype(_BF16)
    assert w_in_t.shape[1] == IN_HI_START + IN_HI_WIDTH
    bkr_t = w_in_t[:, IN_LO_WIDTH:IN_HI_START]
    half = B_ROPE // 2
    w_in_kr_t = jnp.concatenate([bkr_t, bkr_t[:, half:], bkr_t[:, :half]], axis=1)

    wq = W['b_w_uq'].reshape(depth, B_Q_LORA, B_HEADS, B_QK)
    wq = jnp.concatenate([wq, _swap_halves(wq[..., B_NOPE:])], axis=-1)
    wq = wq.reshape(depth, B_Q_LORA, B_HEADS * B_QK_PAD).astype(_BF16)
    gq = W['b_q_norm']
    gk = W['b_k_norm']
    row = lambda g: g[:, None, :].astype(_F32)
    return {
        'norm_in': row(W['norm_in']),
        'w_in_t': w_in_t, 'w_in_kr_t': w_in_kr_t,
        'a_q_norm': row(W['a_q_norm']), 'a_k_norm': row(W['a_k_norm']), 'a_sink': W['a_sink'].astype(_F32),
        'b_cq_norm': row(W['b_cq_norm']), 'b_ckv_norm': row(W['b_ckv_norm']),
        'b_w_uq': wq, 'b_w_ukv': W['b_w_ukv'].astype(_BF16),
        'b_q_gain': row(jnp.concatenate([gq, _swap_halves(gq[:, B_NOPE:])], axis=-1)),
        'b_k_gain_nope': row(gk[:, :B_NOPE]),
        'b_k_gain_rope': row(jnp.concatenate([gk[:, B_NOPE:], _swap_halves(gk[:, B_NOPE:])], axis=-1)),
        'c_q_norm': row(W['c_q_norm']), 'c_k_norm': row(W['c_k_norm']),
        'w_out': W['w_out'].astype(_BF16), 'ple_norm': row(W['ple_norm']),
        'w_ple_gate': W['w_ple_gate'].astype(_BF16), 'w_ple_proj': W['w_ple_proj'].astype(_BF16),
        'ple_post_norm': row(W['ple_post_norm']),
    }


def _rope_tables(seq):
    half = B_ROPE // 2
    inv = ROPE_THETA ** (-jnp.arange(half, dtype=_F32) / half)
    ang = jnp.arange(seq).astype(_F32)[:, None] * inv[None, :]
    cos, sin = jnp.cos(ang), jnp.sin(ang)
    zeros = jnp.zeros((seq, B_ROPE), _F32)
    return {'cos2': jnp.concatenate([cos, cos, zeros], axis=-1),
            'sin2': jnp.concatenate([-sin, sin, zeros], axis=-1)}


def _na_bias_table(rpb):
    c = np.arange(GRID_W)
    cs = np.clip(c - NA_COLS // 2, 0, GRID_W - NA_COLS)
    col_valid = (c[None, :] >= cs[:, None]) & (c[None, :] < cs[:, None] + NA_COLS)
    pad = GRID_W - NA_COLS
    padded = jnp.pad(rpb.astype(_F32), ((0, 0), (0, 0), (0, 0), (pad, pad)), mode='edge')
    by_dc = jnp.stack([padded[..., GRID_W - 1 - qc:2 * GRID_W - 1 - qc] for qc in range(GRID_W)], axis=3)
    by_dc = jnp.where(col_valid, by_dc * LOG2E, NEG_INF)
    masked = jnp.full_like(by_dc, NEG_INF)
    return jnp.concatenate([
        jnp.concatenate([by_dc[:, :, :-1], by_dc[:, :, 1:]], axis=-1),
        jnp.concatenate([masked, by_dc], axis=-1),
        jnp.concatenate([by_dc, masked], axis=-1),
        jnp.concatenate([masked[:, :, :1], masked[:, :, :1]], axis=-1),
    ], axis=2)


def _trunk(x, p, lw, tables, na_bias):
    batch, seq, _ = x.shape
    depth = p.shape[0]
    t = batch * seq
    h = x.reshape(t, D_MODEL)
    p = p.reshape(depth, t, PLE_DIM)
    for layer in range(depth):
        aq, akv, gates, cqkv, bq, bk, bv = _in_proj(h, lw, tables, layer, seq, IN_PROJ_ROWS)
        ya = _win_attn(aq, akv, gates, lw['a_sink'][layer], batch, seq)
        yb = _mla_attn(bq, bk, bv, gates, batch, seq, MLA_QUERY_ROWS)
        yc = _na_attn(cqkv, gates, na_bias, batch, seq, layer)
        h = _out_ple(ya, yb, yc, h, p, lw, layer, OUT_PLE_ROWS)
    return h.reshape(batch, seq, D_MODEL)


def kernel(x_prompt, x_sample, p_prompt, p_sample, norm_in, w_in, a_q_norm, a_k_norm, a_sink, b_cq_norm, b_ckv_norm, b_w_uq, b_w_ukv, b_q_norm, b_k_norm, c_q_norm, c_k_norm, c_rpb, w_out, ple_norm, w_ple_gate, w_ple_proj, ple_post_norm):
    lw = _prepare_weights({
        'norm_in': norm_in, 'w_in': w_in, 'a_q_norm': a_q_norm, 'a_k_norm': a_k_norm, 'a_sink': a_sink,
        'b_cq_norm': b_cq_norm, 'b_ckv_norm': b_ckv_norm, 'b_w_uq': b_w_uq, 'b_w_ukv': b_w_ukv,
        'b_q_norm': b_q_norm, 'b_k_norm': b_k_norm, 'c_q_norm': c_q_norm, 'c_k_norm': c_k_norm,
        'w_out': w_out, 'ple_norm': ple_norm, 'w_ple_gate': w_ple_gate, 'w_ple_proj': w_ple_proj,
        'ple_post_norm': ple_post_norm,
    })
    na_bias = _na_bias_table(c_rpb)
    rope = {}
    outs = []
    for x, p in ((x_prompt, p_prompt), (x_sample, p_sample)):
        seq = x.shape[1]
        if seq not in rope:
            rope[seq] = _rope_tables(seq)
        outs.append(_trunk(x, p, lw, rope[seq], na_bias))
    return tuple(outs)
```

```python
import functools

import jax
import jax.numpy as jnp
import numpy as np
from jax import lax
from jax.experimental import pallas as pl
from jax.experimental.pallas import tpu as pltpu

D_MODEL = 2048
PLE_DIM = 256
HEAD_DIM = 128
A_HEADS = 6
A_KV_HEADS = 2
A_GROUP = A_HEADS // A_KV_HEADS
A_WINDOW = 128
A_BLOCK = 128
B_HEADS = 6
B_Q_LORA = 512
B_KV_LORA = 512
B_NOPE = 128
B_ROPE = 64
B_V = 128
B_QK = B_NOPE + B_ROPE
ROPE_THETA = 10000.0
C_HEADS = 4
GRID_W = 64
NA_ROWS_MAX = 8
NA_COLS = 16
A_WIDTH = A_HEADS * HEAD_DIM
A_KV_WIDTH = A_KV_HEADS * HEAD_DIM
B_WIDTH = B_HEADS * B_V
C_WIDTH = C_HEADS * HEAD_DIM
MIX_WIDTH = A_WIDTH + B_WIDTH + C_WIDTH
EPS = 1e-6
NEG_INF = -1e30
LOG2E = 1.4426950408889634

B_QK_PAD = 2 * HEAD_DIM
IN_A_Q = 0
IN_A_KV = IN_A_Q + A_WIDTH
IN_A_Z = IN_A_KV + 2 * A_KV_WIDTH
IN_B_LAT = IN_A_Z + A_WIDTH
IN_B_KR = IN_B_LAT + B_Q_LORA + B_KV_LORA
IN_B_Z = IN_B_KR + B_ROPE
IN_C_QKV = IN_B_Z + B_WIDTH
IN_C_Z = IN_C_QKV + 3 * C_WIDTH
IN_WIDTH = IN_C_Z + C_WIDTH

VMEM_LIMIT_BYTES = 56 * 1024 * 1024
IN_PROJ_ROWS = 256
OUT_PLE_ROWS = 512
MLA_QUERY_ROWS = 256
MLA_HEADS_PER_STEP = 3
NA_HEADS_PER_STEP = 4
NA_TILE_ROWS = 4
NA_SPAN_ROWS = 12

_BF16 = jnp.bfloat16
_F32 = jnp.float32


def _params(n_grid_dims):
    return pltpu.CompilerParams(
        dimension_semantics=("arbitrary",) * n_grid_dims,
        vmem_limit_bytes=VMEM_LIMIT_BYTES)


def _resident(shape, index_map):
    return pl.BlockSpec(shape, index_map, pipeline_mode=pl.Buffered(1))


def _rms(x, g):
    return x * lax.rsqrt(jnp.mean(x * x, axis=-1, keepdims=True) + EPS) * g


def _dot(a, b):
    return jnp.dot(a, b, preferred_element_type=_F32)


def _dot_nt(a, b):
    return lax.dot_general(a, b, (((1,), (1,)), ((), ())), preferred_element_type=_F32)


def _in_proj_kernel(h_ref, gin_ref, wt_ref, wkr_ref, gaq_ref, gak_ref, gbq_ref, gbkv_ref,
                    gcq_ref, gck_ref,
                    wq_ref, wkv_ref, gq_ref, gkn_ref, gkr_ref, cos_ref, sin_ref,
                    aq_ref, akv_ref, gates_ref, cqkv_ref, bq_ref, bk_ref, bv_ref):
    u = _rms(h_ref[...], gin_ref[...]).astype(_BF16)
    scale = HEAD_DIM ** -0.5 * LOG2E
    b_scale = B_QK ** -0.5 * LOG2E

    def seg(start, width):
        return _dot_nt(u, wt_ref[start:start + width, :])

    def heads(x, g, n, s=None):
        outs = []
        for i in range(n):
            y = _rms(x[:, i * HEAD_DIM:(i + 1) * HEAD_DIM], g)
            outs.append(y if s is None else y * s)
        return outs

    x = seg(IN_B_LAT, B_Q_LORA + B_KV_LORA)
    kr2 = _dot_nt(u, wkr_ref[...])
    cqn = _rms(x[:, :B_Q_LORA], gbq_ref[...]).astype(_BF16)
    ckvn = _rms(x[:, B_Q_LORA:], gbkv_ref[...]).astype(_BF16)

    out_col = 0
    for start, width in ((IN_A_Z, A_WIDTH), (IN_B_Z, B_WIDTH), (IN_C_Z, C_WIDTH)):
        z = seg(start, width)
        gates_ref[:, out_col:out_col + width] = (z * jax.nn.sigmoid(z)).astype(_BF16)
        out_col += width

    xq = _dot(cqn, wq_ref[...])
    xkv = _dot(ckvn, wkv_ref[...])

    xa = seg(IN_A_Q, A_WIDTH)
    xakv = seg(IN_A_KV, 2 * A_KV_WIDTH)
    xc = seg(IN_C_QKV, 3 * C_WIDTH)

    cos2 = cos_ref[...]
    sin2 = sin_ref[...]
    rope_lanes = lax.broadcasted_iota(jnp.int32, (1, HEAD_DIM), 1) < B_ROPE

    def rope(c):
        return c * cos2 + pltpu.roll(c, B_ROPE, 1) * sin2

    def sumsq(y, lanes=None):
        y2 = y * y
        return jnp.sum(y2 if lanes is None else jnp.where(lanes, y2, 0.0), axis=-1, keepdims=True)

    ss_kr = sumsq(kr2, rope_lanes)
    k_rope = rope(kr2 * gkr_ref[...])
    for hd in range(B_HEADS):
        lo, mid, hi = hd * B_QK_PAD, hd * B_QK_PAD + HEAD_DIM, (hd + 1) * B_QK_PAD
        a = xq[:, lo:mid]
        c = xq[:, mid:hi]
        r = lax.rsqrt((sumsq(a) + sumsq(c, rope_lanes)) * (1.0 / B_QK) + EPS)
        bq_ref[:, lo:mid] = (a * r * gq_ref[:, :HEAD_DIM] * b_scale).astype(_BF16)
        bq_ref[:, mid:hi] = (rope(c * r * gq_ref[:, HEAD_DIM:]) * b_scale).astype(_BF16)
        kn = xkv[:, lo:mid]
        r = lax.rsqrt((sumsq(kn) + ss_kr) * (1.0 / B_QK) + EPS)
        bk_ref[:, lo:mid] = (kn * r * gkn_ref[...]).astype(_BF16)
        bk_ref[:, mid:hi] = (k_rope * r).astype(_BF16)
        bv_ref[:, hd * B_V:(hd + 1) * B_V] = xkv[:, mid:hi].astype(_BF16)

    for i, y in enumerate(heads(xa, gaq_ref[...], A_HEADS, scale)):
        aq_ref[:, i * HEAD_DIM:(i + 1) * HEAD_DIM] = y.astype(_BF16)
    for i, y in enumerate(heads(xakv[:, :A_KV_WIDTH], gak_ref[...], A_KV_HEADS)):
        akv_ref[:, i * HEAD_DIM:(i + 1) * HEAD_DIM] = y.astype(_BF16)
    akv_ref[:, A_KV_WIDTH:] = xakv[:, A_KV_WIDTH:].astype(_BF16)
    for i, y in enumerate(heads(xc[:, :C_WIDTH], gcq_ref[...], C_HEADS, scale)):
        cqkv_ref[:, i * HEAD_DIM:(i + 1) * HEAD_DIM] = y.astype(_BF16)
    for i, y in enumerate(heads(xc[:, C_WIDTH:2 * C_WIDTH], gck_ref[...], C_HEADS)):
        cqkv_ref[:, C_WIDTH + i * HEAD_DIM:C_WIDTH + (i + 1) * HEAD_DIM] = y.astype(_BF16)
    cqkv_ref[:, 2 * C_WIDTH:] = xc[:, 2 * C_WIDTH:].astype(_BF16)


def _in_proj(h, lw, tables, layer, seq, tm):
    t = h.shape[0]
    row = lambda i: (i, 0)
    pos = lambda i: (i % (seq // tm), 0)
    vec = lambda n: pl.BlockSpec((None, 1, n), lambda i: (layer, 0, 0))
    mat = lambda r, c: _resident((None, r, c), lambda i: (layer, 0, 0))
    widths = (A_WIDTH, 2 * A_KV_WIDTH, MIX_WIDTH, 3 * C_WIDTH, B_HEADS * B_QK_PAD, B_HEADS * B_QK_PAD,
              B_WIDTH)
    return pl.pallas_call(
        _in_proj_kernel,
        grid=(t // tm,),
        in_specs=[
            pl.BlockSpec((tm, D_MODEL), row),
            vec(D_MODEL),
            mat(IN_WIDTH, D_MODEL), mat(2 * B_ROPE, D_MODEL),
            vec(HEAD_DIM), vec(HEAD_DIM), vec(B_Q_LORA), vec(B_KV_LORA), vec(HEAD_DIM), vec(HEAD_DIM),
            mat(B_Q_LORA, B_HEADS * B_QK_PAD), mat(B_KV_LORA, B_HEADS * (B_NOPE + B_V)),
            vec(B_QK_PAD), vec(HEAD_DIM), vec(2 * B_ROPE),
            pl.BlockSpec((tm, 2 * B_ROPE), pos),
            pl.BlockSpec((tm, 2 * B_ROPE), pos),
        ],
        out_specs=[pl.BlockSpec((tm, w), row) for w in widths],
        out_shape=[jax.ShapeDtypeStruct((t, w), _BF16) for w in widths],
        compiler_params=_params(1),
        name="in_proj",
    )(h, lw['norm_in'], lw['w_in_t'], lw['w_in_kr_t'], lw['a_q_norm'], lw['a_k_norm'],
      lw['b_cq_norm'], lw['b_ckv_norm'], lw['c_q_norm'], lw['c_k_norm'], lw['b_w_uq'], lw['b_w_ukv'], lw['b_q_gain'], lw['b_k_gain_nope'],
      lw['b_k_gain_rope'], tables['cos2'], tables['sin2'])


def _fill_v_ext(vext_ref, v_ref):
    heads, rows, ext = vext_ref.shape
    width = v_ref.shape[1] // heads
    for h in range(heads):
        vext_ref[h, :, :width] = v_ref[:, h * width:(h + 1) * width]
        vext_ref[h, :, width:] = jnp.ones((rows, ext - width), vext_ref.dtype)


def _pipelined(n, logits, finish, depth):
    queue = [logits(i) for i in range(min(depth, n))]
    for i in range(n):
        s = queue.pop(0)
        if i + depth < n:
            queue.append(logits(i + depth))
        finish(i, s)


def _mla_attn_kernel(q_ref, k_ref, v_ref, g_ref, o_ref, vext_ref, *, tq):
    heads = vext_ref.shape[0]
    n_chunks = q_ref.shape[0] // tq
    _fill_v_ext(vext_ref, v_ref)

    def logits(i):
        h, c = divmod(i, n_chunks)
        qk_cols = slice(h * B_QK_PAD, (h + 1) * B_QK_PAD)
        return _dot_nt(q_ref[c * tq:(c + 1) * tq, qk_cols], k_ref[:, qk_cols])

    def finish(i, s):
        h, c = divmod(i, n_chunks)
        e = jnp.exp2(s - jnp.max(s, axis=-1, keepdims=True)).astype(_BF16)
        acc = _dot(e, vext_ref[h])
        rows = slice(c * tq, (c + 1) * tq)
        cols = slice(h * B_V, (h + 1) * B_V)
        o = acc[:, :B_V] / acc[:, B_V:]
        o_ref[rows, cols] = (o * g_ref[rows, cols].astype(_F32)).astype(_BF16)

    _pipelined(heads * n_chunks, logits, finish, depth=1)


def _mla_attn(q, k, v, gates, batch, seq, tq):
    t = q.shape[0]
    heads = MLA_HEADS_PER_STEP
    gate_blk = A_WIDTH // (heads * B_V)
    return pl.pallas_call(
        functools.partial(_mla_attn_kernel, tq=tq),
        grid=(batch, B_HEADS // heads),
        in_specs=[
            pl.BlockSpec((seq, heads * B_QK_PAD), lambda b, hg: (b, hg)),
            pl.BlockSpec((seq, heads * B_QK_PAD), lambda b, hg: (b, hg)),
            pl.BlockSpec((seq, heads * B_V), lambda b, hg: (b, hg)),
            pl.BlockSpec((seq, heads * B_V), lambda b, hg: (b, gate_blk + hg)),
        ],
        out_specs=pl.BlockSpec((seq, heads * B_V), lambda b, hg: (b, hg)),
        out_shape=jax.ShapeDtypeStruct((t, B_WIDTH), _BF16),
        scratch_shapes=[pltpu.VMEM((heads, seq, 2 * B_V), _BF16)],
        compiler_params=_params(2),
        name="mla_attn",
    )(q, k, v, gates)


def _win_attn_kernel(sink_ref, slope_ref, q_ref, k_ref, v_ref, g_ref, o_ref, bias_ref, vext_ref):
    seq = q_ref.shape[0]
    nb = seq // A_BLOCK
    span = 3 * A_BLOCK
    _fill_v_ext(vext_ref, v_ref)

    @pl.when(pl.program_id(0) == 0)
    def _():
        qi = lax.broadcasted_iota(jnp.int32, (A_BLOCK, span), 0)
        si = lax.broadcasted_iota(jnp.int32, (A_BLOCK, span), 1)
        for variant in range(3):
            dist = jnp.abs(si - qi - variant * A_BLOCK)
            valid = dist <= A_WINDOW
            for head in range(A_HEADS):
                slope = slope_ref[head] * LOG2E
                bias_ref[head // A_GROUP, variant, (head % A_GROUP) * A_BLOCK:(head % A_GROUP + 1) * A_BLOCK, :] = (
                    jnp.where(valid, -(slope * dist.astype(_F32)), NEG_INF))

    sinks = [sink_ref[head] * LOG2E for head in range(A_HEADS)]

    def key_start(j):
        return min(max((j - 1) * A_BLOCK, 0), seq - span)

    def logits(i):
        kvh, j = divmod(i, nb)
        qrows = slice(j * A_BLOCK, (j + 1) * A_BLOCK)
        qs = jnp.concatenate(
            [q_ref[qrows, (kvh * A_GROUP + g) * HEAD_DIM:(kvh * A_GROUP + g + 1) * HEAD_DIM]
             for g in range(A_GROUP)], axis=0)
        return _dot_nt(qs, k_ref[key_start(j):key_start(j) + span, kvh * HEAD_DIM:(kvh + 1) * HEAD_DIM])

    def finish(i, s):
        kvh, j = divmod(i, nb)
        start = key_start(j)
        s = s + bias_ref[kvh, (j * A_BLOCK - start) // A_BLOCK]
        probs, sink_terms = [], []
        for g in range(A_GROUP):
            sink = sinks[kvh * A_GROUP + g]
            sg = s[g * A_BLOCK:(g + 1) * A_BLOCK]
            m = jnp.maximum(jnp.max(sg, axis=-1, keepdims=True), sink)
            probs.append(jnp.exp2(sg - m).astype(_BF16))
            sink_terms.append(jnp.exp2(sink - m))
        acc = _dot(jnp.concatenate(probs, axis=0), vext_ref[kvh, start:start + span, :])
        qrows = slice(j * A_BLOCK, (j + 1) * A_BLOCK)
        for g in range(A_GROUP):
            ag = acc[g * A_BLOCK:(g + 1) * A_BLOCK]
            o = ag[:, :HEAD_DIM] / (ag[:, HEAD_DIM:] + sink_terms[g])
            cols = slice((kvh * A_GROUP + g) * HEAD_DIM, (kvh * A_GROUP + g + 1) * HEAD_DIM)
            o_ref[qrows, cols] = (o * g_ref[qrows, cols].astype(_F32)).astype(_BF16)

    _pipelined(A_KV_HEADS * nb, logits, finish, depth=2)


def _win_attn(aq, akv, gates, sink, batch, seq):
    t = aq.shape[0]
    assert seq >= 3 * A_BLOCK
    slopes = jnp.exp2(-8.0 * jnp.arange(1, A_HEADS + 1, dtype=_F32) / A_HEADS)
    return pl.pallas_call(
        _win_attn_kernel,
        grid=(batch,),
        in_specs=[
            pl.BlockSpec(memory_space=pltpu.SMEM),
            pl.BlockSpec(memory_space=pltpu.SMEM),
            pl.BlockSpec((seq, A_WIDTH), lambda b: (b, 0)),
            pl.BlockSpec((seq, A_KV_WIDTH), lambda b: (b, 0)),
            pl.BlockSpec((seq, A_KV_WIDTH), lambda b: (b, 1)),
            pl.BlockSpec((seq, A_WIDTH), lambda b: (b, 0)),
        ],
        out_specs=pl.BlockSpec((seq, A_WIDTH), lambda b: (b, 0)),
        out_shape=jax.ShapeDtypeStruct((t, A_WIDTH), _BF16),
        scratch_shapes=[
            pltpu.VMEM((A_KV_HEADS, 3, A_GROUP * A_BLOCK, 3 * A_BLOCK), _F32),
            pltpu.VMEM((A_KV_HEADS, seq, 2 * HEAD_DIM), _BF16),
        ],
        compiler_params=_params(1),
        name="win_attn",
    )(sink, slopes, aq, akv, akv, gates)


def _na_tiles(rows):
    wr = min(NA_ROWS_MAX, rows)
    assert rows % NA_TILE_ROWS == 0 and rows >= NA_SPAN_ROWS
    starts, tile_pattern, patterns = [], [], []
    for t in range(rows // NA_TILE_ROWS):
        ws = min(max(NA_TILE_ROWS * t - wr // 2, 0), rows - NA_SPAN_ROWS)
        pattern = []
        for i in range(NA_TILE_ROWS):
            r = NA_TILE_ROWS * t + i
            rs = min(max(r - wr // 2, 0), rows - wr)
            assert ws <= rs and rs + wr <= ws + NA_SPAN_ROWS
            pattern.append((rs - ws, r - ws))
        pattern = tuple(pattern)
        if pattern not in patterns:
            patterns.append(pattern)
        starts.append(ws)
        tile_pattern.append(patterns.index(pattern))
    return starts, tile_pattern, patterns


def _na_pair_entry(first, q_off, pair, wr):
    n_dr = 2 * NA_ROWS_MAX - 1
    j0 = 2 * pair
    valid0 = first <= j0 < first + wr
    valid1 = first <= j0 + 1 < first + wr
    dr0 = j0 - q_off + NA_ROWS_MAX - 1
    if valid0 and valid1:
        return dr0
    if valid1:
        return (n_dr - 1) + dr0 + 1
    if valid0:
        return (n_dr - 1) + n_dr + dr0
    return (n_dr - 1) + 2 * n_dr


def _na_attn_kernel(q_ref, k_ref, v_ref, g_ref, bias_ref, o_ref, vext_ref):
    seq = q_ref.shape[0]
    rows = seq // GRID_W
    starts, tile_pattern, patterns = _na_tiles(rows)
    n_tiles = len(starts)
    tq = NA_TILE_ROWS * GRID_W
    span = NA_SPAN_ROWS * GRID_W
    wr = min(NA_ROWS_MAX, rows)
    _fill_v_ext(vext_ref, v_ref)

    heads = vext_ref.shape[0]

    def keys(t):
        return slice(starts[t] * GRID_W, starts[t] * GRID_W + span)

    def logits(i):
        h, t = divmod(i, n_tiles)
        cols = slice(h * HEAD_DIM, (h + 1) * HEAD_DIM)
        return _dot_nt(q_ref[t * tq:(t + 1) * tq, cols], k_ref[keys(t), cols])

    def add_bias(h, t, s):
        out_rows = []
        for i, (first, q_off) in enumerate(patterns[tile_pattern[t]]):
            out_rows.append(jnp.concatenate(
                [s[i * GRID_W:(i + 1) * GRID_W, m * 2 * GRID_W:(m + 1) * 2 * GRID_W]
                 + bias_ref[h, _na_pair_entry(first, q_off, m, wr)]
                 for m in range(NA_SPAN_ROWS // 2)], axis=1))
        return jnp.concatenate(out_rows, axis=0)

    def finish(i, s):
        h, t = divmod(i, n_tiles)
        s = add_bias(h, t, s)
        e = jnp.exp2(s - jnp.max(s, axis=-1, keepdims=True)).astype(_BF16)
        acc = _dot(e, vext_ref[h, keys(t), :])
        rows = slice(t * tq, (t + 1) * tq)
        cols = slice(h * HEAD_DIM, (h + 1) * HEAD_DIM)
        o = acc[:, :HEAD_DIM] / acc[:, HEAD_DIM:]
        o_ref[rows, cols] = (o * g_ref[rows, cols].astype(_F32)).astype(_BF16)

    _pipelined(heads * n_tiles, logits, finish, depth=2)


def _na_attn(cqkv, gates, bias, batch, seq, layer):
    t = cqkv.shape[0]
    heads = NA_HEADS_PER_STEP
    width = heads * HEAD_DIM
    groups = C_HEADS // heads
    gate_blk = (A_WIDTH + B_WIDTH) // width
    blk = lambda off: pl.BlockSpec((seq, width), lambda hg, b: (b, off + hg))
    return pl.pallas_call(
        _na_attn_kernel,
        grid=(groups, batch),
        in_specs=[
            blk(0), blk(groups), blk(2 * groups), blk(gate_blk),
            pl.BlockSpec((None, heads) + bias.shape[2:], lambda hg, b: (layer, hg, 0, 0, 0)),
        ],
        out_specs=pl.BlockSpec((seq, width), lambda hg, b: (b, hg)),
        out_shape=jax.ShapeDtypeStruct((t, C_WIDTH), _BF16),
        scratch_shapes=[pltpu.VMEM((heads, seq, 2 * HEAD_DIM), _BF16)],
        compiler_params=_params(2),
        name="na_attn",
    )(cqkv, cqkv, cqkv, gates, bias)


def _out_ple_kernel(ya_ref, yb_ref, yc_ref, h_ref, p_ref, wo_ref, gple_ref, wg_ref, wp_ref, gpost_ref,
                    o_ref):
    tm = h_ref.shape[0]
    halves = [slice(i * (tm // 2), (i + 1) * (tm // 2)) for i in range(2)]
    pp = [_rms(_dot(p_ref[rows, :].astype(_BF16), wp_ref[...]), gpost_ref[...]) for rows in halves]
    h1 = []
    for rows in halves:
        acc = _dot(ya_ref[rows, :], wo_ref[:A_WIDTH, :])
        acc += _dot(yb_ref[rows, :], wo_ref[A_WIDTH:A_WIDTH + B_WIDTH, :])
        acc += _dot(yc_ref[rows, :], wo_ref[A_WIDTH + B_WIDTH:, :])
        h1.append(h_ref[rows, :] + acc)
    logit = [_dot(_rms(x, gple_ref[...]).astype(_BF16), wg_ref[...]) for x in h1]
    for i, rows in enumerate(halves):
        o_ref[rows, :] = h1[i] + jax.nn.sigmoid(logit[i]) * pp[i]


def _out_ple(ya, yb, yc, h, p, lw, layer, tm):
    t = h.shape[0]
    row = lambda i: (i, 0)
    vec = lambda n: pl.BlockSpec((None, 1, n), lambda i: (layer, 0, 0))
    mat = lambda r, c: _resident((None, r, c), lambda i: (layer, 0, 0))
    return pl.pallas_call(
        _out_ple_kernel,
        grid=(t // tm,),
        in_specs=[
            pl.BlockSpec((tm, A_WIDTH), row),
            pl.BlockSpec((tm, B_WIDTH), row),
            pl.BlockSpec((tm, C_WIDTH), row),
            pl.BlockSpec((tm, D_MODEL), row),
            pl.BlockSpec((None, tm, PLE_DIM), lambda i: (layer, i, 0)),
            mat(MIX_WIDTH, D_MODEL), vec(D_MODEL), mat(D_MODEL, D_MODEL), mat(PLE_DIM, D_MODEL),
            vec(D_MODEL),
        ],
        out_specs=pl.BlockSpec((tm, D_MODEL), row),
        out_shape=jax.ShapeDtypeStruct((t, D_MODEL), _F32),
        compiler_params=_params(1),
        name="out_ple",
    )(ya, yb, yc, h, p, lw['w_out'], lw['ple_norm'], lw['w_ple_gate'], lw['w_ple_proj'],
      lw['ple_post_norm'])


def _swap_halves(x):
    half = x.shape[-1] // 2
    return jnp.concatenate([x[..., half:], x[..., :half]], axis=-1)


def _prepare_weights(W):
    depth = W['w_in'].shape[0]
    w_in_t = jnp.swapaxes(W['w_in'], 1, 2).astype(_BF16)
    assert w_in_t.shape[1] == IN_WIDTH
    bkr_t = w_in_t[:, IN_B_KR:IN_B_KR + B_ROPE]
    half = B_ROPE // 2
    w_in_kr_t = jnp.concatenate([bkr_t, bkr_t[:, half:], bkr_t[:, :half]], axis=1)

    wq = W['b_w_uq'].reshape(depth, B_Q_LORA, B_HEADS, B_QK)
    wq = jnp.concatenate([wq, _swap_halves(wq[..., B_NOPE:])], axis=-1)
    wq = wq.reshape(depth, B_Q_LORA, B_HEADS * B_QK_PAD).astype(_BF16)
    gq = W['b_q_norm']
    gk = W['b_k_norm']
    row = lambda g: g[:, None, :].astype(_F32)
    return {
        'norm_in': row(W['norm_in']),
        'w_in_t': w_in_t, 'w_in_kr_t': w_in_kr_t,
        'a_q_norm': row(W['a_q_norm']), 'a_k_norm': row(W['a_k_norm']), 'a_sink': W['a_sink'].astype(_F32),
        'b_cq_norm': row(W['b_cq_norm']), 'b_ckv_norm': row(W['b_ckv_norm']),
        'b_w_uq': wq, 'b_w_ukv': W['b_w_ukv'].astype(_BF16),
        'b_q_gain': row(jnp.concatenate([gq, _swap_halves(gq[:, B_NOPE:])], axis=-1)),
        'b_k_gain_nope': row(gk[:, :B_NOPE]),
        'b_k_gain_rope': row(jnp.concatenate([gk[:, B_NOPE:], _swap_halves(gk[:, B_NOPE:])], axis=-1)),
        'c_q_norm': row(W['c_q_norm']), 'c_k_norm': row(W['c_k_norm']),
        'w_out': W['w_out'].astype(_BF16), 'ple_norm': row(W['ple_norm']),
        'w_ple_gate': W['w_ple_gate'].astype(_BF16), 'w_ple_proj': W['w_ple_proj'].astype(_BF16),
        'ple_post_norm': row(W['ple_post_norm']),
    }


def _rope_tables(seq):
    half = B_ROPE // 2
    inv = ROPE_THETA ** (-jnp.arange(half, dtype=_F32) / half)
    ang = jnp.arange(seq).astype(_F32)[:, None] * inv[None, :]
    cos, sin = jnp.cos(ang), jnp.sin(ang)
    zeros = jnp.zeros((seq, B_ROPE), _F32)
    return {'cos2': jnp.concatenate([cos, cos, zeros], axis=-1),
            'sin2': jnp.concatenate([-sin, sin, zeros], axis=-1)}


def _na_bias_table(rpb):
    c = np.arange(GRID_W)
    cs = np.clip(c - NA_COLS // 2, 0, GRID_W - NA_COLS)
    col_valid = (c[None, :] >= cs[:, None]) & (c[None, :] < cs[:, None] + NA_COLS)
    pad = GRID_W - NA_COLS
    padded = jnp.pad(rpb.astype(_F32), ((0, 0), (0, 0), (0, 0), (pad, pad)), mode='edge')
    by_dc = jnp.stack([padded[..., GRID_W - 1 - qc:2 * GRID_W - 1 - qc] for qc in range(GRID_W)], axis=3)
    by_dc = jnp.where(col_valid, by_dc * LOG2E, NEG_INF)
    masked = jnp.full_like(by_dc, NEG_INF)
    return jnp.concatenate([
        jnp.concatenate([by_dc[:, :, :-1], by_dc[:, :, 1:]], axis=-1),
        jnp.concatenate([masked, by_dc], axis=-1),
        jnp.concatenate([by_dc, masked], axis=-1),
        jnp.concatenate([masked[:, :, :1], masked[:, :, :1]], axis=-1),
    ], axis=2)


def _trunk(x, p, lw, tables, na_bias):
    batch, seq, _ = x.shape
    depth = p.shape[0]
    t = batch * seq
    h = x.reshape(t, D_MODEL)
    p = p.reshape(depth, t, PLE_DIM)
    for layer in range(depth):
        aq, akv, gates, cqkv, bq, bk, bv = _in_proj(h, lw, tables, layer, seq, IN_PROJ_ROWS)
        ya = _win_attn(aq, akv, gates, lw['a_sink'][layer], batch, seq)
        yb = _mla_attn(bq, bk, bv, gates, batch, seq, MLA_QUERY_ROWS)
        yc = _na_attn(cqkv, gates, na_bias, batch, seq, layer)
        h = _out_ple(ya, yb, yc, h, p, lw, layer, OUT_PLE_ROWS)
    return h.reshape(batch, seq, D_MODEL)


def kernel(x_prompt, x_sample, p_prompt, p_sample, norm_in, w_in, a_q_norm, a_k_norm, a_sink, b_cq_norm, b_ckv_norm, b_w_uq, b_w_ukv, b_q_norm, b_k_norm, c_q_norm, c_k_norm, c_rpb, w_out, ple_norm, w_ple_gate, w_ple_proj, ple_post_norm):
    lw = _prepare_weights({
        'norm_in': norm_in, 'w_in': w_in, 'a_q_norm': a_q_norm, 'a_k_norm': a_k_norm, 'a_sink': a_sink,
        'b_cq_norm': b_cq_norm, 'b_ckv_norm': b_ckv_norm, 'b_w_uq': b_w_uq, 'b_w_ukv': b_w_ukv,
        'b_q_norm': b_q_norm, 'b_k_norm': b_k_norm, 'c_q_norm': c_q_norm, 'c_k_norm': c_k_norm,
        'w_out': w_out, 'ple_norm': ple_norm, 'w_ple_gate': w_ple_gate, 'w_ple_proj': w_ple_proj,
        'ple_post_norm': ple_post_norm,
    })
    na_bias = _na_bias_table(c_rpb)
    rope = {}
    outs = []
    for x, p in ((x_prompt, p_prompt), (x_sample, p_sample)):
        seq = x.shape[1]
        if seq not in rope:
            rope[seq] = _rope_tables(seq)
        outs.append(_trunk(x, p, lw, rope[seq], na_bias))
    return tuple(outs)
```

```python
import functools

import jax
import jax.numpy as jnp
import numpy as np
from jax import lax
from jax.experimental import pallas as pl
from jax.experimental.pallas import tpu as pltpu

D_MODEL = 2048
PLE_DIM = 256
HEAD_DIM = 128
A_HEADS = 6
A_KV_HEADS = 2
A_GROUP = A_HEADS // A_KV_HEADS
A_WINDOW = 128
A_BLOCK = 128
B_HEADS = 6
B_Q_LORA = 512
B_KV_LORA = 512
B_NOPE = 128
B_ROPE = 64
B_V = 128
B_QK = B_NOPE + B_ROPE
ROPE_THETA = 10000.0
C_HEADS = 4
GRID_W = 64
NA_ROWS_MAX = 8
NA_COLS = 16
A_WIDTH = A_HEADS * HEAD_DIM
A_KV_WIDTH = A_KV_HEADS * HEAD_DIM
B_WIDTH = B_HEADS * B_V
C_WIDTH = C_HEADS * HEAD_DIM
MIX_WIDTH = A_WIDTH + B_WIDTH + C_WIDTH
EPS = 1e-6
NEG_INF = -1e30
LOG2E = 1.4426950408889634

B_QK_PAD = 2 * HEAD_DIM
IN_A_Q = 0
IN_A_KV = IN_A_Q + A_WIDTH
IN_A_Z = IN_A_KV + 2 * A_KV_WIDTH
IN_B_LAT = IN_A_Z + A_WIDTH
IN_B_KR = IN_B_LAT + B_Q_LORA + B_KV_LORA
IN_B_Z = IN_B_KR + B_ROPE
IN_C_QKV = IN_B_Z + B_WIDTH
IN_C_Z = IN_C_QKV + 3 * C_WIDTH
IN_WIDTH = IN_C_Z + C_WIDTH

VMEM_LIMIT_BYTES = 56 * 1024 * 1024
IN_PROJ_ROWS = 256
OUT_PLE_ROWS = 512
MLA_QUERY_ROWS = 256
MLA_HEADS_PER_STEP = 3
NA_HEADS_PER_STEP = 4
NA_TILE_ROWS = 4
NA_SPAN_ROWS = 12

_BF16 = jnp.bfloat16
_F32 = jnp.float32


def _params(n_grid_dims):
    return pltpu.CompilerParams(
        dimension_semantics=("arbitrary",) * n_grid_dims,
        vmem_limit_bytes=VMEM_LIMIT_BYTES)


def _resident(shape, index_map):
    return pl.BlockSpec(shape, index_map, pipeline_mode=pl.Buffered(1))


def _rms(x, g):
    return x * lax.rsqrt(jnp.mean(x * x, axis=-1, keepdims=True) + EPS) * g


def _dot(a, b):
    return jnp.dot(a, b, preferred_element_type=_F32)


def _dot_nt(a, b):
    return lax.dot_general(a, b, (((1,), (1,)), ((), ())), preferred_element_type=_F32)


def _in_proj_kernel(h_ref, gin_ref, wt_ref, wkr_ref, gaq_ref, gak_ref, gbq_ref, gbkv_ref,
                    gcq_ref, gck_ref,
                    wq_ref, wkv_ref, gq_ref, gkn_ref, gkr_ref, cos_ref, sin_ref,
                    aq_ref, akv_ref, gates_ref, cqkv_ref, bq_ref, bk_ref, bv_ref):
    u = _rms(h_ref[...], gin_ref[...]).astype(_BF16)
    scale = HEAD_DIM ** -0.5 * LOG2E
    b_scale = B_QK ** -0.5 * LOG2E

    def seg(start, width):
        return _dot_nt(u, wt_ref[start:start + width, :])

    def heads(x, g, n, s=None):
        outs = []
        for i in range(n):
            y = _rms(x[:, i * HEAD_DIM:(i + 1) * HEAD_DIM], g)
            outs.append(y if s is None else y * s)
        return outs

    x = seg(IN_B_LAT, B_Q_LORA + B_KV_LORA)
    kr2 = _dot_nt(u, wkr_ref[...])
    cqn = _rms(x[:, :B_Q_LORA], gbq_ref[...]).astype(_BF16)
    ckvn = _rms(x[:, B_Q_LORA:], gbkv_ref[...]).astype(_BF16)

    out_col = 0
    for start, width in ((IN_A_Z, A_WIDTH), (IN_B_Z, B_WIDTH), (IN_C_Z, C_WIDTH)):
        z = seg(start, width)
        gates_ref[:, out_col:out_col + width] = (z * jax.nn.sigmoid(z)).astype(_BF16)
        out_col += width

    xq = _dot(cqn, wq_ref[...])
    xkv = _dot(ckvn, wkv_ref[...])

    xa = seg(IN_A_Q, A_WIDTH)
    xakv = seg(IN_A_KV, 2 * A_KV_WIDTH)
    xc = seg(IN_C_QKV, 3 * C_WIDTH)

    cos2 = cos_ref[...]
    sin2 = sin_ref[...]
    rope_lanes = lax.broadcasted_iota(jnp.int32, (1, HEAD_DIM), 1) < B_ROPE

    def rope(c):
        return c * cos2 + pltpu.roll(c, B_ROPE, 1) * sin2

    def sumsq(y, lanes=None):
        y2 = y * y
        return jnp.sum(y2 if lanes is None else jnp.where(lanes, y2, 0.0), axis=-1, keepdims=True)

    ss_kr = sumsq(kr2, rope_lanes)
    k_rope = rope(kr2 * gkr_ref[...])
    for hd in range(B_HEADS):
        lo, mid, hi = hd * B_QK_PAD, hd * B_QK_PAD + HEAD_DIM, (hd + 1) * B_QK_PAD
        a = xq[:, lo:mid]
        c = xq[:, mid:hi]
        r = lax.rsqrt((sumsq(a) + sumsq(c, rope_lanes)) * (1.0 / B_QK) + EPS)
        bq_ref[:, lo:mid] = (a * r * gq_ref[:, :HEAD_DIM] * b_scale).astype(_BF16)
        bq_ref[:, mid:hi] = (rope(c * r * gq_ref[:, HEAD_DIM:]) * b_scale).astype(_BF16)
        kn = xkv[:, lo:mid]
        r = lax.rsqrt((sumsq(kn) + ss_kr) * (1.0 / B_QK) + EPS)
        bk_ref[:, lo:mid] = (kn * r * gkn_ref[...]).astype(_BF16)
        bk_ref[:, mid:hi] = (k_rope * r).astype(_BF16)
        bv_ref[:, hd * B_V:(hd + 1) * B_V] = xkv[:, mid:hi].astype(_BF16)

    for i, y in enumerate(heads(xa, gaq_ref[...], A_HEADS, scale)):
        aq_ref[:, i * HEAD_DIM:(i + 1) * HEAD_DIM] = y.astype(_BF16)
    for i, y in enumerate(heads(xakv[:, :A_KV_WIDTH], gak_ref[...], A_KV_HEADS)):
        akv_ref[:, i * HEAD_DIM:(i + 1) * HEAD_DIM] = y.astype(_BF16)
    akv_ref[:, A_KV_WIDTH:] = xakv[:, A_KV_WIDTH:].astype(_BF16)
    for i, y in enumerate(heads(xc[:, :C_WIDTH], gcq_ref[...], C_HEADS, scale)):
        cqkv_ref[:, i * HEAD_DIM:(i + 1) * HEAD_DIM] = y.astype(_BF16)
    for i, y in enumerate(heads(xc[:, C_WIDTH:2 * C_WIDTH], gck_ref[...], C_HEADS)):
        cqkv_ref[:, C_WIDTH + i * HEAD_DIM:C_WIDTH + (i + 1) * HEAD_DIM] = y.astype(_BF16)
    cqkv_ref[:, 2 * C_WIDTH:] = xc[:, 2 * C_WIDTH:].astype(_BF16)


def _in_proj(h, lw, tables, layer, seq, tm):
    t = h.shape[0]
    row = lambda i: (i, 0)
    pos = lambda i: (i % (seq // tm), 0)
    vec = lambda n: pl.BlockSpec((None, 1, n), lambda i: (layer, 0, 0))
    mat = lambda r, c: _resident((None, r, c), lambda i: (layer, 0, 0))
    widths = (A_WIDTH, 2 * A_KV_WIDTH, MIX_WIDTH, 3 * C_WIDTH, B_HEADS * B_QK_PAD, B_HEADS * B_QK_PAD,
              B_WIDTH)
    return pl.pallas_call(
        _in_proj_kernel,
        grid=(t // tm,),
        in_specs=[
            pl.BlockSpec((tm, D_MODEL), row),
            vec(D_MODEL),
            mat(IN_WIDTH, D_MODEL), mat(2 * B_ROPE, D_MODEL),
            vec(HEAD_DIM), vec(HEAD_DIM), vec(B_Q_LORA), vec(B_KV_LORA), vec(HEAD_DIM), vec(HEAD_DIM),
            mat(B_Q_LORA, B_HEADS * B_QK_PAD), mat(B_KV_LORA, B_HEADS * (B_NOPE + B_V)),
            vec(B_QK_PAD), vec(HEAD_DIM), vec(2 * B_ROPE),
            pl.BlockSpec((tm, 2 * B_ROPE), pos),
            pl.BlockSpec((tm, 2 * B_ROPE), pos),
        ],
        out_specs=[pl.BlockSpec((tm, w), row) for w in widths],
        out_shape=[jax.ShapeDtypeStruct((t, w), _BF16) for w in widths],
        compiler_params=_params(1),
        name="in_proj",
    )(h, lw['norm_in'], lw['w_in_t'], lw['w_in_kr_t'], lw['a_q_norm'], lw['a_k_norm'],
      lw['b_cq_norm'], lw['b_ckv_norm'], lw['c_q_norm'], lw['c_k_norm'], lw['b_w_uq'], lw['b_w_ukv'], lw['b_q_gain'], lw['b_k_gain_nope'],
      lw['b_k_gain_rope'], tables['cos2'], tables['sin2'])


def _fill_v_ext(vext_ref, v_ref):
    heads, rows, ext = vext_ref.shape
    width = v_ref.shape[1] // heads
    for h in range(heads):
        vext_ref[h, :, :width] = v_ref[:, h * width:(h + 1) * width]
        vext_ref[h, :, width:] = jnp.ones((rows, ext - width), vext_ref.dtype)


def _pipelined(n, logits, finish, depth):
    queue = [logits(i) for i in range(min(depth, n))]
    for i in range(n):
        s = queue.pop(0)
        if i + depth < n:
            queue.append(logits(i + depth))
        finish(i, s)


def _mla_attn_kernel(q_ref, k_ref, v_ref, g_ref, o_ref, vext_ref, *, tq):
    heads = vext_ref.shape[0]
    n_chunks = q_ref.shape[0] // tq
    _fill_v_ext(vext_ref, v_ref)

    def logits(i):
        h, c = divmod(i, n_chunks)
        qk_cols = slice(h * B_QK_PAD, (h + 1) * B_QK_PAD)
        return _dot_nt(q_ref[c * tq:(c + 1) * tq, qk_cols], k_ref[:, qk_cols])

    def finish(i, s):
        h, c = divmod(i, n_chunks)
        e = jnp.exp2(s - jnp.max(s, axis=-1, keepdims=True)).astype(_BF16)
        acc = _dot(e, vext_ref[h])
        rows = slice(c * tq, (c + 1) * tq)
        cols = slice(h * B_V, (h + 1) * B_V)
        o = acc[:, :B_V] / acc[:, B_V:]
        o_ref[rows, cols] = (o * g_ref[rows, cols].astype(_F32)).astype(_BF16)

    _pipelined(heads * n_chunks, logits, finish, depth=1)


def _mla_attn(q, k, v, gates, batch, seq, tq):
    t = q.shape[0]
    heads = MLA_HEADS_PER_STEP
    gate_blk = A_WIDTH // (heads * B_V)
    return pl.pallas_call(
        functools.partial(_mla_attn_kernel, tq=tq),
        grid=(batch, B_HEADS // heads),
        in_specs=[
            pl.BlockSpec((seq, heads * B_QK_PAD), lambda b, hg: (b, hg)),
            pl.BlockSpec((seq, heads * B_QK_PAD), lambda b, hg: (b, hg)),
            pl.BlockSpec((seq, heads * B_V), lambda b, hg: (b, hg)),
            pl.BlockSpec((seq, heads * B_V), lambda b, hg: (b, gate_blk + hg)),
        ],
        out_specs=pl.BlockSpec((seq, heads * B_V), lambda b, hg: (b, hg)),
        out_shape=jax.ShapeDtypeStruct((t, B_WIDTH), _BF16),
        scratch_shapes=[pltpu.VMEM((heads, seq, 2 * B_V), _BF16)],
        compiler_params=_params(2),
        name="mla_attn",
    )(q, k, v, gates)


def _win_attn_kernel(sink_ref, slope_ref, q_ref, k_ref, v_ref, g_ref, o_ref, bias_ref, vext_ref):
    seq = q_ref.shape[0]
    nb = seq // A_BLOCK
    span = 3 * A_BLOCK
    _fill_v_ext(vext_ref, v_ref)

    @pl.when(pl.program_id(0) == 0)
    def _():
        qi = lax.broadcasted_iota(jnp.int32, (A_BLOCK, span), 0)
        si = lax.broadcasted_iota(jnp.int32, (A_BLOCK, span), 1)
        for variant in range(3):
            dist = jnp.abs(si - qi - variant * A_BLOCK)
            valid = dist <= A_WINDOW
            for head in range(A_HEADS):
                slope = slope_ref[head] * LOG2E
                bias_ref[head // A_GROUP, variant, (head % A_GROUP) * A_BLOCK:(head % A_GROUP + 1) * A_BLOCK, :] = (
                    jnp.where(valid, -(slope * dist.astype(_F32)), NEG_INF))

    sinks = [sink_ref[head] * LOG2E for head in range(A_HEADS)]

    def key_start(j):
        return min(max((j - 1) * A_BLOCK, 0), seq - span)

    def logits(i):
        kvh, j = divmod(i, nb)
        qrows = slice(j * A_BLOCK, (j + 1) * A_BLOCK)
        qs = jnp.concatenate(
            [q_ref[qrows, (kvh * A_GROUP + g) * HEAD_DIM:(kvh * A_GROUP + g + 1) * HEAD_DIM]
             for g in range(A_GROUP)], axis=0)
        return _dot_nt(qs, k_ref[key_start(j):key_start(j) + span, kvh * HEAD_DIM:(kvh + 1) * HEAD_DIM])

    def finish(i, s):
        kvh, j = divmod(i, nb)
        start = key_start(j)
        s = s + bias_ref[kvh, (j * A_BLOCK - start) // A_BLOCK]
        probs, sink_terms = [], []
        for g in range(A_GROUP):
            sink = sinks[kvh * A_GROUP + g]
            sg = s[g * A_BLOCK:(g + 1) * A_BLOCK]
            m = jnp.maximum(jnp.max(sg, axis=-1, keepdims=True), sink)
            probs.append(jnp.exp2(sg - m).astype(_BF16))
            sink_terms.append(jnp.exp2(sink - m))
        acc = _dot(jnp.concatenate(probs, axis=0), vext_ref[kvh, start:start + span, :])
        qrows = slice(j * A_BLOCK, (j + 1) * A_BLOCK)
        for g in range(A_GROUP):
            ag = acc[g * A_BLOCK:(g + 1) * A_BLOCK]
            o = ag[:, :HEAD_DIM] / (ag[:, HEAD_DIM:] + sink_terms[g])
            cols = slice((kvh * A_GROUP + g) * HEAD_DIM, (kvh * A_GROUP + g + 1) * HEAD_DIM)
            o_ref[qrows, cols] = (o * g_ref[qrows, cols].astype(_F32)).astype(_BF16)

    _pipelined(A_KV_HEADS * nb, logits, finish, depth=2)


def _win_attn(aq, akv, gates, sink, batch, seq):
    t = aq.shape[0]
    assert seq >= 3 * A_BLOCK
    slopes = jnp.exp2(-8.0 * jnp.arange(1, A_HEADS + 1, dtype=_F32) / A_HEADS)
    return pl.pallas_call(
        _win_attn_kernel,
        grid=(batch,),
        in_specs=[
            pl.BlockSpec(memory_space=pltpu.SMEM),
            pl.BlockSpec(memory_space=pltpu.SMEM),
            pl.BlockSpec((seq, A_WIDTH), lambda b: (b, 0)),
            pl.BlockSpec((seq, A_KV_WIDTH), lambda b: (b, 0)),
            pl.BlockSpec((seq, A_KV_WIDTH), lambda b: (b, 1)),
            pl.BlockSpec((seq, A_WIDTH), lambda b: (b, 0)),
        ],
        out_specs=pl.BlockSpec((seq, A_WIDTH), lambda b: (b, 0)),
        out_shape=jax.ShapeDtypeStruct((t, A_WIDTH), _BF16),
        scratch_shapes=[
            pltpu.VMEM((A_KV_HEADS, 3, A_GROUP * A_BLOCK, 3 * A_BLOCK), _F32),
            pltpu.VMEM((A_KV_HEADS, seq, 2 * HEAD_DIM), _BF16),
        ],
        compiler_params=_params(1),
        name="win_attn",
    )(sink, slopes, aq, akv, akv, gates)


def _na_tiles(rows):
    wr = min(NA_ROWS_MAX, rows)
    assert rows % NA_TILE_ROWS == 0 and rows >= NA_SPAN_ROWS
    starts, tile_pattern, patterns = [], [], []
    for t in range(rows // NA_TILE_ROWS):
        ws = min(max(NA_TILE_ROWS * t - wr // 2, 0), rows - NA_SPAN_ROWS)
        pattern = []
        for i in range(NA_TILE_ROWS):
            r = NA_TILE_ROWS * t + i
            rs = min(max(r - wr // 2, 0), rows - wr)
            assert ws <= rs and rs + wr <= ws + NA_SPAN_ROWS
            pattern.append((rs - ws, r - ws))
        pattern = tuple(pattern)
        if pattern not in patterns:
            patterns.append(pattern)
        starts.append(ws)
        tile_pattern.append(patterns.index(pattern))
    return starts, tile_pattern, patterns


def _na_pair_entry(first, q_off, pair, wr):
    n_dr = 2 * NA_ROWS_MAX - 1
    j0 = 2 * pair
    valid0 = first <= j0 < first + wr
    valid1 = first <= j0 + 1 < first + wr
    dr0 = j0 - q_off + NA_ROWS_MAX - 1
    if valid0 and valid1:
        return dr0
    if valid1:
        return (n_dr - 1) + dr0 + 1
    if valid0:
        return (n_dr - 1) + n_dr + dr0
    return (n_dr - 1) + 2 * n_dr


def _na_attn_kernel(q_ref, k_ref, v_ref, g_ref, bias_ref, o_ref, vext_ref):
    seq = q_ref.shape[0]
    rows = seq // GRID_W
    starts, tile_pattern, patterns = _na_tiles(rows)
    n_tiles = len(starts)
    tq = NA_TILE_ROWS * GRID_W
    span = NA_SPAN_ROWS * GRID_W
    wr = min(NA_ROWS_MAX, rows)
    _fill_v_ext(vext_ref, v_ref)

    heads = vext_ref.shape[0]

    def keys(t):
        return slice(starts[t] * GRID_W, starts[t] * GRID_W + span)

    def logits(i):
        h, t = divmod(i, n_tiles)
        cols = slice(h * HEAD_DIM, (h + 1) * HEAD_DIM)
        return _dot_nt(q_ref[t * tq:(t + 1) * tq, cols], k_ref[keys(t), cols])

    def add_bias(h, t, s):
        out_rows = []
        for i, (first, q_off) in enumerate(patterns[tile_pattern[t]]):
            out_rows.append(jnp.concatenate(
                [s[i * GRID_W:(i + 1) * GRID_W, m * 2 * GRID_W:(m + 1) * 2 * GRID_W]
                 + bias_ref[h, _na_pair_entry(first, q_off, m, wr)]
                 for m in range(NA_SPAN_ROWS // 2)], axis=1))
        return jnp.concatenate(out_rows, axis=0)

    def finish(i, s):
        h, t = divmod(i, n_tiles)
        s = add_bias(h, t, s)
        e = jnp.exp2(s - jnp.max(s, axis=-1, keepdims=True)).astype(_BF16)
        acc = _dot(e, vext_ref[h, keys(t), :])
        rows = slice(t * tq, (t + 1) * tq)
        cols = slice(h * HEAD_DIM, (h + 1) * HEAD_DIM)
        o = acc[:, :HEAD_DIM] / acc[:, HEAD_DIM:]
        o_ref[rows, cols] = (o * g_ref[rows, cols].astype(_F32)).astype(_BF16)

    _pipelined(heads * n_tiles, logits, finish, depth=2)


def _na_attn(cqkv, gates, bias, batch, seq, layer):
    t = cqkv.shape[0]
    heads = NA_HEADS_PER_STEP
    width = heads * HEAD_DIM
    groups = C_HEADS // heads
    gate_blk = (A_WIDTH + B_WIDTH) // width
    blk = lambda off: pl.BlockSpec((seq, width), lambda hg, b: (b, off + hg))
    return pl.pallas_call(
        _na_attn_kernel,
        grid=(groups, batch),
        in_specs=[
            blk(0), blk(groups), blk(2 * groups), blk(gate_blk),
            pl.BlockSpec((None, heads) + bias.shape[2:], lambda hg, b: (layer, hg, 0, 0, 0)),
        ],
        out_specs=pl.BlockSpec((seq, width), lambda hg, b: (b, hg)),
        out_shape=jax.ShapeDtypeStruct((t, C_WIDTH), _BF16),
        scratch_shapes=[pltpu.VMEM((heads, seq, 2 * HEAD_DIM), _BF16)],
        compiler_params=_params(2),
        name="na_attn",
    )(cqkv, cqkv, cqkv, gates, bias)


def _out_ple_kernel(ya_ref, yb_ref, yc_ref, h_ref, p_ref, wo_ref, gple_ref, wg_ref, wp_ref, gpost_ref,
                    o_ref):
    tm = h_ref.shape[0]
    halves = [slice(i * (tm // 2), (i + 1) * (tm // 2)) for i in range(2)]
    h1 = []
    for rows in halves:
        acc = _dot(ya_ref[rows, :], wo_ref[:A_WIDTH, :])
        acc += _dot(yb_ref[rows, :], wo_ref[A_WIDTH:A_WIDTH + B_WIDTH, :])
        acc += _dot(yc_ref[rows, :], wo_ref[A_WIDTH + B_WIDTH:, :])
        h1.append(h_ref[rows, :] + acc)
    pp = [_rms(_dot(p_ref[rows, :].astype(_BF16), wp_ref[...]), gpost_ref[...]) for rows in halves]
    logit = [_dot(_rms(x, gple_ref[...]).astype(_BF16), wg_ref[...]) for x in h1]
    for i, rows in enumerate(halves):
        o_ref[rows, :] = h1[i] + jax.nn.sigmoid(logit[i]) * pp[i]


def _out_ple(ya, yb, yc, h, p, lw, layer, tm):
    t = h.shape[0]
    row = lambda i: (i, 0)
    vec = lambda n: pl.BlockSpec((None, 1, n), lambda i: (layer, 0, 0))
    mat = lambda r, c: _resident((None, r, c), lambda i: (layer, 0, 0))
    return pl.pallas_call(
        _out_ple_kernel,
        grid=(t // tm,),
        in_specs=[
            pl.BlockSpec((tm, A_WIDTH), row),
            pl.BlockSpec((tm, B_WIDTH), row),
            pl.BlockSpec((tm, C_WIDTH), row),
            pl.BlockSpec((tm, D_MODEL), row),
            pl.BlockSpec((None, tm, PLE_DIM), lambda i: (layer, i, 0)),
            mat(MIX_WIDTH, D_MODEL), vec(D_MODEL), mat(D_MODEL, D_MODEL), mat(PLE_DIM, D_MODEL),
            vec(D_MODEL),
        ],
        out_specs=pl.BlockSpec((tm, D_MODEL), row),
        out_shape=jax.ShapeDtypeStruct((t, D_MODEL), _F32),
        compiler_params=_params(1),
        name="out_ple",
    )(ya, yb, yc, h, p, lw['w_out'], lw['ple_norm'], lw['w_ple_gate'], lw['w_ple_proj'],
      lw['ple_post_norm'])


def _swap_halves(x):
    half = x.shape[-1] // 2
    return jnp.concatenate([x[..., half:], x[..., :half]], axis=-1)


def _prepare_weights(W):
    depth = W['w_in'].shape[0]
    w_in_t = jnp.swapaxes(W['w_in'], 1, 2).astype(_BF16)
    assert w_in_t.shape[1] == IN_WIDTH
    bkr_t = w_in_t[:, IN_B_KR:IN_B_KR + B_ROPE]
    half = B_ROPE // 2
    w_in_kr_t = jnp.concatenate([bkr_t, bkr_t[:, half:], bkr_t[:, :half]], axis=1)

    wq = W['b_w_uq'].reshape(depth, B_Q_LORA, B_HEADS, B_QK)
    wq = jnp.concatenate([wq, _swap_halves(wq[..., B_NOPE:])], axis=-1)
    wq = wq.reshape(depth, B_Q_LORA, B_HEADS * B_QK_PAD).astype(_BF16)
    gq = W['b_q_norm']
    gk = W['b_k_norm']
    row = lambda g: g[:, None, :].astype(_F32)
    return {
        'norm_in': row(W['norm_in']),
        'w_in_t': w_in_t, 'w_in_kr_t': w_in_kr_t,
        'a_q_norm': row(W['a_q_norm']), 'a_k_norm': row(W['a_k_norm']), 'a_sink': W['a_sink'].astype(_F32),
        'b_cq_norm': row(W['b_cq_norm']), 'b_ckv_norm': row(W['b_ckv_norm']),
        'b_w_uq': wq, 'b_w_ukv': W['b_w_ukv'].astype(_BF16),
        'b_q_gain': row(jnp.concatenate([gq, _swap_halves(gq[:, B_NOPE:])], axis=-1)),
        'b_k_gain_nope': row(gk[:, :B_NOPE]),
        'b_k_gain_rope': row(jnp.concatenate([gk[:, B_NOPE:], _swap_halves(gk[:, B_NOPE:])], axis=-1)),
        'c_q_norm': row(W['c_q_norm']), 'c_k_norm': row(W['c_k_norm']),
        'w_out': W['w_out'].astype(_BF16), 'ple_norm': row(W['ple_norm']),
        'w_ple_gate': W['w_ple_gate'].astype(_BF16), 'w_ple_proj': W['w_ple_proj'].astype(_BF16),
        'ple_post_norm': row(W['ple_post_norm']),
    }


def _rope_tables(seq):
    half = B_ROPE // 2
    inv = ROPE_THETA ** (-jnp.arange(half, dtype=_F32) / half)
    ang = jnp.arange(seq).astype(_F32)[:, None] * inv[None, :]
    cos, sin = jnp.cos(ang), jnp.sin(ang)
    zeros = jnp.zeros((seq, B_ROPE), _F32)
    return {'cos2': jnp.concatenate([cos, cos, zeros], axis=-1),
            'sin2': jnp.concatenate([-sin, sin, zeros], axis=-1)}


def _na_bias_table(rpb):
    c = np.arange(GRID_W)
    cs = np.clip(c - NA_COLS // 2, 0, GRID_W - NA_COLS)
    col_valid = (c[None, :] >= cs[:, None]) & (c[None, :] < cs[:, None] + NA_COLS)
    pad = GRID_W - NA_COLS
    padded = jnp.pad(rpb.astype(_F32), ((0, 0), (0, 0), (0, 0), (pad, pad)), mode='edge')
    by_dc = jnp.stack([padded[..., GRID_W - 1 - qc:2 * GRID_W - 1 - qc] for qc in range(GRID_W)], axis=3)
    by_dc = jnp.where(col_valid, by_dc * LOG2E, NEG_INF)
    masked = jnp.full_like(by_dc, NEG_INF)
    return jnp.concatenate([
        jnp.concatenate([by_dc[:, :, :-1], by_dc[:, :, 1:]], axis=-1),
        jnp.concatenate([masked, by_dc], axis=-1),
        jnp.concatenate([by_dc, masked], axis=-1),
        jnp.concatenate([masked[:, :, :1], masked[:, :, :1]], axis=-1),
    ], axis=2)


def _trunk(x, p, lw, tables, na_bias):
    batch, seq, _ = x.shape
    depth = p.shape[0]
    t = batch * seq
    h = x.reshape(t, D_MODEL)
    p = p.reshape(depth, t, PLE_DIM)
    for layer in range(depth):
        aq, akv, gates, cqkv, bq, bk, bv = _in_proj(h, lw, tables, layer, seq, IN_PROJ_ROWS)
        ya = _win_attn(aq, akv, gates, lw['a_sink'][layer], batch, seq)
        yb = _mla_attn(bq, bk, bv, gates, batch, seq, MLA_QUERY_ROWS)
        yc = _na_attn(cqkv, gates, na_bias, batch, seq, layer)
        h = _out_ple(ya, yb, yc, h, p, lw, layer, OUT_PLE_ROWS)
    return h.reshape(batch, seq, D_MODEL)


def kernel(x_prompt, x_sample, p_prompt, p_sample, norm_in, w_in, a_q_norm, a_k_norm, a_sink, b_cq_norm, b_ckv_norm, b_w_uq, b_w_ukv, b_q_norm, b_k_norm, c_q_norm, c_k_norm, c_rpb, w_out, ple_norm, w_ple_gate, w_ple_proj, ple_post_norm):
    lw = _prepare_weights({
        'norm_in': norm_in, 'w_in': w_in, 'a_q_norm': a_q_norm, 'a_k_norm': a_k_norm, 'a_sink': a_sink,
        'b_cq_norm': b_cq_norm, 'b_ckv_norm': b_ckv_norm, 'b_w_uq': b_w_uq, 'b_w_ukv': b_w_ukv,
        'b_q_norm': b_q_norm, 'b_k_norm': b_k_norm, 'c_q_norm': c_q_norm, 'c_k_norm': c_k_norm,
        'w_out': w_out, 'ple_norm': ple_norm, 'w_ple_gate': w_ple_gate, 'w_ple_proj': w_ple_proj,
        'ple_post_norm': ple_post_norm,
    })
    na_bias = _na_bias_table(c_rpb)
    rope = {}
    outs = []
    for x, p in ((x_prompt, p_prompt), (x_sample, p_sample)):
        seq = x.shape[1]
        if seq not in rope:
            rope[seq] = _rope_tables(seq)
        outs.append(_trunk(x, p, lw, rope[seq], na_bias))
    return tuple(outs)
```
